```python
import math
import jax, jax.numpy as jnp
from jax import lax
import numpy as np

D_MODEL = 1024
BATCH = 4
SEQ = 8192
DEPTH = 1

N_ATTN_HEADS = 8
HEAD_DIM = 64
ATTN_WIDTH = N_ATTN_HEADS * HEAD_DIM
ROT_DIM = HEAD_DIM // 4
ROPE_THETA = 500000.0
MOBA_BLOCK = 256
MOBA_TOP_K = 3
Q_CHUNK = 64
SSM_WIDTH = D_MODEL // 2
SSM_GROUP = 16
N_SSM_GROUPS = SSM_WIDTH // SSM_GROUP
SSM_STATE = 64
DT_MIN = 1e-3
DT_MAX = 1e-1
D_FF = 2816
CONV_WIDTH = 3
IN_WIDTH = 3 * ATTN_WIDTH + SSM_WIDTH + 2 * D_MODEL
LN_EPS = 1e-5
DEEPNORM_ALPHA = (2.0 * DEPTH) ** 0.25
DEEPNORM_BETA = (8.0 * DEPTH) ** -0.25
NEG_INF = -1e30

kernel_name = 'moba_s5_gated_hybrid_deepnorm_block'


def layer_norm(t, g, b):
    tf = t.astype(jnp.float32)
    mu = jnp.mean(tf, axis=-1, keepdims=True)
    var = jnp.mean(jnp.square(tf - mu), axis=-1, keepdims=True)
    y = (tf - mu) * lax.rsqrt(var + LN_EPS) * g.astype(jnp.float32) + b.astype(jnp.float32)
    return y.astype(t.dtype)


def rotary_tables(seq_len):
    inv_freq = ROPE_THETA ** (-jnp.arange(0, ROT_DIM, 2, dtype=jnp.float32) / ROT_DIM)
    pos = jnp.arange(seq_len, dtype=jnp.float32)
    ang = pos[:, None] * inv_freq[None, :]
    return jnp.cos(ang), jnp.sin(ang)


def apply_partial_rotary(t, cos, sin):
    half = ROT_DIM // 2
    r1 = t[..., :half]
    r2 = t[..., half:ROT_DIM]
    rest = t[..., ROT_DIM:]
    c = cos[None, :, None, :].astype(t.dtype)
    s = sin[None, :, None, :].astype(t.dtype)
    return jnp.concatenate([r1 * c - r2 * s, r1 * s + r2 * c, rest], axis=-1)


def moba_attention(q, k, v):
    bsz, seq, n_heads, dh = q.shape
    n_blocks = -(-seq // MOBA_BLOCK)
    seq_pad = n_blocks * MOBA_BLOCK
    pad = ((0, 0), (0, 0), (0, seq_pad - seq), (0, 0))
    qh = jnp.pad(q.transpose(0, 2, 1, 3), pad)
    kh = jnp.pad(k.transpose(0, 2, 1, 3), pad)
    vh = jnp.pad(v.transpose(0, 2, 1, 3), pad)
    kb = kh.reshape(bsz, n_heads, n_blocks, MOBA_BLOCK, dh)
    vb = vh.reshape(bsz, n_heads, n_blocks, MOBA_BLOCK, dh)
    k_mean = jnp.mean(kb.astype(jnp.float32), axis=3)
    top_k = min(MOBA_TOP_K, n_blocks)
    n_sel = top_k + 1
    b_idx = jnp.arange(bsz)[:, None, None, None]
    h_idx = jnp.arange(n_heads)[None, :, None, None]
    blk_ids = jnp.arange(n_blocks)
    k_offs = jnp.arange(MOBA_BLOCK)
    q_offs = jnp.arange(Q_CHUNK)
    scale = dh ** -0.5

    def one_chunk(ci):
        q0 = ci * Q_CHUNK
        own = q0 // MOBA_BLOCK
        qc = lax.dynamic_slice_in_dim(qh, q0, Q_CHUNK, axis=2)
        gate = jnp.einsum('bhqd,bhnd->bhqn', qc.astype(jnp.float32), k_mean)
        gate = jnp.where(blk_ids < own, gate, NEG_INF)
        _, sel = lax.top_k(gate, top_k)
        sel_valid = sel < own
        own_idx = jnp.broadcast_to(own, sel.shape[:-1] + (1,)).astype(sel.dtype)
        idx = jnp.concatenate([sel, own_idx], axis=-1)
        kg = kb[b_idx, h_idx, idx]
        vg = vb[b_idx, h_idx, idx]
        s = jnp.einsum('bhqd,bhqnkd->bhqnk', qc, kg, preferred_element_type=jnp.float32) * scale
        causal = (own * MOBA_BLOCK + k_offs)[None, :] <= (q0 + q_offs)[:, None]
        mask = jnp.concatenate([
            jnp.broadcast_to(sel_valid[..., None], sel.shape + (MOBA_BLOCK,)),
            jnp.broadcast_to(causal[None, None, :, None, :], (bsz, n_heads, Q_CHUNK, 1, MOBA_BLOCK)),
        ], axis=3)
        s = jnp.where(mask, s, NEG_INF)
        p = jax.nn.softmax(s.reshape(bsz, n_heads, Q_CHUNK, n_sel * MOBA_BLOCK), axis=-1)
        vflat = vg.reshape(bsz, n_heads, Q_CHUNK, n_sel * MOBA_BLOCK, dh)
        return jnp.einsum('bhqm,bhqmd->bhqd', p.astype(vflat.dtype), vflat)

    out = lax.map(one_chunk, jnp.arange(seq_pad // Q_CHUNK))
    out = out.transpose(1, 2, 0, 3, 4).reshape(bsz, n_heads, seq_pad, dh)[:, :, :seq]
    return out.transpose(0, 2, 1, 3).reshape(bsz, seq, n_heads * dh)


def s5_branch(u, a_re, a_im, log_dt, b_re, b_im, c_re, c_im, d_skip, w_glu):
    bsz, seq, _ = u.shape
    f32 = jnp.float32
    uf = u.astype(f32).reshape(bsz, seq, N_SSM_GROUPS, SSM_GROUP)
    a_re = a_re.astype(f32)
    a_im = a_im.astype(f32)
    dt = jnp.exp(log_dt.astype(f32))[:, None]
    mag = jnp.exp(a_re * dt)
    lb_re = mag * jnp.cos(a_im * dt)
    lb_im = mag * jnp.sin(a_im * dt)
    den = a_re * a_re + a_im * a_im
    n_re = lb_re - 1.0
    n_im = lb_im
    z_re = (n_re * a_re + n_im * a_im) / den
    z_im = (n_im * a_re - n_re * a_im) / den
    bu_re = jnp.einsum('gph,bsgh->bsgp', b_re.astype(f32), uf)
    bu_im = jnp.einsum('gph,bsgh->bsgp', b_im.astype(f32), uf)
    x_re = z_re * bu_re - z_im * bu_im
    x_im = z_re * bu_im + z_im * bu_re
    a_el_re = jnp.broadcast_to(lb_re, (1, seq, N_SSM_GROUPS, SSM_STATE))
    a_el_im = jnp.broadcast_to(lb_im, (1, seq, N_SSM_GROUPS, SSM_STATE))

    def combine(e1, e2):
        a1r, a1i, b1r, b1i = e1
        a2r, a2i, b2r, b2i = e2
        return (a2r * a1r - a2i * a1i,
                a2r * a1i + a2i * a1r,
                a2r * b1r - a2i * b1i + b2r,
                a2r * b1i + a2i * b1r + b2i)

    _, _, h_re, h_im = lax.associative_scan(combine, (a_el_re, a_el_im, x_re, x_im), axis=1)
    y = (jnp.einsum('ghp,bsgp->bsgh', c_re.astype(f32), h_re)
         - jnp.einsum('ghp,bsgp->bsgh', c_im.astype(f32), h_im)
         + d_skip.astype(f32) * uf)
    y = jax.nn.gelu(y.reshape(bsz, seq, SSM_WIDTH))
    y = y * jax.nn.sigmoid(y @ w_glu.astype(f32))
    return y.astype(u.dtype)


def causal_depthwise_conv(t, w, b):
    seq = t.shape[1]
    tp = jnp.pad(t, ((0, 0), (CONV_WIDTH - 1, 0), (0, 0)))
    out = b + tp[:, 0:seq, :] * w[0]
    for j in range(1, CONV_WIDTH):
        out = out + tp[:, j:j + seq, :] * w[j]
    return out


def setup_inputs(seed: int = 0) -> dict:
    key = jax.random.key(seed)
    ks = jax.random.split(key, 24)
    f32 = jnp.float32
    L = DEPTH

    def dense(k, fan_in, fan_out, scale=1.0):
        return jax.random.normal(k, (L, fan_in, fan_out), f32) * (scale * fan_in ** -0.5)

    x = jax.random.normal(ks[0], (BATCH, SEQ, D_MODEL), f32)
    col_scale = jnp.concatenate([
        jnp.ones((2 * ATTN_WIDTH,), f32),
        jnp.full((ATTN_WIDTH + SSM_WIDTH,), DEEPNORM_BETA, f32),
        jnp.ones((2 * D_MODEL,), f32)])
    w_in = dense(ks[1], D_MODEL, IN_WIDTH) * col_scale
    w_attn_proj = dense(ks[2], ATTN_WIDTH, D_MODEL)
    n = jnp.arange(SSM_STATE, dtype=f32)
    gp = (L, N_SSM_GROUPS, SSM_STATE)
    ssm_a_re = -0.5 + 0.01 * jax.random.normal(ks[3], gp, f32)
    ssm_a_im = math.pi * n + 0.01 * jax.random.normal(ks[4], gp, f32)
    ssm_log_dt = jax.random.uniform(ks[5], (L, N_SSM_GROUPS), f32,
                                    minval=math.log(DT_MIN), maxval=math.log(DT_MAX))
    bshape = (L, N_SSM_GROUPS, SSM_STATE, SSM_GROUP)
    cshape = (L, N_SSM_GROUPS, SSM_GROUP, SSM_STATE)
    bscale = (2.0 * SSM_GROUP) ** -0.5
    cscale = (2.0 * SSM_STATE) ** -0.5
    ssm_b_re = jax.random.normal(ks[6], bshape, f32) * bscale
    ssm_b_im = jax.random.normal(ks[7], bshape, f32) * bscale
    ssm_c_re = jax.random.normal(ks[8], cshape, f32) * cscale
    ssm_c_im = jax.random.normal(ks[9], cshape, f32) * cscale
    ssm_d = jax.random.normal(ks[10], (L, N_SSM_GROUPS, SSM_GROUP), f32)
    w_glu = dense(ks[11], SSM_WIDTH, SSM_WIDTH)
    w_ssm_proj = dense(ks[12], SSM_WIDTH, D_MODEL)
    w_out = dense(ks[13], D_MODEL, D_MODEL, DEEPNORM_BETA)
    ln1_g = 1.0 + 0.02 * jax.random.normal(ks[14], (L, D_MODEL), f32)
    ln1_b = 0.02 * jax.random.normal(ks[15], (L, D_MODEL), f32)
    w_up = dense(ks[16], D_MODEL, 2 * D_FF, DEEPNORM_BETA)
    conv_w = jax.random.normal(ks[17], (L, CONV_WIDTH, 2 * D_FF), f32) * CONV_WIDTH ** -0.5
    conv_b = 0.01 * jax.random.normal(ks[18], (L, 2 * D_FF), f32)
    w_down = dense(ks[19], D_FF, D_MODEL, DEEPNORM_BETA)
    ln2_g = 1.0 + 0.02 * jax.random.normal(ks[20], (L, D_MODEL), f32)
    ln2_b = 0.02 * jax.random.normal(ks[21], (L, D_MODEL), f32)
    return {'x': x, 'w_in': w_in, 'w_attn_proj': w_attn_proj,
            'ssm_a_re': ssm_a_re, 'ssm_a_im': ssm_a_im, 'ssm_log_dt': ssm_log_dt,
            'ssm_b_re': ssm_b_re, 'ssm_b_im': ssm_b_im, 'ssm_c_re': ssm_c_re, 'ssm_c_im': ssm_c_im,
            'ssm_d': ssm_d, 'w_glu': w_glu, 'w_ssm_proj': w_ssm_proj, 'w_out': w_out,
            'ln1_g': ln1_g, 'ln1_b': ln1_b, 'w_up': w_up, 'conv_w': conv_w, 'conv_b': conv_b,
            'w_down': w_down, 'ln2_g': ln2_g, 'ln2_b': ln2_b}


def reference(x, w_in, w_attn_proj, ssm_a_re, ssm_a_im, ssm_log_dt, ssm_b_re, ssm_b_im,
              ssm_c_re, ssm_c_im, ssm_d, w_glu, w_ssm_proj, w_out, ln1_g, ln1_b,
              w_up, conv_w, conv_b, w_down, ln2_g, ln2_b):
    bsz, seq, _ = x.shape
    cos, sin = rotary_tables(seq)
    splits = np.cumsum([ATTN_WIDTH, ATTN_WIDTH, ATTN_WIDTH, SSM_WIDTH, D_MODEL])
    h = x
    for l in range(DEPTH):
        proj = h @ w_in[l]
        q, k, v, u, g_attn, g_ssm = jnp.split(proj, splits, axis=-1)
        q = apply_partial_rotary(q.reshape(bsz, seq, N_ATTN_HEADS, HEAD_DIM), cos, sin)
        k = apply_partial_rotary(k.reshape(bsz, seq, N_ATTN_HEADS, HEAD_DIM), cos, sin)
        v = v.reshape(bsz, seq, N_ATTN_HEADS, HEAD_DIM)
        attn = moba_attention(q, k, v)
        ssm = s5_branch(u, ssm_a_re[l], ssm_a_im[l], ssm_log_dt[l], ssm_b_re[l], ssm_b_im[l],
                        ssm_c_re[l], ssm_c_im[l], ssm_d[l], w_glu[l])
        merged = (jax.nn.sigmoid(g_attn) * (attn @ w_attn_proj[l])
                  + jax.nn.sigmoid(g_ssm) * (ssm @ w_ssm_proj[l]))
        mix = merged @ w_out[l]
        h = layer_norm(DEEPNORM_ALPHA * h + mix, ln1_g[l], ln1_b[l])
        up = causal_depthwise_conv(h @ w_up[l], conv_w[l], conv_b[l])
        val, gate = jnp.split(up, 2, axis=-1)
        ff = (jax.nn.gelu(gate) * val) @ w_down[l]
        h = layer_norm(DEEPNORM_ALPHA * h + ff, ln2_g[l], ln2_b[l])
    return h
```

```python
import functools
import math

import jax
import jax.numpy as jnp
from jax import lax
from jax.experimental import pallas as pl
from jax.experimental.pallas import tpu as pltpu

F32 = jnp.float32
BF16 = jnp.bfloat16

N_HEADS = 8
HEAD_DIM = 64
ROT_DIM = HEAD_DIM // 4
ROPE_THETA = 500000.0
MOBA_BLOCK = 256
MOBA_TOP_K = 3
SSM_GROUP = 16
SSM_STATE = 64
CONV_WIDTH = 3
LN_EPS = 1e-5
NEG_INF = -1e30

LANES = 128
SUBLANES = 8
VMEM_LIMIT = 56 * 1024 * 1024

SSM_CHUNK = 16
ROW_TILE = 512
FF_CHUNK = 256

_NT = (((1,), (1,)), ((), ()))


def _dot(a, b):
    return jnp.dot(a, b, preferred_element_type=F32)


def _dot_nt(a, b):
    return lax.dot_general(a, b, _NT, preferred_element_type=F32)


def _layer_norm(t, g, b):
    mu = jnp.mean(t, axis=-1, keepdims=True)
    d = t - mu
    var = jnp.mean(d * d, axis=-1, keepdims=True)
    return d * lax.rsqrt(var + LN_EPS) * g + b


def _inproj_kernel(x_ref, w_ref, cos_ref, s1_ref, s2_ref,
                   q_ref, k_ref, v_ref, u_ref, ga_ref, gs_ref, km_ref, *, aw, sw, dm, scale):
    xb = x_ref[...].astype(BF16)
    cos_t, s1_t, s2_t = cos_ref[...], s1_ref[...], s2_ref[...]
    tm = xb.shape[0]

    def rotary(t):
        return t * cos_t + pltpu.roll(t, ROT_DIM // 2, 1) * s1_t + pltpu.roll(t, LANES - ROT_DIM // 2, 1) * s2_t

    q = _dot(xb, w_ref[:, 0:aw])
    for j in range(aw // LANES):
        sl = slice(j * LANES, (j + 1) * LANES)
        q_ref[:, sl] = (rotary(q[:, sl]) * scale).astype(BF16)
    k = _dot(xb, w_ref[:, aw:2 * aw])
    for j in range(aw // LANES):
        sl = slice(j * LANES, (j + 1) * LANES)
        kr = rotary(k[:, sl])
        k_ref[:, sl] = kr.astype(BF16)
        for r in range(tm // MOBA_BLOCK):
            km_ref[r, :, sl] = jnp.mean(kr[r * MOBA_BLOCK:(r + 1) * MOBA_BLOCK], axis=0, keepdims=True)
    v_ref[...] = _dot(xb, w_ref[:, 2 * aw:3 * aw]).astype(BF16)
    u_ref[...] = _dot(xb, w_ref[:, 3 * aw:3 * aw + sw]).astype(BF16)
    c0 = 3 * aw + sw
    ga_ref[...] = jax.nn.sigmoid(_dot(xb, w_ref[:, c0:c0 + dm])).astype(BF16)
    gs_ref[...] = jax.nn.sigmoid(_dot(xb, w_ref[:, c0 + dm:c0 + 2 * dm])).astype(BF16)


def _rotary_tables(seq):
    half = ROT_DIM // 2
    inv_freq = ROPE_THETA ** (-jnp.arange(0, ROT_DIM, 2, dtype=F32) / ROT_DIM)
    ang = jnp.arange(seq, dtype=F32)[:, None] * inv_freq[None, :]
    cos, sin = jnp.cos(ang), jnp.sin(ang)
    ones = jnp.ones((seq, HEAD_DIM - ROT_DIM), F32)
    zeros = jnp.zeros((seq, HEAD_DIM - ROT_DIM), F32)
    zh = jnp.zeros((seq, half), F32)
    cos_h = jnp.concatenate([cos, cos, ones], axis=1)
    s1_h = jnp.concatenate([zh, sin, zeros], axis=1)
    s2_h = jnp.concatenate([-sin, zh, zeros], axis=1)
    rep = LANES // HEAD_DIM
    return tuple(jnp.tile(t, (1, rep)) for t in (cos_h, s1_h, s2_h))


def _in_projection(x2, w_in, seq):
    n, dm = x2.shape
    aw, sw = N_HEADS * HEAD_DIM, w_in.shape[1] - 3 * N_HEADS * HEAD_DIM - 2 * dm
    tm = ROW_TILE
    cos_t, s1_t, s2_t = _rotary_tables(seq)
    tiles_per_seq = seq // tm
    row = lambda i: (i, 0)
    tab = lambda i: (i % tiles_per_seq, 0)
    out_shape = (
        jax.ShapeDtypeStruct((n, aw), BF16), jax.ShapeDtypeStruct((n, aw), BF16),
        jax.ShapeDtypeStruct((n, aw), BF16), jax.ShapeDtypeStruct((n, sw), BF16),
        jax.ShapeDtypeStruct((n, dm), BF16), jax.ShapeDtypeStruct((n, dm), BF16),
        jax.ShapeDtypeStruct((n // MOBA_BLOCK, 1, aw), F32),
    )
    return pl.pallas_call(
        functools.partial(_inproj_kernel, aw=aw, sw=sw, dm=dm, scale=HEAD_DIM ** -0.5),
        grid=(n // tm,),
        in_specs=[
            pl.BlockSpec((tm, dm), row),
            pl.BlockSpec(w_in.shape, lambda i: (0, 0)),
            pl.BlockSpec((tm, LANES), tab), pl.BlockSpec((tm, LANES), tab), pl.BlockSpec((tm, LANES), tab),
        ],
        out_specs=(
            pl.BlockSpec((tm, aw), row), pl.BlockSpec((tm, aw), row), pl.BlockSpec((tm, aw), row),
            pl.BlockSpec((tm, sw), row), pl.BlockSpec((tm, dm), row), pl.BlockSpec((tm, dm), row),
            pl.BlockSpec((tm // MOBA_BLOCK, 1, aw), lambda i: (i, 0, 0)),
        ),
        out_shape=out_shape,
        compiler_params=pltpu.CompilerParams(dimension_semantics=("parallel",), vmem_limit_bytes=VMEM_LIMIT),
        name="in_projection",
    )(x2, w_in, cos_t, s1_t, s2_t)


def _select_bias(gate, own, nb):
    blk = lax.broadcasted_iota(jnp.int32, gate.shape, 0)
    past = blk < own
    g = jnp.where(past, gate, NEG_INF)
    bias = jnp.full(gate.shape, NEG_INF, F32)
    for _ in range(min(MOBA_TOP_K, nb)):
        mx = jnp.max(g, axis=0, keepdims=True)
        first = jnp.min(jnp.where(g == mx, blk, nb), axis=0, keepdims=True)
        pick = blk == first
        bias = jnp.where(jnp.logical_and(pick, past), 0.0, bias)
        g = jnp.where(pick, -jnp.inf, g)
    return bias


def _attn_kernel(q_ref, k_ref, vt_ref, km_ref, o_ref,
                 va_ref, vb_ref, bias_a_ref, bias_b_ref, acc_a_ref, acc_b_ref, *, nb):
    own = pl.program_id(2)
    blk = MOBA_BLOCK
    row_d = lax.broadcasted_iota(jnp.int32, (LANES, blk), 0)

    @pl.when(own == 0)
    def _():
        def fill(n, c):
            vt = vt_ref[0, 0, n].astype(F32)
            va_ref[n] = jnp.where(row_d < HEAD_DIM, vt, 1.0).astype(BF16)
            vb_ref[n] = jnp.where(row_d >= HEAD_DIM, vt, 1.0).astype(BF16)
            return c
        lax.fori_loop(0, nb, fill, 0)

    q2 = q_ref[0].astype(F32)
    lane = lax.broadcasted_iota(jnp.int32, q2.shape, 1)
    qa = jnp.where(lane < HEAD_DIM, q2, 0.0).astype(BF16)
    qb = jnp.where(lane >= HEAD_DIM, q2, 0.0).astype(BF16)

    kmb = km_ref[0].astype(BF16)
    bias_a_ref[...] = _select_bias(_dot_nt(kmb, qa), own, nb)
    bias_b_ref[...] = _select_bias(_dot_nt(kmb, qb), own, nb)

    kd = k_ref[0, pl.ds(pl.multiple_of(own * blk, blk), blk), :]
    key_i = lax.broadcasted_iota(jnp.int32, (blk, blk), 0)
    qry_i = lax.broadcasted_iota(jnp.int32, (blk, blk), 1)
    causal = key_i <= qry_i

    def first(qh, v_ref, acc_ref):
        s = jnp.where(causal, _dot_nt(kd, qh), NEG_INF)
        m = jnp.max(s, axis=0, keepdims=True)
        p = jnp.exp(s - m)
        acc_ref[...] = _dot(v_ref[own], p.astype(BF16))
        return m

    m_a = first(qa, va_ref, acc_a_ref)
    m_b = first(qb, vb_ref, acc_b_ref)

    def step(n, qh, v_ref, bias_ref, acc_ref, kb, m):
        s = _dot_nt(kb, qh) + bias_ref[pl.ds(n, 1), :]
        m_new = jnp.maximum(m, jnp.max(s, axis=0, keepdims=True))
        p = jnp.exp(s - m_new)
        acc_ref[...] = acc_ref[...] * jnp.exp(m - m_new) + _dot(v_ref[n], p.astype(BF16))
        return m_new

    def body(n, carry):
        m_a, m_b = carry
        kb = k_ref[0, pl.ds(pl.multiple_of(n * blk, blk), blk), :]
        m_a = step(n, qa, va_ref, bias_a_ref, acc_a_ref, kb, m_a)
        m_b = step(n, qb, vb_ref, bias_b_ref, acc_b_ref, kb, m_b)
        return m_a, m_b

    lax.fori_loop(0, own, body, (m_a, m_b))

    acc_a, acc_b = acc_a_ref[...], acc_b_ref[...]
    o_a = acc_a[:HEAD_DIM] / acc_a[HEAD_DIM:]
    o_b = acc_b[HEAD_DIM:] / acc_b[:HEAD_DIM]
    o_ref[0] = jnp.concatenate([o_a, o_b], axis=0).T.astype(o_ref.dtype)


def _moba_attention(q, k, v, kmean):
    bsz, seq, aw = q.shape
    nb = seq // MOBA_BLOCK
    hp = aw // LANES
    vt = v.reshape(bsz, nb, MOBA_BLOCK, hp, LANES).transpose(0, 3, 1, 4, 2)
    return pl.pallas_call(
        functools.partial(_attn_kernel, nb=nb),
        grid=(bsz, hp, nb),
        in_specs=[
            pl.BlockSpec((1, MOBA_BLOCK, LANES), lambda b, h, i: (b, i, h)),
            pl.BlockSpec((1, seq, LANES), lambda b, h, i: (b, 0, h)),
            pl.BlockSpec((1, 1, nb, LANES, MOBA_BLOCK), lambda b, h, i: (b, h, 0, 0, 0)),
            pl.BlockSpec((1, nb, LANES), lambda b, h, i: (b, 0, h)),
        ],
        out_specs=pl.BlockSpec((1, MOBA_BLOCK, LANES), lambda b, h, i: (b, i, h)),
        out_shape=jax.ShapeDtypeStruct((bsz, seq, aw), BF16),
        scratch_shapes=[
            pltpu.VMEM((nb, LANES, MOBA_BLOCK), BF16), pltpu.VMEM((nb, LANES, MOBA_BLOCK), BF16),
            pltpu.VMEM((nb, MOBA_BLOCK), F32), pltpu.VMEM((nb, MOBA_BLOCK), F32),
            pltpu.VMEM((LANES, MOBA_BLOCK), F32), pltpu.VMEM((LANES, MOBA_BLOCK), F32),
        ],
        compiler_params=pltpu.CompilerParams(
            dimension_semantics=("parallel", "parallel", "arbitrary"), vmem_limit_bytes=VMEM_LIMIT),
        name="moba_attention",
    )(q, k, vt, kmean)


def _s5_tables(a_re, a_im, log_dt, b_re, b_im, c_re, c_im, d_skip):
    hi = lax.Precision.HIGHEST
    L = SSM_CHUNK
    g, p = a_re.shape
    dt = jnp.exp(log_dt)[:, None]
    mag = jnp.exp(a_re * dt)
    lb_re, lb_im = mag * jnp.cos(a_im * dt), mag * jnp.sin(a_im * dt)
    den = a_re * a_re + a_im * a_im
    n_re, n_im = lb_re - 1.0, lb_im
    z_re = (n_re * a_re + n_im * a_im) / den
    z_im = (n_im * a_re - n_re * a_im) / den
    j = jnp.arange(L + 1, dtype=F32)[None, :, None]
    pmag = jnp.exp(j * (a_re * dt)[:, None, :])
    ang = j * (a_im * dt)[:, None, :]
    pr, pi = pmag * jnp.cos(ang), pmag * jnp.sin(ang)
    zb_re = z_re[:, :, None] * b_re - z_im[:, :, None] * b_im
    zb_im = z_re[:, :, None] * b_im + z_im[:, :, None] * b_re
    cl_re = c_re[:, None] * pr[:, :, None, :] - c_im[:, None] * pi[:, :, None, :]
    cl_im = c_re[:, None] * pi[:, :, None, :] + c_im[:, None] * pr[:, :, None, :]
    kern = (jnp.einsum('gjop,gpi->gjoi', cl_re[:, :L], zb_re, precision=hi)
            - jnp.einsum('gjop,gpi->gjoi', cl_im[:, :L], zb_im, precision=hi))
    kern = kern.at[:, 0].add(jax.vmap(jnp.diag)(d_skip))
    s_i = jnp.arange(L)[:, None]
    t_i = jnp.arange(L)[None, :]
    lag = t_i - s_i
    toep = jnp.where((lag >= 0)[None, :, :, None, None], kern[:, jnp.maximum(lag, 0)], 0.0)
    toep = toep.transpose(0, 1, 4, 2, 3).reshape(g, L * SSM_GROUP, L * SSM_GROUP)
    rev = L - 1 - jnp.arange(L)
    w_re = pr[:, rev][:, :, :, None] * zb_re[:, None] - pi[:, rev][:, :, :, None] * zb_im[:, None]
    w_im = pr[:, rev][:, :, :, None] * zb_im[:, None] + pi[:, rev][:, :, :, None] * zb_re[:, None]
    w_re = w_re.transpose(0, 1, 3, 2).reshape(g, L * SSM_GROUP, p)
    w_im = w_im.transpose(0, 1, 3, 2).reshape(g, L * SSM_GROUP, p)
    b_pow = jnp.concatenate([w_re, w_im], axis=-1)
    b_pow_sw = jnp.concatenate([w_im, w_re], axis=-1)
    cp_re = cl_re[:, 1:].transpose(0, 3, 1, 2).reshape(g, p, L * SSM_GROUP)
    cp_im = cl_im[:, 1:].transpose(0, 3, 1, 2).reshape(g, p, L * SSM_GROUP)
    c_pow = jnp.concatenate([cp_re, -cp_im], axis=1)
    lr, li = pr[:, L], pi[:, L]
    lam = jnp.stack([jnp.concatenate([lr, lr], -1), jnp.concatenate([-li, li], -1)], axis=1)
    return toep.astype(BF16), b_pow.astype(BF16), b_pow_sw.astype(BF16), c_pow.astype(BF16), lam


def _s5_kernel(u_ref, toep_ref, bp_ref, bps_ref, cp_ref, lam_ref, y_ref, s_ref, ss_ref, hp_ref, *, bsz):
    u = u_ref[0]
    s_ref[...] = _dot(u, bp_ref[0])
    ss_ref[...] = _dot(u, bps_ref[0])
    a = lam_ref[0, 0:1, :]
    bv = lam_ref[0, 1:2, :]
    rows = u.shape[0]
    per = SUBLANES // bsz
    row_i = lax.broadcasted_iota(jnp.int32, (SUBLANES, 2 * SSM_STATE), 0)

    def body(kk, carry):
        h, hs = carry
        r0 = pl.multiple_of(kk * SUBLANES, SUBLANES)
        s8 = s_ref[pl.ds(r0, SUBLANES), :]
        ss8 = ss_ref[pl.ds(r0, SUBLANES), :]
        hprev = h
        for c in range(per):
            h_new = a * h + bv * hs + s8
            hs_new = a * hs - bv * h + ss8
            h = pltpu.roll(h_new, bsz, 0) if per > 1 else h_new
            hs = pltpu.roll(hs_new, bsz, 0) if per > 1 else hs_new
            if c + 1 < per:
                hprev = jnp.where(row_i < (c + 1) * bsz, hprev, h)
        hp_ref[pl.ds(r0, SUBLANES), :] = hprev
        return h, hs

    zero = jnp.zeros((SUBLANES, 2 * SSM_STATE), F32)
    lax.fori_loop(0, rows // SUBLANES, body, (zero, zero))
    y = _dot(u, toep_ref[0]) + _dot(hp_ref[...].astype(BF16), cp_ref[0])
    y_ref[0] = y


def _s5_scan(u, tables, bsz, seq):
    toep, b_pow, b_pow_sw, c_pow, lam = tables
    g = toep.shape[0]
    L = SSM_CHUNK
    nc = seq // L
    w = L * SSM_GROUP
    p2 = 2 * SSM_STATE
    assert SUBLANES % bsz == 0 and (nc * bsz) % SUBLANES == 0
    ug = u.reshape(bsz, nc, L, g, SSM_GROUP).transpose(3, 1, 0, 2, 4).reshape(g, nc * bsz, w)
    rows = nc * bsz
    grp = lambda i: (i, 0, 0)
    y = pl.pallas_call(
        functools.partial(_s5_kernel, bsz=bsz),
        grid=(g,),
        in_specs=[
            pl.BlockSpec((1, rows, w), grp), pl.BlockSpec((1, w, w), grp),
            pl.BlockSpec((1, w, p2), grp), pl.BlockSpec((1, w, p2), grp),
            pl.BlockSpec((1, p2, w), grp), pl.BlockSpec((1, 2, p2), grp),
        ],
        out_specs=pl.BlockSpec((1, rows, w), grp),
        out_shape=jax.ShapeDtypeStruct((g, rows, w), F32),
        scratch_shapes=[pltpu.VMEM((rows, p2), F32), pltpu.VMEM((rows, p2), F32), pltpu.VMEM((rows, p2), F32)],
        compiler_params=pltpu.CompilerParams(dimension_semantics=("parallel",), vmem_limit_bytes=VMEM_LIMIT),
        name="s5_scan",
    )(ug, toep, b_pow, b_pow_sw, c_pow, lam)
    return y.reshape(g, nc, bsz, L, SSM_GROUP).transpose(2, 1, 3, 0, 4).reshape(bsz * seq, g * SSM_GROUP)


def _merge_kernel(x_ref, attn_ref, y_ref, ga_ref, gs_ref, wglu_ref, wap_ref, wsp_ref, wout_ref, g_ref, b_ref,
                  h_ref, *, alpha):
    ssm = jax.nn.gelu(y_ref[...])
    ssm = ssm * jax.nn.sigmoid(_dot(ssm.astype(BF16), wglu_ref[...]))
    merged = (ga_ref[...].astype(F32) * _dot(attn_ref[...], wap_ref[...])
              + gs_ref[...].astype(F32) * _dot(ssm.astype(BF16), wsp_ref[...]))
    mix = _dot(merged.astype(BF16), wout_ref[...])
    h_ref[...] = _layer_norm(alpha * x_ref[...] + mix, g_ref[...], b_ref[...])


def _merge(x2, attn, y, ga, gs, w_glu, w_ap, w_sp, w_out, ln_g, ln_b, alpha):
    n, dm = x2.shape
    aw, sw = attn.shape[1], y.shape[1]
    tm = ROW_TILE
    row = lambda i: (i, 0)
    full = lambda i: (0, 0)
    return pl.pallas_call(
        functools.partial(_merge_kernel, alpha=alpha),
        grid=(n // tm,),
        in_specs=[
            pl.BlockSpec((tm, dm), row), pl.BlockSpec((tm, aw), row), pl.BlockSpec((tm, sw), row),
            pl.BlockSpec((tm, dm), row), pl.BlockSpec((tm, dm), row),
            pl.BlockSpec(w_glu.shape, full), pl.BlockSpec(w_ap.shape, full), pl.BlockSpec(w_sp.shape, full),
            pl.BlockSpec(w_out.shape, full), pl.BlockSpec((1, dm), full), pl.BlockSpec((1, dm), full),
        ],
        out_specs=pl.BlockSpec((tm, dm), row),
        out_shape=jax.ShapeDtypeStruct((n, dm), F32),
        compiler_params=pltpu.CompilerParams(dimension_semantics=("parallel",), vmem_limit_bytes=VMEM_LIMIT),
        name="merge_ln1",
    )(x2, attn, y, ga, gs, w_glu, w_ap, w_sp, w_out, ln_g, ln_b)


def _ffn_kernel(h_ref, wv_ref, wg_ref, cwv_ref, cwg_ref, wd_ref, g_ref, b_ref, o_ref,
                halo_v_ref, halo_g_ref, acc_ref, *, alpha, tiles_per_seq, n_chunks):
    i = pl.program_id(0)
    h = h_ref[...]
    hb = h.astype(BF16)
    tm = h.shape[0]
    row_i = lax.broadcasted_iota(jnp.int32, (tm, FF_CHUNK), 0)
    seq_start = (i % tiles_per_seq) == 0

    def conv(up, cw, halo_ref, j):
        prev = halo_ref[j]
        halo_ref[j] = up[tm - SUBLANES:]
        p1 = prev[SUBLANES - 1:SUBLANES]
        p2 = prev[SUBLANES - 2:SUBLANES - 1]
        d1 = jnp.where(row_i == 0, p1, pltpu.roll(up, 1, 0))
        d2 = jnp.where(row_i == 0, p2, jnp.where(row_i == 1, p1, pltpu.roll(up, 2, 0)))
        return cw[3:4] + cw[0:1] * d2 + cw[1:2] * d1 + cw[2:3] * up

    def body(j, c):
        val = conv(_dot(hb, wv_ref[j]), cwv_ref[j], halo_v_ref, j)
        gate = conv(_dot(hb, wg_ref[j]), cwg_ref[j], halo_g_ref, j)
        act = (jax.nn.gelu(gate) * val).astype(BF16)
        acc_ref[...] += _dot(act, wd_ref[j])
        return c

    @pl.when(seq_start)
    def _():
        halo_v_ref[...] = jnp.zeros_like(halo_v_ref)
        halo_g_ref[...] = jnp.zeros_like(halo_g_ref)

    acc_ref[...] = jnp.zeros_like(acc_ref)
    lax.fori_loop(0, n_chunks, body, 0)
    o_ref[...] = _layer_norm(alpha * h + acc_ref[...], g_ref[...], b_ref[...])


def _ffn(h, w_up, conv_w, conv_b, w_down, ln_g, ln_b, alpha, seq):
    n, dm = h.shape
    dff = w_down.shape[0]
    ck = FF_CHUNK
    nck = dff // ck
    tm = ROW_TILE
    chunked = lambda w: w.reshape(dm, nck, ck).transpose(1, 0, 2)
    wv, wg = chunked(w_up[:, :dff]), chunked(w_up[:, dff:])
    cw = jnp.concatenate([conv_w, conv_b[None, :]], axis=0)
    cw = jnp.pad(cw, ((0, SUBLANES - cw.shape[0]), (0, 0)))
    cparams = lambda c: c.reshape(SUBLANES, nck, ck).transpose(1, 0, 2)
    cwv, cwg = cparams(cw[:, :dff]), cparams(cw[:, dff:])
    wd = w_down.reshape(nck, ck, dm)
    row = lambda i: (i, 0)
    full3 = lambda i: (0, 0, 0)
    full2 = lambda i: (0, 0)
    once = pl.Buffered(1)
    return pl.pallas_call(
        functools.partial(_ffn_kernel, alpha=alpha, tiles_per_seq=seq // tm, n_chunks=nck),
        grid=(n // tm,),
        in_specs=[
            pl.BlockSpec((tm, dm), row),
            pl.BlockSpec(wv.shape, full3, pipeline_mode=once), pl.BlockSpec(wg.shape, full3, pipeline_mode=once),
            pl.BlockSpec(cwv.shape, full3), pl.BlockSpec(cwg.shape, full3),
            pl.BlockSpec(wd.shape, full3, pipeline_mode=once),
            pl.BlockSpec((1, dm), full2), pl.BlockSpec((1, dm), full2),
        ],
        out_specs=pl.BlockSpec((tm, dm), row),
        out_shape=jax.ShapeDtypeStruct((n, dm), F32),
        scratch_shapes=[
            pltpu.VMEM((nck, SUBLANES, ck), F32), pltpu.VMEM((nck, SUBLANES, ck), F32),
            pltpu.VMEM((tm, dm), F32),
        ],
        compiler_params=pltpu.CompilerParams(dimension_semantics=("arbitrary",), vmem_limit_bytes=VMEM_LIMIT),
        name="conv_ffn_ln2",
    )(h, wv, wg, cwv, cwg, wd, ln_g, ln_b)


def kernel(x, w_in, w_attn_proj, ssm_a_re, ssm_a_im, ssm_log_dt, ssm_b_re, ssm_b_im, ssm_c_re, ssm_c_im, ssm_d,
           w_glu, w_ssm_proj, w_out, ln1_g, ln1_b, w_up, conv_w, conv_b, w_down, ln2_g, ln2_b):
    bsz, seq, dm = x.shape
    depth = w_in.shape[0]
    alpha = (2.0 * depth) ** 0.25
    n = bsz * seq
    assert seq % ROW_TILE == 0 and ROW_TILE % MOBA_BLOCK == 0 and seq % SSM_CHUNK == 0
    h = x.reshape(n, dm)
    for l in range(depth):
        q, k, v, u, ga, gs, kmean = _in_projection(h, w_in[l].astype(BF16), seq)
        aw = q.shape[1]
        attn = _moba_attention(q.reshape(bsz, seq, aw), k.reshape(bsz, seq, aw), v.reshape(bsz, seq, aw),
                               kmean.reshape(bsz, seq // MOBA_BLOCK, aw)).reshape(n, aw)
        tables = _s5_tables(ssm_a_re[l], ssm_a_im[l], ssm_log_dt[l], ssm_b_re[l], ssm_b_im[l],
                            ssm_c_re[l], ssm_c_im[l], ssm_d[l])
        y = _s5_scan(u, tables, bsz, seq)
        h = _merge(h, attn, y, ga, gs, w_glu[l].astype(BF16), w_attn_proj[l].astype(BF16),
                   w_ssm_proj[l].astype(BF16), w_out[l].astype(BF16), ln1_g[l][None], ln1_b[l][None], alpha)
        h = _ffn(h, w_up[l].astype(BF16), conv_w[l], conv_b[l], w_down[l].astype(BF16),
                 ln2_g[l][None], ln2_b[l][None], alpha, seq)
    return h.reshape(bsz, seq, dm)
```

```python
import functools
import math

import jax
import jax.numpy as jnp
from jax import lax
from jax.experimental import pallas as pl
from jax.experimental.pallas import tpu as pltpu

F32 = jnp.float32
BF16 = jnp.bfloat16

N_HEADS = 8
HEAD_DIM = 64
ROT_DIM = HEAD_DIM // 4
ROPE_THETA = 500000.0
MOBA_BLOCK = 256
MOBA_TOP_K = 3
SSM_GROUP = 16
SSM_STATE = 64
CONV_WIDTH = 3
LN_EPS = 1e-5
NEG_INF = -1e30

LANES = 128
SUBLANES = 8
VMEM_LIMIT = 56 * 1024 * 1024

SSM_CHUNK = 16
ROW_TILE = 512
FF_CHUNK = 256

_NT = (((1,), (1,)), ((), ()))


def _dot(a, b):
    return jnp.dot(a, b, preferred_element_type=F32)


def _dot_nt(a, b):
    return lax.dot_general(a, b, _NT, preferred_element_type=F32)


def _layer_norm(t, g, b):
    mu = jnp.mean(t, axis=-1, keepdims=True)
    d = t - mu
    var = jnp.mean(d * d, axis=-1, keepdims=True)
    return d * lax.rsqrt(var + LN_EPS) * g + b


def _inproj_kernel(x_ref, w_ref, cos_ref, s1_ref, s2_ref,
                   q_ref, k_ref, v_ref, u_ref, ga_ref, gs_ref, km_ref, *, aw, sw, dm, scale):
    xb = x_ref[...].astype(BF16)
    cos_t, s1_t, s2_t = cos_ref[...], s1_ref[...], s2_ref[...]
    tm = xb.shape[0]

    def rotary(t):
        return t * cos_t + pltpu.roll(t, ROT_DIM // 2, 1) * s1_t + pltpu.roll(t, LANES - ROT_DIM // 2, 1) * s2_t

    q = _dot(xb, w_ref[:, 0:aw])
    for j in range(aw // LANES):
        sl = slice(j * LANES, (j + 1) * LANES)
        q_ref[:, sl] = (rotary(q[:, sl]) * scale).astype(BF16)
    k = _dot(xb, w_ref[:, aw:2 * aw])
    for j in range(aw // LANES):
        sl = slice(j * LANES, (j + 1) * LANES)
        kr = rotary(k[:, sl])
        k_ref[:, sl] = kr.astype(BF16)
        for r in range(tm // MOBA_BLOCK):
            km_ref[r, :, sl] = jnp.mean(kr[r * MOBA_BLOCK:(r + 1) * MOBA_BLOCK], axis=0, keepdims=True)
    v_ref[...] = _dot(xb, w_ref[:, 2 * aw:3 * aw]).astype(BF16)
    u_ref[...] = _dot(xb, w_ref[:, 3 * aw:3 * aw + sw]).astype(BF16)
    c0 = 3 * aw + sw
    ga_ref[...] = jax.nn.sigmoid(_dot(xb, w_ref[:, c0:c0 + dm])).astype(BF16)
    gs_ref[...] = jax.nn.sigmoid(_dot(xb, w_ref[:, c0 + dm:c0 + 2 * dm])).astype(BF16)


def _rotary_tables(seq):
    half = ROT_DIM // 2
    inv_freq = ROPE_THETA ** (-jnp.arange(0, ROT_DIM, 2, dtype=F32) / ROT_DIM)
    ang = jnp.arange(seq, dtype=F32)[:, None] * inv_freq[None, :]
    cos, sin = jnp.cos(ang), jnp.sin(ang)
    ones = jnp.ones((seq, HEAD_DIM - ROT_DIM), F32)
    zeros = jnp.zeros((seq, HEAD_DIM - ROT_DIM), F32)
    zh = jnp.zeros((seq, half), F32)
    cos_h = jnp.concatenate([cos, cos, ones], axis=1)
    s1_h = jnp.concatenate([zh, sin, zeros], axis=1)
    s2_h = jnp.concatenate([-sin, zh, zeros], axis=1)
    rep = LANES // HEAD_DIM
    return tuple(jnp.tile(t, (1, rep)) for t in (cos_h, s1_h, s2_h))


def _in_projection(x2, w_in, seq):
    n, dm = x2.shape
    aw, sw = N_HEADS * HEAD_DIM, w_in.shape[1] - 3 * N_HEADS * HEAD_DIM - 2 * dm
    tm = ROW_TILE
    cos_t, s1_t, s2_t = _rotary_tables(seq)
    tiles_per_seq = seq // tm
    row = lambda i: (i, 0)
    tab = lambda i: (i % tiles_per_seq, 0)
    out_shape = (
        jax.ShapeDtypeStruct((n, aw), BF16), jax.ShapeDtypeStruct((n, aw), BF16),
        jax.ShapeDtypeStruct((n, aw), BF16), jax.ShapeDtypeStruct((n, sw), BF16),
        jax.ShapeDtypeStruct((n, dm), BF16), jax.ShapeDtypeStruct((n, dm), BF16),
        jax.ShapeDtypeStruct((n // MOBA_BLOCK, 1, aw), F32),
    )
    return pl.pallas_call(
        functools.partial(_inproj_kernel, aw=aw, sw=sw, dm=dm, scale=HEAD_DIM ** -0.5 * math.log2(math.e)),
        grid=(n // tm,),
        in_specs=[
            pl.BlockSpec((tm, dm), row),
            pl.BlockSpec(w_in.shape, lambda i: (0, 0)),
            pl.BlockSpec((tm, LANES), tab), pl.BlockSpec((tm, LANES), tab), pl.BlockSpec((tm, LANES), tab),
        ],
        out_specs=(
            pl.BlockSpec((tm, aw), row), pl.BlockSpec((tm, aw), row), pl.BlockSpec((tm, aw), row),
            pl.BlockSpec((tm, sw), row), pl.BlockSpec((tm, dm), row), pl.BlockSpec((tm, dm), row),
            pl.BlockSpec((tm // MOBA_BLOCK, 1, aw), lambda i: (i, 0, 0)),
        ),
        out_shape=out_shape,
        compiler_params=pltpu.CompilerParams(dimension_semantics=("parallel",), vmem_limit_bytes=VMEM_LIMIT),
        name="in_projection",
    )(x2, w_in, cos_t, s1_t, s2_t)


PAIR = 2 * MOBA_BLOCK
MASKED = 2.0 * NEG_INF
BIAS_PAD = 8


def _select_bias(gate, own, nb):
    blk = lax.broadcasted_iota(jnp.int32, gate.shape, 0)
    past = blk < own
    g = jnp.where(past, gate, NEG_INF)
    bias = jnp.full(gate.shape, MASKED, F32)
    for _ in range(min(MOBA_TOP_K, nb)):
        mx = jnp.max(g, axis=0, keepdims=True)
        first = jnp.min(jnp.where(g == mx, blk, nb), axis=0, keepdims=True)
        pick = blk == first
        bias = jnp.where(jnp.logical_and(pick, past), 0.0, bias)
        g = jnp.where(pick, -jnp.inf, g)
    pad = jnp.full((BIAS_PAD, gate.shape[1]), MASKED, F32)
    return jnp.concatenate([bias, pad], axis=0)


def _attn_kernel(q_ref, k_ref, vt_ref, km_ref, o_ref,
                 va_ref, vb_ref, bias_ref, s_ref, p_ref, cm_ref, al_ref, acc_ref, *, nb):
    own = pl.program_id(2)
    blk = MOBA_BLOCK
    n_pairs = nb // 2
    v_refs = (va_ref, vb_ref)
    row_d = lax.broadcasted_iota(jnp.int32, (LANES, blk), 0)
    head_rows = (row_d < HEAD_DIM, row_d >= HEAD_DIM)

    @pl.when(own == 0)
    def _():
        def fill(j, c):
            for half in range(2):
                vt = vt_ref[0, 0, 2 * j + half].astype(F32)
                for h in range(2):
                    v_refs[h][j, :, half * blk:(half + 1) * blk] = jnp.where(head_rows[h], vt, 1.0).astype(BF16)
            return c
        lax.fori_loop(0, n_pairs, fill, 0)

    q2 = q_ref[0].astype(F32)
    lane = lax.broadcasted_iota(jnp.int32, q2.shape, 1)
    qh = (jnp.where(lane < HEAD_DIM, q2, 0.0).astype(BF16), jnp.where(lane >= HEAD_DIM, q2, 0.0).astype(BF16))

    kmb = km_ref[0].astype(BF16)
    for h in range(2):
        bias_ref[h] = _select_bias(_dot_nt(kmb, qh[h]), own, nb)

    kd = k_ref[0, pl.ds(pl.multiple_of(own * blk, blk), blk), :]
    key_i = lax.broadcasted_iota(jnp.int32, (blk, blk), 0)
    qry_i = lax.broadcasted_iota(jnp.int32, (blk, blk), 1)
    causal = key_i <= qry_i
    vt_own = vt_ref[0, 0, own].astype(F32)
    m0 = []
    for h in range(2):
        s = jnp.where(causal, _dot_nt(kd, qh[h]), NEG_INF)
        m = jnp.max(s, axis=0, keepdims=True)
        p = jnp.exp2(s - m)
        acc_ref[h] = _dot(jnp.where(head_rows[h], vt_own, 1.0).astype(BF16), p.astype(BF16))
        m0.append(m)

    def stage_a(t, slot):
        j = jnp.minimum(t, n_pairs - 1)
        kp = k_ref[0, pl.ds(pl.multiple_of(j * PAIR, PAIR), PAIR), :]
        for h in range(2):
            s = _dot_nt(kp, qh[h])
            s_ref[slot, h] = s
            cm_ref[slot, 2 * h:2 * h + 1, :] = jnp.max(s[:blk], axis=0, keepdims=True)
            cm_ref[slot, 2 * h + 1:2 * h + 2, :] = jnp.max(s[blk:], axis=0, keepdims=True)

    def stage_b(t, slot, m):
        m_out = []
        for h in range(2):
            b0 = bias_ref[h, pl.ds(2 * t, 1), :]
            b1 = bias_ref[h, pl.ds(2 * t + 1, 1), :]
            cm0 = cm_ref[slot, 2 * h:2 * h + 1, :]
            cm1 = cm_ref[slot, 2 * h + 1:2 * h + 2, :]
            m_new = jnp.maximum(m[h], jnp.maximum(cm0 + b0, cm1 + b1))
            al_ref[slot, h:h + 1, :] = jnp.exp2(m[h] - m_new)
            p_ref[slot, h, 0:blk] = jnp.exp2(s_ref[slot, h, 0:blk] - (m_new - b0)).astype(BF16)
            p_ref[slot, h, blk:PAIR] = jnp.exp2(s_ref[slot, h, blk:PAIR] - (m_new - b1)).astype(BF16)
            m_out.append(m_new)
        return tuple(m_out)

    def stage_c(t, slot):
        j = jnp.minimum(t, n_pairs - 1)
        for h in range(2):
            pv = _dot(v_refs[h][j], p_ref[slot, h])
            acc_ref[h] = acc_ref[h] * al_ref[slot, h:h + 1, :] + pv

    stage_a(0, 0)
    m1 = stage_b(0, 0, tuple(m0))
    stage_a(1, 1)

    def trip(t, slot, m):
        stage_a(t, slot)
        m = stage_b(t - 1, 1 - slot, m)
        stage_c(t - 2, slot)
        return m

    def double_trip(i, m):
        t = 2 + 2 * i
        return trip(t + 1, 1, trip(t, 0, m))

    n_trips = (own + 1) // 2
    lax.fori_loop(0, (n_trips + 1) // 2, double_trip, m1)

    acc_a, acc_b = acc_ref[0], acc_ref[1]
    o_a = acc_a[:HEAD_DIM] / acc_a[HEAD_DIM:]
    o_b = acc_b[HEAD_DIM:] / acc_b[:HEAD_DIM]
    o_ref[0] = jnp.concatenate([o_a, o_b], axis=0).T.astype(o_ref.dtype)


def _moba_attention(q, k, v, kmean):
    bsz, seq, aw = q.shape
    nb = seq // MOBA_BLOCK
    hp = aw // LANES
    assert nb % 2 == 0
    vt = v.reshape(bsz, nb, MOBA_BLOCK, hp, LANES).transpose(0, 3, 1, 4, 2)
    return pl.pallas_call(
        functools.partial(_attn_kernel, nb=nb),
        grid=(bsz, hp, nb),
        in_specs=[
            pl.BlockSpec((1, MOBA_BLOCK, LANES), lambda b, h, i: (b, i, h)),
            pl.BlockSpec((1, seq, LANES), lambda b, h, i: (b, 0, h)),
            pl.BlockSpec((1, 1, nb, LANES, MOBA_BLOCK), lambda b, h, i: (b, h, 0, 0, 0)),
            pl.BlockSpec((1, nb, LANES), lambda b, h, i: (b, 0, h)),
        ],
        out_specs=pl.BlockSpec((1, MOBA_BLOCK, LANES), lambda b, h, i: (b, i, h)),
        out_shape=jax.ShapeDtypeStruct((bsz, seq, aw), BF16),
        scratch_shapes=[
            pltpu.VMEM((nb // 2, LANES, PAIR), BF16), pltpu.VMEM((nb // 2, LANES, PAIR), BF16),
            pltpu.VMEM((2, nb + BIAS_PAD, MOBA_BLOCK), F32),
            pltpu.VMEM((2, 2, PAIR, MOBA_BLOCK), F32),
            pltpu.VMEM((2, 2, PAIR, MOBA_BLOCK), BF16),
            pltpu.VMEM((2, SUBLANES, MOBA_BLOCK), F32),
            pltpu.VMEM((2, SUBLANES, MOBA_BLOCK), F32),
            pltpu.VMEM((2, LANES, MOBA_BLOCK), F32),
        ],
        compiler_params=pltpu.CompilerParams(
            dimension_semantics=("parallel", "parallel", "arbitrary"), vmem_limit_bytes=VMEM_LIMIT),
        name="moba_attention",
    )(q, k, vt, kmean)


def _s5_tables(a_re, a_im, log_dt, b_re, b_im, c_re, c_im, d_skip):
    hi = lax.Precision.HIGHEST
    L = SSM_CHUNK
    g, p = a_re.shape
    dt = jnp.exp(log_dt)[:, None]
    mag = jnp.exp(a_re * dt)
    lb_re, lb_im = mag * jnp.cos(a_im * dt), mag * jnp.sin(a_im * dt)
    den = a_re * a_re + a_im * a_im
    n_re, n_im = lb_re - 1.0, lb_im
    z_re = (n_re * a_re + n_im * a_im) / den
    z_im = (n_im * a_re - n_re * a_im) / den
    j = jnp.arange(L + 1, dtype=F32)[None, :, None]
    pmag = jnp.exp(j * (a_re * dt)[:, None, :])
    ang = j * (a_im * dt)[:, None, :]
    pr, pi = pmag * jnp.cos(ang), pmag * jnp.sin(ang)
    zb_re = z_re[:, :, None] * b_re - z_im[:, :, None] * b_im
    zb_im = z_re[:, :, None] * b_im + z_im[:, :, None] * b_re
    cl_re = c_re[:, None] * pr[:, :, None, :] - c_im[:, None] * pi[:, :, None, :]
    cl_im = c_re[:, None] * pi[:, :, None, :] + c_im[:, None] * pr[:, :, None, :]
    kern = (jnp.einsum('gjop,gpi->gjoi', cl_re[:, :L], zb_re, precision=hi)
            - jnp.einsum('gjop,gpi->gjoi', cl_im[:, :L], zb_im, precision=hi))
    kern = kern.at[:, 0].add(jax.vmap(jnp.diag)(d_skip))
    s_i = jnp.arange(L)[:, None]
    t_i = jnp.arange(L)[None, :]
    lag = t_i - s_i
    toep = jnp.where((lag >= 0)[None, :, :, None, None], kern[:, jnp.maximum(lag, 0)], 0.0)
    toep = toep.transpose(0, 1, 4, 2, 3).reshape(g, L * SSM_GROUP, L * SSM_GROUP)
    rev = L - 1 - jnp.arange(L)
    w_re = pr[:, rev][:, :, :, None] * zb_re[:, None] - pi[:, rev][:, :, :, None] * zb_im[:, None]
    w_im = pr[:, rev][:, :, :, None] * zb_im[:, None] + pi[:, rev][:, :, :, None] * zb_re[:, None]
    w_re = w_re.transpose(0, 1, 3, 2).reshape(g, L * SSM_GROUP, p)
    w_im = w_im.transpose(0, 1, 3, 2).reshape(g, L * SSM_GROUP, p)
    b_pow = jnp.concatenate([w_re, w_im], axis=-1)
    b_pow_sw = jnp.concatenate([w_im, w_re], axis=-1)
    cp_re = cl_re[:, 1:].transpose(0, 3, 1, 2).reshape(g, p, L * SSM_GROUP)
    cp_im = cl_im[:, 1:].transpose(0, 3, 1, 2).reshape(g, p, L * SSM_GROUP)
    c_pow = jnp.concatenate([cp_re, -cp_im], axis=1)
    lr, li = pr[:, L], pi[:, L]
    lam = jnp.stack([jnp.concatenate([lr, lr], -1), jnp.concatenate([-li, li], -1)], axis=1)
    return toep.astype(BF16), b_pow.astype(BF16), b_pow_sw.astype(BF16), c_pow.astype(BF16), lam


def _s5_kernel(u_ref, toep_ref, bp_ref, bps_ref, cp_ref, lam_ref, y_ref, s_ref, ss_ref, hp_ref, *, bsz):
    u = u_ref[0]
    s_ref[...] = _dot(u, bp_ref[0])
    ss_ref[...] = _dot(u, bps_ref[0])
    a = lam_ref[0, 0:1, :]
    bv = lam_ref[0, 1:2, :]
    rows = u.shape[0]
    per = SUBLANES // bsz
    row_i = lax.broadcasted_iota(jnp.int32, (SUBLANES, 2 * SSM_STATE), 0)

    def body(kk, carry):
        h, hs = carry
        r0 = pl.multiple_of(kk * SUBLANES, SUBLANES)
        s8 = s_ref[pl.ds(r0, SUBLANES), :]
        ss8 = ss_ref[pl.ds(r0, SUBLANES), :]
        hprev = h
        for c in range(per):
            h_new = a * h + bv * hs + s8
            hs_new = a * hs - bv * h + ss8
            h = pltpu.roll(h_new, bsz, 0) if per > 1 else h_new
            hs = pltpu.roll(hs_new, bsz, 0) if per > 1 else hs_new
            if c + 1 < per:
                hprev = jnp.where(row_i < (c + 1) * bsz, hprev, h)
        hp_ref[pl.ds(r0, SUBLANES), :] = hprev
        return h, hs

    zero = jnp.zeros((SUBLANES, 2 * SSM_STATE), F32)
    lax.fori_loop(0, rows // SUBLANES, body, (zero, zero))
    y = _dot(u, toep_ref[0]) + _dot(hp_ref[...].astype(BF16), cp_ref[0])
    y_ref[0] = y


def _s5_scan(u, tables, bsz, seq):
    toep, b_pow, b_pow_sw, c_pow, lam = tables
    g = toep.shape[0]
    L = SSM_CHUNK
    nc = seq // L
    w = L * SSM_GROUP
    p2 = 2 * SSM_STATE
    assert SUBLANES % bsz == 0 and (nc * bsz) % SUBLANES == 0
    ug = u.reshape(bsz, nc, L, g, SSM_GROUP).transpose(3, 1, 0, 2, 4).reshape(g, nc * bsz, w)
    rows = nc * bsz
    grp = lambda i: (i, 0, 0)
    y = pl.pallas_call(
        functools.partial(_s5_kernel, bsz=bsz),
        grid=(g,),
        in_specs=[
            pl.BlockSpec((1, rows, w), grp), pl.BlockSpec((1, w, w), grp),
            pl.BlockSpec((1, w, p2), grp), pl.BlockSpec((1, w, p2), grp),
            pl.BlockSpec((1, p2, w), grp), pl.BlockSpec((1, 2, p2), grp),
        ],
        out_specs=pl.BlockSpec((1, rows, w), grp),
        out_shape=jax.ShapeDtypeStruct((g, rows, w), F32),
        scratch_shapes=[pltpu.VMEM((rows, p2), F32), pltpu.VMEM((rows, p2), F32), pltpu.VMEM((rows, p2), F32)],
        compiler_params=pltpu.CompilerParams(dimension_semantics=("parallel",), vmem_limit_bytes=VMEM_LIMIT),
        name="s5_scan",
    )(ug, toep, b_pow, b_pow_sw, c_pow, lam)
    return y.reshape(g, nc, bsz, L, SSM_GROUP).transpose(2, 1, 3, 0, 4).reshape(bsz * seq, g * SSM_GROUP)


def _merge_kernel(x_ref, attn_ref, y_ref, ga_ref, gs_ref, wglu_ref, wap_ref, wsp_ref, wout_ref, g_ref, b_ref,
                  h_ref, *, alpha):
    ssm = jax.nn.gelu(y_ref[...])
    ssm = ssm * jax.nn.sigmoid(_dot(ssm.astype(BF16), wglu_ref[...]))
    merged = (ga_ref[...].astype(F32) * _dot(attn_ref[...], wap_ref[...])
              + gs_ref[...].astype(F32) * _dot(ssm.astype(BF16), wsp_ref[...]))
    mix = _dot(merged.astype(BF16), wout_ref[...])
    h_ref[...] = _layer_norm(alpha * x_ref[...] + mix, g_ref[...], b_ref[...])


def _merge(x2, attn, y, ga, gs, w_glu, w_ap, w_sp, w_out, ln_g, ln_b, alpha):
    n, dm = x2.shape
    aw, sw = attn.shape[1], y.shape[1]
    tm = ROW_TILE
    row = lambda i: (i, 0)
    full = lambda i: (0, 0)
    return pl.pallas_call(
        functools.partial(_merge_kernel, alpha=alpha),
        grid=(n // tm,),
        in_specs=[
            pl.BlockSpec((tm, dm), row), pl.BlockSpec((tm, aw), row), pl.BlockSpec((tm, sw), row),
            pl.BlockSpec((tm, dm), row), pl.BlockSpec((tm, dm), row),
            pl.BlockSpec(w_glu.shape, full), pl.BlockSpec(w_ap.shape, full), pl.BlockSpec(w_sp.shape, full),
            pl.BlockSpec(w_out.shape, full), pl.BlockSpec((1, dm), full), pl.BlockSpec((1, dm), full),
        ],
        out_specs=pl.BlockSpec((tm, dm), row),
        out_shape=jax.ShapeDtypeStruct((n, dm), F32),
        compiler_params=pltpu.CompilerParams(dimension_semantics=("parallel",), vmem_limit_bytes=VMEM_LIMIT),
        name="merge_ln1",
    )(x2, attn, y, ga, gs, w_glu, w_ap, w_sp, w_out, ln_g, ln_b)


def _ffn_kernel(h_ref, wv_ref, wg_ref, cwv_ref, cwg_ref, wd_ref, g_ref, b_ref, o_ref,
                halo_v_ref, halo_g_ref, acc_ref, *, alpha, tiles_per_seq, n_chunks):
    i = pl.program_id(0)
    h = h_ref[...]
    hb = h.astype(BF16)
    tm = h.shape[0]
    row_i = lax.broadcasted_iota(jnp.int32, (tm, FF_CHUNK), 0)
    seq_start = (i % tiles_per_seq) == 0

    def conv(up, cw, halo_ref, j):
        prev = halo_ref[j]
        halo_ref[j] = up[tm - SUBLANES:]
        p1 = prev[SUBLANES - 1:SUBLANES]
        p2 = prev[SUBLANES - 2:SUBLANES - 1]
        d1 = jnp.where(row_i == 0, p1, pltpu.roll(up, 1, 0))
        d2 = jnp.where(row_i == 0, p2, jnp.where(row_i == 1, p1, pltpu.roll(up, 2, 0)))
        return cw[3:4] + cw[0:1] * d2 + cw[1:2] * d1 + cw[2:3] * up

    def body(j, c):
        val = conv(_dot(hb, wv_ref[j]), cwv_ref[j], halo_v_ref, j)
        gate = conv(_dot(hb, wg_ref[j]), cwg_ref[j], halo_g_ref, j)
        act = (jax.nn.gelu(gate) * val).astype(BF16)
        acc_ref[...] += _dot(act, wd_ref[j])
        return c

    @pl.when(seq_start)
    def _():
        halo_v_ref[...] = jnp.zeros_like(halo_v_ref)
        halo_g_ref[...] = jnp.zeros_like(halo_g_ref)

    acc_ref[...] = jnp.zeros_like(acc_ref)
    lax.fori_loop(0, n_chunks, body, 0)
    o_ref[...] = _layer_norm(alpha * h + acc_ref[...], g_ref[...], b_ref[...])


def _ffn(h, w_up, conv_w, conv_b, w_down, ln_g, ln_b, alpha, seq):
    n, dm = h.shape
    dff = w_down.shape[0]
    ck = FF_CHUNK
    nck = dff // ck
    tm = ROW_TILE
    chunked = lambda w: w.reshape(dm, nck, ck).transpose(1, 0, 2)
    wv, wg = chunked(w_up[:, :dff]), chunked(w_up[:, dff:])
    cw = jnp.concatenate([conv_w, conv_b[None, :]], axis=0)
    cw = jnp.pad(cw, ((0, SUBLANES - cw.shape[0]), (0, 0)))
    cparams = lambda c: c.reshape(SUBLANES, nck, ck).transpose(1, 0, 2)
    cwv, cwg = cparams(cw[:, :dff]), cparams(cw[:, dff:])
    wd = w_down.reshape(nck, ck, dm)
    row = lambda i: (i, 0)
    full3 = lambda i: (0, 0, 0)
    full2 = lambda i: (0, 0)
    once = pl.Buffered(1)
    return pl.pallas_call(
        functools.partial(_ffn_kernel, alpha=alpha, tiles_per_seq=seq // tm, n_chunks=nck),
        grid=(n // tm,),
        in_specs=[
            pl.BlockSpec((tm, dm), row),
            pl.BlockSpec(wv.shape, full3, pipeline_mode=once), pl.BlockSpec(wg.shape, full3, pipeline_mode=once),
            pl.BlockSpec(cwv.shape, full3), pl.BlockSpec(cwg.shape, full3),
            pl.BlockSpec(wd.shape, full3, pipeline_mode=once),
            pl.BlockSpec((1, dm), full2), pl.BlockSpec((1, dm), full2),
        ],
        out_specs=pl.BlockSpec((tm, dm), row),
        out_shape=jax.ShapeDtypeStruct((n, dm), F32),
        scratch_shapes=[
            pltpu.VMEM((nck, SUBLANES, ck), F32), pltpu.VMEM((nck, SUBLANES, ck), F32),
            pltpu.VMEM((tm, dm), F32),
        ],
        compiler_params=pltpu.CompilerParams(dimension_semantics=("arbitrary",), vmem_limit_bytes=VMEM_LIMIT),
        name="conv_ffn_ln2",
    )(h, wv, wg, cwv, cwg, wd, ln_g, ln_b)


def kernel(x, w_in, w_attn_proj, ssm_a_re, ssm_a_im, ssm_log_dt, ssm_b_re, ssm_b_im, ssm_c_re, ssm_c_im, ssm_d,
           w_glu, w_ssm_proj, w_out, ln1_g, ln1_b, w_up, conv_w, conv_b, w_down, ln2_g, ln2_b):
    bsz, seq, dm = x.shape
    depth = w_in.shape[0]
    alpha = (2.0 * depth) ** 0.25
    n = bsz * seq
    assert seq % ROW_TILE == 0 and ROW_TILE % MOBA_BLOCK == 0 and seq % SSM_CHUNK == 0
    h = x.reshape(n, dm)
    for l in range(depth):
        q, k, v, u, ga, gs, kmean = _in_projection(h, w_in[l].astype(BF16), seq)
        aw = q.shape[1]
        attn = _moba_attention(q.reshape(bsz, seq, aw), k.reshape(bsz, seq, aw), v.reshape(bsz, seq, aw),
                               kmean.reshape(bsz, seq // MOBA_BLOCK, aw)).reshape(n, aw)
        tables = _s5_tables(ssm_a_re[l], ssm_a_im[l], ssm_log_dt[l], ssm_b_re[l], ssm_b_im[l],
                            ssm_c_re[l], ssm_c_im[l], ssm_d[l])
        y = _s5_scan(u, tables, bsz, seq)
        h = _merge(h, attn, y, ga, gs, w_glu[l].astype(BF16), w_attn_proj[l].astype(BF16),
                   w_ssm_proj[l].astype(BF16), w_out[l].astype(BF16), ln1_g[l][None], ln1_b[l][None], alpha)
        h = _ffn(h, w_up[l].astype(BF16), conv_w[l], conv_b[l], w_down[l].astype(BF16),
                 ln2_g[l][None], ln2_b[l][None], alpha, seq)
    return h.reshape(bsz, seq, dm)
```

```python
import functools
import math

import jax
import jax.numpy as jnp
from jax import lax
from jax.experimental import pallas as pl
from jax.experimental.pallas import tpu as pltpu

F32 = jnp.float32
BF16 = jnp.bfloat16

N_HEADS = 8
HEAD_DIM = 64
ROT_DIM = HEAD_DIM // 4
ROPE_THETA = 500000.0
MOBA_BLOCK = 256
MOBA_TOP_K = 3
SSM_GROUP = 16
SSM_STATE = 64
CONV_WIDTH = 3
LN_EPS = 1e-5
NEG_INF = -1e30

LANES = 128
SUBLANES = 8
VMEM_LIMIT = 56 * 1024 * 1024

SSM_CHUNK = 16
ROW_TILE = 512
FF_CHUNK = 256

_NT = (((1,), (1,)), ((), ()))


def _dot(a, b):
    return jnp.dot(a, b, preferred_element_type=F32)


def _dot_nt(a, b):
    return lax.dot_general(a, b, _NT, preferred_element_type=F32)


def _layer_norm(t, g, b):
    mu = jnp.mean(t, axis=-1, keepdims=True)
    d = t - mu
    var = jnp.mean(d * d, axis=-1, keepdims=True)
    return d * lax.rsqrt(var + LN_EPS) * g + b


def _inproj_kernel(x_ref, w_ref, wvt_ref, cos_ref, s1_ref, s2_ref,
                   q_ref, k_ref, vt_ref, u_ref, ga_ref, gs_ref, km_ref, *, aw, sw, dm, scale):
    xb = x_ref[...].astype(BF16)
    cos_t, s1_t, s2_t = cos_ref[...], s1_ref[...], s2_ref[...]
    tm = xb.shape[0]

    def rotary(t):
        return t * cos_t + pltpu.roll(t, ROT_DIM // 2, 1) * s1_t + pltpu.roll(t, LANES - ROT_DIM // 2, 1) * s2_t

    q = _dot(xb, w_ref[:, 0:aw])
    for j in range(aw // LANES):
        sl = slice(j * LANES, (j + 1) * LANES)
        q_ref[:, sl] = (rotary(q[:, sl]) * scale).astype(BF16)
    k = _dot(xb, w_ref[:, aw:2 * aw])
    for j in range(aw // LANES):
        sl = slice(j * LANES, (j + 1) * LANES)
        kr = rotary(k[:, sl])
        k_ref[:, sl] = kr.astype(BF16)
        for r in range(tm // MOBA_BLOCK):
            km_ref[r, :, sl] = jnp.mean(kr[r * MOBA_BLOCK:(r + 1) * MOBA_BLOCK], axis=0, keepdims=True)
    vt = _dot_nt(wvt_ref[...], xb)
    for r in range(tm // MOBA_BLOCK):
        for j in range(aw // LANES):
            vt_ref[r, j] = vt[j * LANES:(j + 1) * LANES, r * MOBA_BLOCK:(r + 1) * MOBA_BLOCK].astype(BF16)
    u_ref[...] = _dot(xb, w_ref[:, 3 * aw:3 * aw + sw]).astype(BF16)
    c0 = 3 * aw + sw
    ga_ref[...] = jax.nn.sigmoid(_dot(xb, w_ref[:, c0:c0 + dm])).astype(BF16)
    gs_ref[...] = jax.nn.sigmoid(_dot(xb, w_ref[:, c0 + dm:c0 + 2 * dm])).astype(BF16)


def _rotary_tables(seq):
    half = ROT_DIM // 2
    inv_freq = ROPE_THETA ** (-jnp.arange(0, ROT_DIM, 2, dtype=F32) / ROT_DIM)
    ang = jnp.arange(seq, dtype=F32)[:, None] * inv_freq[None, :]
    cos, sin = jnp.cos(ang), jnp.sin(ang)
    ones = jnp.ones((seq, HEAD_DIM - ROT_DIM), F32)
    zeros = jnp.zeros((seq, HEAD_DIM - ROT_DIM), F32)
    zh = jnp.zeros((seq, half), F32)
    cos_h = jnp.concatenate([cos, cos, ones], axis=1)
    s1_h = jnp.concatenate([zh, sin, zeros], axis=1)
    s2_h = jnp.concatenate([-sin, zh, zeros], axis=1)
    rep = LANES // HEAD_DIM
    return tuple(jnp.tile(t, (1, rep)) for t in (cos_h, s1_h, s2_h))


def _in_projection(x2, w_in, seq):
    n, dm = x2.shape
    aw, sw = N_HEADS * HEAD_DIM, w_in.shape[1] - 3 * N_HEADS * HEAD_DIM - 2 * dm
    tm = ROW_TILE
    cos_t, s1_t, s2_t = _rotary_tables(seq)
    w_vt = w_in[:, 2 * aw:3 * aw].T
    hp = aw // LANES
    tiles_per_seq = seq // tm
    row = lambda i: (i, 0)
    tab = lambda i: (i % tiles_per_seq, 0)
    out_shape = (
        jax.ShapeDtypeStruct((n, aw), BF16), jax.ShapeDtypeStruct((n, aw), BF16),
        jax.ShapeDtypeStruct((n // MOBA_BLOCK, hp, LANES, MOBA_BLOCK), BF16), jax.ShapeDtypeStruct((n, sw), BF16),
        jax.ShapeDtypeStruct((n, dm), BF16), jax.ShapeDtypeStruct((n, dm), BF16),
        jax.ShapeDtypeStruct((n // MOBA_BLOCK, 1, aw), F32),
    )
    return pl.pallas_call(
        functools.partial(_inproj_kernel, aw=aw, sw=sw, dm=dm, scale=HEAD_DIM ** -0.5 * math.log2(math.e)),
        grid=(n // tm,),
        in_specs=[
            pl.BlockSpec((tm, dm), row),
            pl.BlockSpec(w_in.shape, lambda i: (0, 0)),
            pl.BlockSpec(w_vt.shape, lambda i: (0, 0)),
            pl.BlockSpec((tm, LANES), tab), pl.BlockSpec((tm, LANES), tab), pl.BlockSpec((tm, LANES), tab),
        ],
        out_specs=(
            pl.BlockSpec((tm, aw), row), pl.BlockSpec((tm, aw), row),
            pl.BlockSpec((tm // MOBA_BLOCK, hp, LANES, MOBA_BLOCK), lambda i: (i, 0, 0, 0)),
            pl.BlockSpec((tm, sw), row), pl.BlockSpec((tm, dm), row), pl.BlockSpec((tm, dm), row),
            pl.BlockSpec((tm // MOBA_BLOCK, 1, aw), lambda i: (i, 0, 0)),
        ),
        out_shape=out_shape,
        compiler_params=pltpu.CompilerParams(dimension_semantics=("parallel",), vmem_limit_bytes=VMEM_LIMIT),
        name="in_projection",
    )(x2, w_in, w_vt, cos_t, s1_t, s2_t)


PAIR = 2 * MOBA_BLOCK
MASKED = 2.0 * NEG_INF
BIAS_PAD = 8


def _select_bias(gate, own, nb):
    blk = lax.broadcasted_iota(jnp.int32, gate.shape, 0)
    past = blk < own
    g = jnp.where(past, gate, NEG_INF)
    bias = jnp.full(gate.shape, MASKED, F32)
    for _ in range(min(MOBA_TOP_K, nb)):
        mx = jnp.max(g, axis=0, keepdims=True)
        first = jnp.min(jnp.where(g == mx, blk, nb), axis=0, keepdims=True)
        pick = blk == first
        bias = jnp.where(jnp.logical_and(pick, past), 0.0, bias)
        g = jnp.where(pick, -jnp.inf, g)
    pad = jnp.full((BIAS_PAD, gate.shape[1]), MASKED, F32)
    return jnp.concatenate([bias, pad], axis=0)


def _attn_kernel(q_ref, k_ref, vt_ref, km_ref, o_ref,
                 va_ref, vb_ref, bias_ref, s_ref, p_ref, cm_ref, al_ref, acc_ref, *, nb):
    own = pl.program_id(2)
    blk = MOBA_BLOCK
    n_pairs = nb // 2
    v_refs = (va_ref, vb_ref)
    row_d = lax.broadcasted_iota(jnp.int32, (LANES, blk), 0)
    head_rows = (row_d < HEAD_DIM, row_d >= HEAD_DIM)

    @pl.when(own == 0)
    def _():
        def fill(j, c):
            for half in range(2):
                vt = vt_ref[0, 2 * j + half, 0].astype(F32)
                for h in range(2):
                    v_refs[h][j, :, half * blk:(half + 1) * blk] = jnp.where(head_rows[h], vt, 1.0).astype(BF16)
            return c
        lax.fori_loop(0, n_pairs, fill, 0)

    q2 = q_ref[0].astype(F32)
    lane = lax.broadcasted_iota(jnp.int32, q2.shape, 1)
    qh = (jnp.where(lane < HEAD_DIM, q2, 0.0).astype(BF16), jnp.where(lane >= HEAD_DIM, q2, 0.0).astype(BF16))

    kmb = km_ref[0].astype(BF16)
    for h in range(2):
        bias_ref[h] = _select_bias(_dot_nt(kmb, qh[h]), own, nb)

    kd = k_ref[0, pl.ds(pl.multiple_of(own * blk, blk), blk), :]
    key_i = lax.broadcasted_iota(jnp.int32, (blk, blk), 0)
    qry_i = lax.broadcasted_iota(jnp.int32, (blk, blk), 1)
    causal = key_i <= qry_i
    vt_own = vt_ref[0, own, 0].astype(F32)
    m0 = []
    for h in range(2):
        s = jnp.where(causal, _dot_nt(kd, qh[h]), NEG_INF)
        m = jnp.max(s, axis=0, keepdims=True)
        p = jnp.exp2(s - m)
        acc_ref[h] = _dot(jnp.where(head_rows[h], vt_own, 1.0).astype(BF16), p.astype(BF16))
        m0.append(m)

    def stage_a(t, slot):
        j = jnp.minimum(t, n_pairs - 1)
        kp = k_ref[0, pl.ds(pl.multiple_of(j * PAIR, PAIR), PAIR), :]
        for h in range(2):
            s = _dot_nt(kp, qh[h])
            s_ref[slot, h] = s
            cm_ref[slot, 2 * h:2 * h + 1, :] = jnp.max(s[:blk], axis=0, keepdims=True)
            cm_ref[slot, 2 * h + 1:2 * h + 2, :] = jnp.max(s[blk:], axis=0, keepdims=True)

    def stage_b(t, slot, m):
        m_out = []
        for h in range(2):
            b0 = bias_ref[h, pl.ds(2 * t, 1), :]
            b1 = bias_ref[h, pl.ds(2 * t + 1, 1), :]
            cm0 = cm_ref[slot, 2 * h:2 * h + 1, :]
            cm1 = cm_ref[slot, 2 * h + 1:2 * h + 2, :]
            m_new = jnp.maximum(m[h], jnp.maximum(cm0 + b0, cm1 + b1))
            al_ref[slot, h:h + 1, :] = jnp.exp2(m[h] - m_new)
            p_ref[slot, h, 0:blk] = jnp.exp2(s_ref[slot, h, 0:blk] - (m_new - b0)).astype(BF16)
            p_ref[slot, h, blk:PAIR] = jnp.exp2(s_ref[slot, h, blk:PAIR] - (m_new - b1)).astype(BF16)
            m_out.append(m_new)
        return tuple(m_out)

    def stage_c(t, slot):
        j = jnp.minimum(t, n_pairs - 1)
        for h in range(2):
            pv = _dot(v_refs[h][j], p_ref[slot, h])
            acc_ref[h] = acc_ref[h] * al_ref[slot, h:h + 1, :] + pv

    stage_a(0, 0)
    m1 = stage_b(0, 0, tuple(m0))
    stage_a(1, 1)

    def trip(t, slot, m):
        stage_a(t, slot)
        m = stage_b(t - 1, 1 - slot, m)
        stage_c(t - 2, slot)
        return m

    def double_trip(i, m):
        t = 2 + 2 * i
        return trip(t + 1, 1, trip(t, 0, m))

    n_trips = (own + 1) // 2
    lax.fori_loop(0, (n_trips + 1) // 2, double_trip, m1)

    acc_a, acc_b = acc_ref[0], acc_ref[1]
    o_a = acc_a[:HEAD_DIM] / acc_a[HEAD_DIM:]
    o_b = acc_b[HEAD_DIM:] / acc_b[:HEAD_DIM]
    o_ref[0] = jnp.concatenate([o_a, o_b], axis=0).T.astype(o_ref.dtype)


def _moba_attention(q, k, vt, kmean):
    bsz, seq, aw = q.shape
    nb = seq // MOBA_BLOCK
    hp = aw // LANES
    assert nb % 2 == 0
    return pl.pallas_call(
        functools.partial(_attn_kernel, nb=nb),
        grid=(bsz, hp, nb),
        in_specs=[
            pl.BlockSpec((1, MOBA_BLOCK, LANES), lambda b, h, i: (b, i, h)),
            pl.BlockSpec((1, seq, LANES), lambda b, h, i: (b, 0, h)),
            pl.BlockSpec((1, nb, 1, LANES, MOBA_BLOCK), lambda b, h, i: (b, 0, h, 0, 0)),
            pl.BlockSpec((1, nb, LANES), lambda b, h, i: (b, 0, h)),
        ],
        out_specs=pl.BlockSpec((1, MOBA_BLOCK, LANES), lambda b, h, i: (b, i, h)),
        out_shape=jax.ShapeDtypeStruct((bsz, seq, aw), BF16),
        scratch_shapes=[
            pltpu.VMEM((nb // 2, LANES, PAIR), BF16), pltpu.VMEM((nb // 2, LANES, PAIR), BF16),
            pltpu.VMEM((2, nb + BIAS_PAD, MOBA_BLOCK), F32),
            pltpu.VMEM((2, 2, PAIR, MOBA_BLOCK), F32),
            pltpu.VMEM((2, 2, PAIR, MOBA_BLOCK), BF16),
            pltpu.VMEM((2, SUBLANES, MOBA_BLOCK), F32),
            pltpu.VMEM((2, SUBLANES, MOBA_BLOCK), F32),
            pltpu.VMEM((2, LANES, MOBA_BLOCK), F32),
        ],
        compiler_params=pltpu.CompilerParams(
            dimension_semantics=("parallel", "parallel", "arbitrary"), vmem_limit_bytes=VMEM_LIMIT),
        name="moba_attention",
    )(q, k, vt, kmean)


def _s5_tables(a_re, a_im, log_dt, b_re, b_im, c_re, c_im, d_skip):
    hi = lax.Precision.HIGHEST
    L = SSM_CHUNK
    g, p = a_re.shape
    dt = jnp.exp(log_dt)[:, None]
    mag = jnp.exp(a_re * dt)
    lb_re, lb_im = mag * jnp.cos(a_im * dt), mag * jnp.sin(a_im * dt)
    den = a_re * a_re + a_im * a_im
    n_re, n_im = lb_re - 1.0, lb_im
    z_re = (n_re * a_re + n_im * a_im) / den
    z_im = (n_im * a_re - n_re * a_im) / den
    j = jnp.arange(L + 1, dtype=F32)[None, :, None]
    pmag = jnp.exp(j * (a_re * dt)[:, None, :])
    ang = j * (a_im * dt)[:, None, :]
    pr, pi = pmag * jnp.cos(ang), pmag * jnp.sin(ang)
    zb_re = z_re[:, :, None] * b_re - z_im[:, :, None] * b_im
    zb_im = z_re[:, :, None] * b_im + z_im[:, :, None] * b_re
    cl_re = c_re[:, None] * pr[:, :, None, :] - c_im[:, None] * pi[:, :, None, :]
    cl_im = c_re[:, None] * pi[:, :, None, :] + c_im[:, None] * pr[:, :, None, :]
    kern = (jnp.einsum('gjop,gpi->gjoi', cl_re[:, :L], zb_re, precision=hi)
            - jnp.einsum('gjop,gpi->gjoi', cl_im[:, :L], zb_im, precision=hi))
    kern = kern.at[:, 0].add(jax.vmap(jnp.diag)(d_skip))
    s_i = jnp.arange(L)[:, None]
    t_i = jnp.arange(L)[None, :]
    lag = t_i - s_i
    toep = jnp.where((lag >= 0)[None, :, :, None, None], kern[:, jnp.maximum(lag, 0)], 0.0)
    toep = toep.transpose(0, 1, 4, 2, 3).reshape(g, L * SSM_GROUP, L * SSM_GROUP)
    rev = L - 1 - jnp.arange(L)
    w_re = pr[:, rev][:, :, :, None] * zb_re[:, None] - pi[:, rev][:, :, :, None] * zb_im[:, None]
    w_im = pr[:, rev][:, :, :, None] * zb_im[:, None] + pi[:, rev][:, :, :, None] * zb_re[:, None]
    w_re = w_re.transpose(0, 1, 3, 2).reshape(g, L * SSM_GROUP, p)
    w_im = w_im.transpose(0, 1, 3, 2).reshape(g, L * SSM_GROUP, p)
    b_pow = jnp.concatenate([w_re, w_im], axis=-1)
    b_pow_sw = jnp.concatenate([w_im, w_re], axis=-1)
    cp_re = cl_re[:, 1:].transpose(0, 3, 1, 2).reshape(g, p, L * SSM_GROUP)
    cp_im = cl_im[:, 1:].transpose(0, 3, 1, 2).reshape(g, p, L * SSM_GROUP)
    c_pow = jnp.concatenate([cp_re, -cp_im], axis=1)
    lr, li = pr[:, L], pi[:, L]
    lam = jnp.stack([jnp.concatenate([lr, lr], -1), jnp.concatenate([-li, li], -1)], axis=1)
    return toep.astype(BF16), b_pow.astype(BF16), b_pow_sw.astype(BF16), c_pow.astype(BF16), lam


def _s5_kernel(u_ref, toep_ref, bp_ref, bps_ref, cp_ref, lam_ref, y_ref, s_ref, ss_ref, hp_ref, *, bsz):
    u = u_ref[0]
    s_ref[...] = _dot(u, bp_ref[0])
    ss_ref[...] = _dot(u, bps_ref[0])
    a = lam_ref[0, 0:1, :]
    bv = lam_ref[0, 1:2, :]
    rows = u.shape[0]
    per = SUBLANES // bsz
    row_i = lax.broadcasted_iota(jnp.int32, (SUBLANES, 2 * SSM_STATE), 0)

    def body(kk, carry):
        h, hs = carry
        r0 = pl.multiple_of(kk * SUBLANES, SUBLANES)
        s8 = s_ref[pl.ds(r0, SUBLANES), :]
        ss8 = ss_ref[pl.ds(r0, SUBLANES), :]
        hprev = h
        for c in range(per):
            h_new = a * h + bv * hs + s8
            hs_new = a * hs - bv * h + ss8
            h = pltpu.roll(h_new, bsz, 0) if per > 1 else h_new
            hs = pltpu.roll(hs_new, bsz, 0) if per > 1 else hs_new
            if c + 1 < per:
                hprev = jnp.where(row_i < (c + 1) * bsz, hprev, h)
        hp_ref[pl.ds(r0, SUBLANES), :] = hprev
        return h, hs

    zero = jnp.zeros((SUBLANES, 2 * SSM_STATE), F32)
    lax.fori_loop(0, rows // SUBLANES, body, (zero, zero))
    y = _dot(u, toep_ref[0]) + _dot(hp_ref[...].astype(BF16), cp_ref[0])
    y_ref[0] = y


def _s5_scan(u, tables, bsz, seq):
    toep, b_pow, b_pow_sw, c_pow, lam = tables
    g = toep.shape[0]
    L = SSM_CHUNK
    nc = seq // L
    w = L * SSM_GROUP
    p2 = 2 * SSM_STATE
    assert SUBLANES % bsz == 0 and (nc * bsz) % SUBLANES == 0
    ug = u.reshape(bsz, nc, L, g, SSM_GROUP).transpose(3, 1, 0, 2, 4).reshape(g, nc * bsz, w)
    rows = nc * bsz
    grp = lambda i: (i, 0, 0)
    y = pl.pallas_call(
        functools.partial(_s5_kernel, bsz=bsz),
        grid=(g,),
        in_specs=[
            pl.BlockSpec((1, rows, w), grp), pl.BlockSpec((1, w, w), grp),
            pl.BlockSpec((1, w, p2), grp), pl.BlockSpec((1, w, p2), grp),
            pl.BlockSpec((1, p2, w), grp), pl.BlockSpec((1, 2, p2), grp),
        ],
        out_specs=pl.BlockSpec((1, rows, w), grp),
        out_shape=jax.ShapeDtypeStruct((g, rows, w), F32),
        scratch_shapes=[pltpu.VMEM((rows, p2), F32), pltpu.VMEM((rows, p2), F32), pltpu.VMEM((rows, p2), F32)],
        compiler_params=pltpu.CompilerParams(dimension_semantics=("parallel",), vmem_limit_bytes=VMEM_LIMIT),
        name="s5_scan",
    )(ug, toep, b_pow, b_pow_sw, c_pow, lam)
    return y.reshape(g, nc, bsz, L, SSM_GROUP).transpose(2, 1, 3, 0, 4).reshape(bsz * seq, g * SSM_GROUP)


def _merge_kernel(x_ref, attn_ref, y_ref, ga_ref, gs_ref, wglu_ref, wap_ref, wsp_ref, wout_ref, g_ref, b_ref,
                  h_ref, *, alpha):
    ssm = jax.nn.gelu(y_ref[...])
    ssm = ssm * jax.nn.sigmoid(_dot(ssm.astype(BF16), wglu_ref[...]))
    merged = (ga_ref[...].astype(F32) * _dot(attn_ref[...], wap_ref[...])
              + gs_ref[...].astype(F32) * _dot(ssm.astype(BF16), wsp_ref[...]))
    mix = _dot(merged.astype(BF16), wout_ref[...])
    h_ref[...] = _layer_norm(alpha * x_ref[...] + mix, g_ref[...], b_ref[...])


def _merge(x2, attn, y, ga, gs, w_glu, w_ap, w_sp, w_out, ln_g, ln_b, alpha):
    n, dm = x2.shape
    aw, sw = attn.shape[1], y.shape[1]
    tm = ROW_TILE
    row = lambda i: (i, 0)
    full = lambda i: (0, 0)
    return pl.pallas_call(
        functools.partial(_merge_kernel, alpha=alpha),
        grid=(n // tm,),
        in_specs=[
            pl.BlockSpec((tm, dm), row), pl.BlockSpec((tm, aw), row), pl.BlockSpec((tm, sw), row),
            pl.BlockSpec((tm, dm), row), pl.BlockSpec((tm, dm), row),
            pl.BlockSpec(w_glu.shape, full), pl.BlockSpec(w_ap.shape, full), pl.BlockSpec(w_sp.shape, full),
            pl.BlockSpec(w_out.shape, full), pl.BlockSpec((1, dm), full), pl.BlockSpec((1, dm), full),
        ],
        out_specs=pl.BlockSpec((tm, dm), row),
        out_shape=jax.ShapeDtypeStruct((n, dm), F32),
        compiler_params=pltpu.CompilerParams(dimension_semantics=("parallel",), vmem_limit_bytes=VMEM_LIMIT),
        name="merge_ln1",
    )(x2, attn, y, ga, gs, w_glu, w_ap, w_sp, w_out, ln_g, ln_b)


def _ffn_kernel(h_ref, wup_ref, cw_ref, wd_ref, g_ref, b_ref, o_ref, halo_ref, act_ref,
                *, alpha, tiles_per_seq, dff):
    i = pl.program_id(0)
    h = h_ref[...]
    hb = h.astype(BF16)
    tm = h.shape[0]
    ck = FF_CHUNK
    row8 = lax.broadcasted_iota(jnp.int32, (SUBLANES, ck), 0)

    @pl.when((i % tiles_per_seq) == 0)
    def _():
        halo_ref[...] = jnp.zeros_like(halo_ref)

    def conv(up, c0):
        cols = slice(c0, c0 + ck)
        prev = halo_ref[:, cols]
        halo_ref[:, cols] = up[tm - SUBLANES:]
        p1 = prev[SUBLANES - 1:SUBLANES]
        p2 = prev[SUBLANES - 2:SUBLANES - 1]
        r1 = pltpu.roll(up, 1, 0)
        r2 = pltpu.roll(up, 2, 0)
        d1 = jnp.concatenate([jnp.where(row8 == 0, p1, r1[:SUBLANES]), r1[SUBLANES:]], axis=0)
        d2 = jnp.concatenate([jnp.where(row8 == 0, p2, jnp.where(row8 == 1, p1, r2[:SUBLANES])), r2[SUBLANES:]],
                             axis=0)
        cw = cw_ref[:, cols]
        return cw[3:4] + cw[0:1] * d2 + cw[1:2] * d1 + cw[2:3] * up

    for j in range(dff // ck):
        c0 = j * ck
        val = conv(_dot(hb, wup_ref[:, c0:c0 + ck]), c0)
        gate = conv(_dot(hb, wup_ref[:, dff + c0:dff + c0 + ck]), dff + c0)
        act_ref[:, c0:c0 + ck] = (jax.nn.gelu(gate) * val).astype(BF16)
    ff = _dot(act_ref[...], wd_ref[...])
    o_ref[...] = _layer_norm(alpha * h + ff, g_ref[...], b_ref[...])


def _ffn(h, w_up, conv_w, conv_b, w_down, ln_g, ln_b, alpha, seq):
    n, dm = h.shape
    dff = w_down.shape[0]
    tm = ROW_TILE
    assert dff % FF_CHUNK == 0
    cw = jnp.concatenate([conv_w, conv_b[None, :]], axis=0)
    cw = jnp.pad(cw, ((0, SUBLANES - cw.shape[0]), (0, 0)))
    row = lambda i: (i, 0)
    full = lambda i: (0, 0)
    once = pl.Buffered(1)
    return pl.pallas_call(
        functools.partial(_ffn_kernel, alpha=alpha, tiles_per_seq=seq // tm, dff=dff),
        grid=(n // tm,),
        in_specs=[
            pl.BlockSpec((tm, dm), row),
            pl.BlockSpec(w_up.shape, full, pipeline_mode=once),
            pl.BlockSpec(cw.shape, full),
            pl.BlockSpec(w_down.shape, full, pipeline_mode=once),
            pl.BlockSpec((1, dm), full), pl.BlockSpec((1, dm), full),
        ],
        out_specs=pl.BlockSpec((tm, dm), row),
        out_shape=jax.ShapeDtypeStruct((n, dm), F32),
        scratch_shapes=[
            pltpu.VMEM((SUBLANES, 2 * dff), F32),
            pltpu.VMEM((tm, dff), BF16),
        ],
        compiler_params=pltpu.CompilerParams(dimension_semantics=("arbitrary",), vmem_limit_bytes=VMEM_LIMIT),
        name="conv_ffn_ln2",
    )(h, w_up, cw, w_down, ln_g, ln_b)


def kernel(x, w_in, w_attn_proj, ssm_a_re, ssm_a_im, ssm_log_dt, ssm_b_re, ssm_b_im, ssm_c_re, ssm_c_im, ssm_d,
           w_glu, w_ssm_proj, w_out, ln1_g, ln1_b, w_up, conv_w, conv_b, w_down, ln2_g, ln2_b):
    bsz, seq, dm = x.shape
    depth = w_in.shape[0]
    alpha = (2.0 * depth) ** 0.25
    n = bsz * seq
    assert seq % ROW_TILE == 0 and ROW_TILE % MOBA_BLOCK == 0 and seq % SSM_CHUNK == 0
    h = x.reshape(n, dm)
    for l in range(depth):
        q, k, vt, u, ga, gs, kmean = _in_projection(h, w_in[l].astype(BF16), seq)
        aw = q.shape[1]
        nb = seq // MOBA_BLOCK
        attn = _moba_attention(q.reshape(bsz, seq, aw), k.reshape(bsz, seq, aw), vt.reshape(bsz, nb, *vt.shape[1:]),
                               kmean.reshape(bsz, nb, aw)).reshape(n, aw)
        tables = _s5_tables(ssm_a_re[l], ssm_a_im[l], ssm_log_dt[l], ssm_b_re[l], ssm_b_im[l],
                            ssm_c_re[l], ssm_c_im[l], ssm_d[l])
        y = _s5_scan(u, tables, bsz, seq)
        h = _merge(h, attn, y, ga, gs, w_glu[l].astype(BF16), w_attn_proj[l].astype(BF16),
                   w_ssm_proj[l].astype(BF16), w_out[l].astype(BF16), ln1_g[l][None], ln1_b[l][None], alpha)
        h = _ffn(h, w_up[l].astype(BF16), conv_w[l], conv_b[l], w_down[l].astype(BF16),
                 ln2_g[l][None], ln2_b[l][None], alpha, seq)
    return h.reshape(bsz, seq, dm)
```

```python
import functools
import math

import jax
import jax.numpy as jnp
from jax import lax
from jax.experimental import pallas as pl
from jax.experimental.pallas import tpu as pltpu

F32 = jnp.float32
BF16 = jnp.bfloat16

N_HEADS = 8
HEAD_DIM = 64
ROT_DIM = HEAD_DIM // 4
ROPE_THETA = 500000.0
MOBA_BLOCK = 256
MOBA_TOP_K = 3
SSM_GROUP = 16
SSM_STATE = 64
CONV_WIDTH = 3
LN_EPS = 1e-5
NEG_INF = -1e30

LANES = 128
SUBLANES = 8
VMEM_LIMIT = 56 * 1024 * 1024

SSM_CHUNK = 16
ROW_TILE = 512
FF_CHUNK = 256

_NT = (((1,), (1,)), ((), ()))


def _dot(a, b):
    return jnp.dot(a, b, preferred_element_type=F32)


def _dot_nt(a, b):
    return lax.dot_general(a, b, _NT, preferred_element_type=F32)


def _layer_norm(t, g, b):
    mu = jnp.mean(t, axis=-1, keepdims=True)
    d = t - mu
    var = jnp.mean(d * d, axis=-1, keepdims=True)
    return d * lax.rsqrt(var + LN_EPS) * g + b


def _inproj_kernel(x_ref, w_ref, wvt_ref, cos_ref, s1_ref, s2_ref,
                   q_ref, k_ref, vt_ref, u_ref, ga_ref, gs_ref, km_ref, *, aw, sw, dm, scale):
    xb = x_ref[...].astype(BF16)
    cos_t, s1_t, s2_t = cos_ref[...], s1_ref[...], s2_ref[...]
    tm = xb.shape[0]

    def rotary(t):
        return t * cos_t + pltpu.roll(t, ROT_DIM // 2, 1) * s1_t + pltpu.roll(t, LANES - ROT_DIM // 2, 1) * s2_t

    q = _dot(xb, w_ref[:, 0:aw])
    for j in range(aw // LANES):
        sl = slice(j * LANES, (j + 1) * LANES)
        q_ref[:, sl] = (rotary(q[:, sl]) * scale).astype(BF16)
    k = _dot(xb, w_ref[:, aw:2 * aw])
    for j in range(aw // LANES):
        sl = slice(j * LANES, (j + 1) * LANES)
        kr = rotary(k[:, sl])
        k_ref[:, sl] = kr.astype(BF16)
        for r in range(tm // MOBA_BLOCK):
            km_ref[r, :, sl] = jnp.mean(kr[r * MOBA_BLOCK:(r + 1) * MOBA_BLOCK], axis=0, keepdims=True)
    vt = _dot_nt(wvt_ref[...], xb)
    for r in range(tm // MOBA_BLOCK):
        for j in range(aw // LANES):
            vt_ref[r, j] = vt[j * LANES:(j + 1) * LANES, r * MOBA_BLOCK:(r + 1) * MOBA_BLOCK].astype(BF16)
    u_ref[...] = _dot(xb, w_ref[:, 3 * aw:3 * aw + sw]).astype(BF16)
    c0 = 3 * aw + sw
    ga_ref[...] = jax.nn.sigmoid(_dot(xb, w_ref[:, c0:c0 + dm])).astype(BF16)
    gs_ref[...] = jax.nn.sigmoid(_dot(xb, w_ref[:, c0 + dm:c0 + 2 * dm])).astype(BF16)


def _rotary_tables(seq):
    half = ROT_DIM // 2
    inv_freq = ROPE_THETA ** (-jnp.arange(0, ROT_DIM, 2, dtype=F32) / ROT_DIM)
    ang = jnp.arange(seq, dtype=F32)[:, None] * inv_freq[None, :]
    cos, sin = jnp.cos(ang), jnp.sin(ang)
    ones = jnp.ones((seq, HEAD_DIM - ROT_DIM), F32)
    zeros = jnp.zeros((seq, HEAD_DIM - ROT_DIM), F32)
    zh = jnp.zeros((seq, half), F32)
    cos_h = jnp.concatenate([cos, cos, ones], axis=1)
    s1_h = jnp.concatenate([zh, sin, zeros], axis=1)
    s2_h = jnp.concatenate([-sin, zh, zeros], axis=1)
    rep = LANES // HEAD_DIM
    return tuple(jnp.tile(t, (1, rep)) for t in (cos_h, s1_h, s2_h))


def _in_projection(x2, w_in, seq):
    n, dm = x2.shape
    aw, sw = N_HEADS * HEAD_DIM, w_in.shape[1] - 3 * N_HEADS * HEAD_DIM - 2 * dm
    tm = ROW_TILE
    cos_t, s1_t, s2_t = _rotary_tables(seq)
    w_vt = w_in[:, 2 * aw:3 * aw].T
    hp = aw // LANES
    tiles_per_seq = seq // tm
    row = lambda i: (i, 0)
    tab = lambda i: (i % tiles_per_seq, 0)
    out_shape = (
        jax.ShapeDtypeStruct((n, aw), BF16), jax.ShapeDtypeStruct((n, aw), BF16),
        jax.ShapeDtypeStruct((n // MOBA_BLOCK, hp, LANES, MOBA_BLOCK), BF16), jax.ShapeDtypeStruct((n, sw), BF16),
        jax.ShapeDtypeStruct((n, dm), BF16), jax.ShapeDtypeStruct((n, dm), BF16),
        jax.ShapeDtypeStruct((n // MOBA_BLOCK, 1, aw), F32),
    )
    return pl.pallas_call(
        functools.partial(_inproj_kernel, aw=aw, sw=sw, dm=dm, scale=HEAD_DIM ** -0.5 * math.log2(math.e)),
        grid=(n // tm,),
        in_specs=[
            pl.BlockSpec((tm, dm), row),
            pl.BlockSpec(w_in.shape, lambda i: (0, 0)),
            pl.BlockSpec(w_vt.shape, lambda i: (0, 0)),
            pl.BlockSpec((tm, LANES), tab), pl.BlockSpec((tm, LANES), tab), pl.BlockSpec((tm, LANES), tab),
        ],
        out_specs=(
            pl.BlockSpec((tm, aw), row), pl.BlockSpec((tm, aw), row),
            pl.BlockSpec((tm // MOBA_BLOCK, hp, LANES, MOBA_BLOCK), lambda i: (i, 0, 0, 0)),
            pl.BlockSpec((tm, sw), row), pl.BlockSpec((tm, dm), row), pl.BlockSpec((tm, dm), row),
            pl.BlockSpec((tm // MOBA_BLOCK, 1, aw), lambda i: (i, 0, 0)),
        ),
        out_shape=out_shape,
        compiler_params=pltpu.CompilerParams(dimension_semantics=("parallel",), vmem_limit_bytes=VMEM_LIMIT),
        name="in_projection",
    )(x2, w_in, w_vt, cos_t, s1_t, s2_t)


PAIR = 2 * MOBA_BLOCK
Q_TILE = PAIR
MASKED = 2.0 * NEG_INF
V_ROWS = HEAD_DIM + 16
SEL_ROWS = SUBLANES
SEL_CHUNK = 1024


def _head_split(q2):
    qf = q2.astype(F32)
    lane = lax.broadcasted_iota(jnp.int32, qf.shape, 1)
    return (jnp.where(lane < HEAD_DIM, qf, 0.0).astype(BF16), jnp.where(lane >= HEAD_DIM, qf, 0.0).astype(BF16))


def _select_kernel(q_ref, km_ref, sel_ref, *, nb):
    kmb = km_ref[0].astype(BF16)
    seq = q_ref.shape[1]
    shape = (nb, SEL_CHUNK)
    blk = lax.broadcasted_iota(jnp.int32, shape, 0)
    for c in range(seq // SEL_CHUNK):
        cols = slice(c * SEL_CHUNK, (c + 1) * SEL_CHUNK)
        own = (lax.broadcasted_iota(jnp.int32, shape, 1) + c * SEL_CHUNK) // MOBA_BLOCK
        qh = _head_split(q_ref[0, cols, :])
        for h in range(2):
            g = jnp.where(blk < own, _dot_nt(kmb, qh[h]), NEG_INF)
            rows = []
            for _ in range(min(MOBA_TOP_K, nb)):
                mx = jnp.max(g, axis=0, keepdims=True)
                first = jnp.min(jnp.where(g == mx, blk, nb), axis=0, keepdims=True)
                rows.append(jnp.where(first < own[0:1], first, -1))
                g = jnp.where(blk == first, -jnp.inf, g)
            rows.append(jnp.full((SEL_ROWS - len(rows), SEL_CHUNK), -1, jnp.int32))
            sel_ref[0, 0, h, :, cols] = jnp.concatenate(rows, axis=0)


def _moba_select(q, kmean):
    bsz, seq, aw = q.shape
    nb = seq // MOBA_BLOCK
    hp = aw // LANES
    assert seq % SEL_CHUNK == 0
    return pl.pallas_call(
        functools.partial(_select_kernel, nb=nb),
        grid=(bsz, hp),
        in_specs=[
            pl.BlockSpec((1, seq, LANES), lambda b, h: (b, 0, h)),
            pl.BlockSpec((1, nb, LANES), lambda b, h: (b, 0, h)),
        ],
        out_specs=pl.BlockSpec((1, 1, 2, SEL_ROWS, seq), lambda b, h: (b, h, 0, 0, 0)),
        out_shape=jax.ShapeDtypeStruct((bsz, hp, 2, SEL_ROWS, seq), jnp.int32),
        compiler_params=pltpu.CompilerParams(
            dimension_semantics=("parallel", "parallel"), vmem_limit_bytes=VMEM_LIMIT),
        name="moba_select",
    )(q, kmean)


def _attn_kernel(q_ref, k_ref, vt_ref, sel_ref, o_ref,
                 va_ref, vb_ref, s0_ref, s1_ref, p0_ref, p1_ref, cm_ref, al_ref, acc_ref, *, nb):
    i = pl.program_id(2)
    blk = MOBA_BLOCK
    v_refs = (va_ref, vb_ref)
    s_refs = (s0_ref, s1_ref)
    p_refs = (p0_ref, p1_ref)

    @pl.when(i == 0)
    def _():
        ones = jnp.ones((V_ROWS - HEAD_DIM, blk), BF16)

        def fill(j, c):
            for half in range(2):
                vt = vt_ref[0, 2 * j + half, 0]
                cols = slice(half * blk, (half + 1) * blk)
                for h in range(2):
                    v_refs[h][j, 0:HEAD_DIM, cols] = vt[h * HEAD_DIM:(h + 1) * HEAD_DIM]
                    v_refs[h][j, HEAD_DIM:V_ROWS, cols] = ones
            return c
        lax.fori_loop(0, nb // 2, fill, 0)

    qh = _head_split(q_ref[0])
    sel = (sel_ref[0, 0, 0], sel_ref[0, 0, 1])

    def chosen(h, n):
        sv = sel[h]
        hit = sv[0:1] == n
        for r in range(1, MOBA_TOP_K):
            hit = jnp.logical_or(hit, sv[r:r + 1] == n)
        return hit

    def past_bias(pair):
        return [tuple(jnp.where(chosen(h, 2 * pair + half), 0.0, MASKED) for half in range(2)) for h in range(2)]

    key_i = lax.broadcasted_iota(jnp.int32, (blk, blk), 0)
    qry_i = lax.broadcasted_iota(jnp.int32, (blk, blk), 1)
    causal = key_i <= qry_i

    def own_mask(s, h):
        top = jnp.concatenate([jnp.where(causal, s[:blk, :blk], MASKED),
                               jnp.where(chosen(h, 2 * i)[:, blk:], s[:blk, blk:], MASKED)], axis=1)
        bot = jnp.concatenate([jnp.full((blk, blk), MASKED, F32),
                               jnp.where(causal, s[blk:, blk:], MASKED)], axis=1)
        return jnp.concatenate([top, bot], axis=0)

    def stage_a(pair, slot, own):
        kp = k_ref[0, pl.ds(pl.multiple_of(pair * PAIR, PAIR), PAIR), :]
        for h in range(2):
            s = _dot_nt(kp, qh[h])
            if own:
                s = own_mask(s, h)
            s_refs[slot][h] = s
            cm_ref[slot, 2 * h:2 * h + 1, :] = jnp.max(s[:blk], axis=0, keepdims=True)
            cm_ref[slot, 2 * h + 1:2 * h + 2, :] = jnp.max(s[blk:], axis=0, keepdims=True)

    def stage_b(slot, m, bias):
        m_out = []
        for h in range(2):
            cm0 = cm_ref[slot, 2 * h:2 * h + 1, :]
            cm1 = cm_ref[slot, 2 * h + 1:2 * h + 2, :]
            if bias is None:
                m_new = jnp.maximum(m[h], jnp.maximum(cm0, cm1))
                c0 = c1 = m_new
            else:
                b0, b1 = bias[h]
                m_new = jnp.maximum(m[h], jnp.maximum(cm0 + b0, cm1 + b1))
                c0, c1 = m_new - b0, m_new - b1
            al_ref[slot, h:h + 1, :] = jnp.exp2(m[h] - m_new)
            p_refs[slot][h, 0:blk] = jnp.exp2(s_refs[slot][h, 0:blk] - c0).astype(BF16)
            p_refs[slot][h, blk:PAIR] = jnp.exp2(s_refs[slot][h, blk:PAIR] - c1).astype(BF16)
            m_out.append(m_new)
        return tuple(m_out)

    def stage_c(pair, slot):
        for h in range(2):
            pv = _dot(v_refs[h][pair], p_refs[slot][h])
            acc_ref[h] = acc_ref[h] * al_ref[slot, h:h + 1, :] + pv

    def trip(t, slot, m):
        stage_a(t - 1, slot, False)
        m = stage_b(1 - slot, m, past_bias(t - 2))
        stage_c(jnp.where(t == 2, i, t - 3), slot)
        return m

    acc_ref[...] = jnp.zeros_like(acc_ref)
    floor = jnp.full((1, Q_TILE), NEG_INF, F32)
    stage_a(i, 0, True)
    stage_a(0, 1, False)
    m0 = stage_b(0, (floor, floor), None)
    n_list = i + 1

    @pl.when(i == 0)
    def _():
        stage_c(i, 0)

    @pl.when(i > 0)
    def _():
        def double_trip(d, m):
            t = 2 + 2 * d
            return trip(t + 1, 1, trip(t, 0, m))

        m1 = lax.fori_loop(0, (n_list - 2) // 2, double_trip, m0)

        @pl.when(n_list % 2 == 1)
        def _():
            m2 = trip(n_list - 1, 0, m1)
            stage_b(0, m2, past_bias(n_list - 2))
            stage_c(n_list - 3, 1)
            stage_c(n_list - 2, 0)

        @pl.when(n_list % 2 == 0)
        def _():
            stage_b(1, m1, past_bias(n_list - 2))
            stage_c(jnp.where(n_list == 2, i, n_list - 3), 0)
            stage_c(n_list - 2, 1)

    outs = []
    for h in range(2):
        acc = acc_ref[h]
        outs.append(acc[:HEAD_DIM] / acc[HEAD_DIM:HEAD_DIM + 1])
    o_ref[0] = jnp.concatenate(outs, axis=0).T.astype(o_ref.dtype)


def _moba_attention(q, k, vt, sel):
    bsz, seq, aw = q.shape
    nb = seq // MOBA_BLOCK
    hp = aw // LANES
    assert nb % 2 == 0
    return pl.pallas_call(
        functools.partial(_attn_kernel, nb=nb),
        grid=(bsz, hp, nb // 2),
        in_specs=[
            pl.BlockSpec((1, Q_TILE, LANES), lambda b, h, i: (b, i, h)),
            pl.BlockSpec((1, seq, LANES), lambda b, h, i: (b, 0, h)),
            pl.BlockSpec((1, nb, 1, LANES, MOBA_BLOCK), lambda b, h, i: (b, 0, h, 0, 0)),
            pl.BlockSpec((1, 1, 2, SEL_ROWS, Q_TILE), lambda b, h, i: (b, h, 0, 0, i)),
        ],
        out_specs=pl.BlockSpec((1, Q_TILE, LANES), lambda b, h, i: (b, i, h)),
        out_shape=jax.ShapeDtypeStruct((bsz, seq, aw), BF16),
        scratch_shapes=[
            pltpu.VMEM((nb // 2, V_ROWS, PAIR), BF16), pltpu.VMEM((nb // 2, V_ROWS, PAIR), BF16),
            pltpu.VMEM((2, PAIR, Q_TILE), F32), pltpu.VMEM((2, PAIR, Q_TILE), F32),
            pltpu.VMEM((2, PAIR, Q_TILE), BF16), pltpu.VMEM((2, PAIR, Q_TILE), BF16),
            pltpu.VMEM((2, SUBLANES, Q_TILE), F32),
            pltpu.VMEM((2, SUBLANES, Q_TILE), F32),
            pltpu.VMEM((2, V_ROWS, Q_TILE), F32),
        ],
        compiler_params=pltpu.CompilerParams(
            dimension_semantics=("parallel", "parallel", "arbitrary"), vmem_limit_bytes=VMEM_LIMIT),
        name="moba_attention",
    )(q, k, vt, sel)


def _s5_tables(a_re, a_im, log_dt, b_re, b_im, c_re, c_im, d_skip):
    hi = lax.Precision.HIGHEST
    L = SSM_CHUNK
    g, p = a_re.shape
    dt = jnp.exp(log_dt)[:, None]
    mag = jnp.exp(a_re * dt)
    lb_re, lb_im = mag * jnp.cos(a_im * dt), mag * jnp.sin(a_im * dt)
    den = a_re * a_re + a_im * a_im
    n_re, n_im = lb_re - 1.0, lb_im
    z_re = (n_re * a_re + n_im * a_im) / den
    z_im = (n_im * a_re - n_re * a_im) / den
    j = jnp.arange(L + 1, dtype=F32)[None, :, None]
    pmag = jnp.exp(j * (a_re * dt)[:, None, :])
    ang = j * (a_im * dt)[:, None, :]
    pr, pi = pmag * jnp.cos(ang), pmag * jnp.sin(ang)
    zb_re = z_re[:, :, None] * b_re - z_im[:, :, None] * b_im
    zb_im = z_re[:, :, None] * b_im + z_im[:, :, None] * b_re
    cl_re = c_re[:, None] * pr[:, :, None, :] - c_im[:, None] * pi[:, :, None, :]
    cl_im = c_re[:, None] * pi[:, :, None, :] + c_im[:, None] * pr[:, :, None, :]
    kern = (jnp.einsum('gjop,gpi->gjoi', cl_re[:, :L], zb_re, precision=hi)
            - jnp.einsum('gjop,gpi->gjoi', cl_im[:, :L], zb_im, precision=hi))
    kern = kern.at[:, 0].add(jax.vmap(jnp.diag)(d_skip))
    s_i = jnp.arange(L)[:, None]
    t_i = jnp.arange(L)[None, :]
    lag = t_i - s_i
    toep = jnp.where((lag >= 0)[None, :, :, None, None], kern[:, jnp.maximum(lag, 0)], 0.0)
    toep = toep.transpose(0, 1, 4, 2, 3).reshape(g, L * SSM_GROUP, L * SSM_GROUP)
    rev = L - 1 - jnp.arange(L)
    w_re = pr[:, rev][:, :, :, None] * zb_re[:, None] - pi[:, rev][:, :, :, None] * zb_im[:, None]
    w_im = pr[:, rev][:, :, :, None] * zb_im[:, None] + pi[:, rev][:, :, :, None] * zb_re[:, None]
    w_re = w_re.transpose(0, 1, 3, 2).reshape(g, L * SSM_GROUP, p)
    w_im = w_im.transpose(0, 1, 3, 2).reshape(g, L * SSM_GROUP, p)
    b_pow = jnp.concatenate([w_re, w_im], axis=-1)
    b_pow_sw = jnp.concatenate([w_im, w_re], axis=-1)
    cp_re = cl_re[:, 1:].transpose(0, 3, 1, 2).reshape(g, p, L * SSM_GROUP)
    cp_im = cl_im[:, 1:].transpose(0, 3, 1, 2).reshape(g, p, L * SSM_GROUP)
    c_pow = jnp.concatenate([cp_re, -cp_im], axis=1)
    lr, li = pr[:, L], pi[:, L]
    lam = jnp.stack([jnp.concatenate([lr, lr], -1), jnp.concatenate([-li, li], -1)], axis=1)
    return toep.astype(BF16), b_pow.astype(BF16), b_pow_sw.astype(BF16), c_pow.astype(BF16), lam


def _s5_kernel(u_ref, toep_ref, bp_ref, bps_ref, cp_ref, lam_ref, y_ref, s_ref, ss_ref, hp_ref, *, bsz):
    u = u_ref[0]
    s_ref[...] = _dot(u, bp_ref[0])
    ss_ref[...] = _dot(u, bps_ref[0])
    a = lam_ref[0, 0:1, :]
    bv = lam_ref[0, 1:2, :]
    rows = u.shape[0]
    per = SUBLANES // bsz
    row_i = lax.broadcasted_iota(jnp.int32, (SUBLANES, 2 * SSM_STATE), 0)

    def body(kk, carry):
        h, hs = carry
        r0 = pl.multiple_of(kk * SUBLANES, SUBLANES)
        s8 = s_ref[pl.ds(r0, SUBLANES), :]
        ss8 = ss_ref[pl.ds(r0, SUBLANES), :]
        hprev = h
        for c in range(per):
            h_new = a * h + bv * hs + s8
            hs_new = a * hs - bv * h + ss8
            h = pltpu.roll(h_new, bsz, 0) if per > 1 else h_new
            hs = pltpu.roll(hs_new, bsz, 0) if per > 1 else hs_new
            if c + 1 < per:
                hprev = jnp.where(row_i < (c + 1) * bsz, hprev, h)
        hp_ref[pl.ds(r0, SUBLANES), :] = hprev
        return h, hs

    zero = jnp.zeros((SUBLANES, 2 * SSM_STATE), F32)
    lax.fori_loop(0, rows // SUBLANES, body, (zero, zero))
    y = _dot(u, toep_ref[0]) + _dot(hp_ref[...].astype(BF16), cp_ref[0])
    y_ref[0] = y


def _s5_scan(u, tables, bsz, seq):
    toep, b_pow, b_pow_sw, c_pow, lam = tables
    g = toep.shape[0]
    L = SSM_CHUNK
    nc = seq // L
    w = L * SSM_GROUP
    p2 = 2 * SSM_STATE
    assert SUBLANES % bsz == 0 and (nc * bsz) % SUBLANES == 0
    ug = u.reshape(bsz, nc, L, g, SSM_GROUP).transpose(3, 1, 0, 2, 4).reshape(g, nc * bsz, w)
    rows = nc * bsz
    grp = lambda i: (i, 0, 0)
    y = pl.pallas_call(
        functools.partial(_s5_kernel, bsz=bsz),
        grid=(g,),
        in_specs=[
            pl.BlockSpec((1, rows, w), grp), pl.BlockSpec((1, w, w), grp),
            pl.BlockSpec((1, w, p2), grp), pl.BlockSpec((1, w, p2), grp),
            pl.BlockSpec((1, p2, w), grp), pl.BlockSpec((1, 2, p2), grp),
        ],
        out_specs=pl.BlockSpec((1, rows, w), grp),
        out_shape=jax.ShapeDtypeStruct((g, rows, w), F32),
        scratch_shapes=[pltpu.VMEM((rows, p2), F32), pltpu.VMEM((rows, p2), F32), pltpu.VMEM((rows, p2), F32)],
        compiler_params=pltpu.CompilerParams(dimension_semantics=("parallel",), vmem_limit_bytes=VMEM_LIMIT),
        name="s5_scan",
    )(ug, toep, b_pow, b_pow_sw, c_pow, lam)
    return y.reshape(g, nc, bsz, L, SSM_GROUP).transpose(2, 1, 3, 0, 4).reshape(bsz * seq, g * SSM_GROUP)


def _merge_kernel(x_ref, attn_ref, y_ref, ga_ref, gs_ref, wglu_ref, wap_ref, wsp_ref, wout_ref, g_ref, b_ref,
                  h_ref, *, alpha):
    ssm = jax.nn.gelu(y_ref[...])
    ssm = ssm * jax.nn.sigmoid(_dot(ssm.astype(BF16), wglu_ref[...]))
    merged = (ga_ref[...].astype(F32) * _dot(attn_ref[...], wap_ref[...])
              + gs_ref[...].astype(F32) * _dot(ssm.astype(BF16), wsp_ref[...]))
    mix = _dot(merged.astype(BF16), wout_ref[...])
    h_ref[...] = _layer_norm(alpha * x_ref[...] + mix, g_ref[...], b_ref[...])


def _merge(x2, attn, y, ga, gs, w_glu, w_ap, w_sp, w_out, ln_g, ln_b, alpha):
    n, dm = x2.shape
    aw, sw = attn.shape[1], y.shape[1]
    tm = ROW_TILE
    row = lambda i: (i, 0)
    full = lambda i: (0, 0)
    return pl.pallas_call(
        functools.partial(_merge_kernel, alpha=alpha),
        grid=(n // tm,),
        in_specs=[
            pl.BlockSpec((tm, dm), row), pl.BlockSpec((tm, aw), row), pl.BlockSpec((tm, sw), row),
            pl.BlockSpec((tm, dm), row), pl.BlockSpec((tm, dm), row),
            pl.BlockSpec(w_glu.shape, full), pl.BlockSpec(w_ap.shape, full), pl.BlockSpec(w_sp.shape, full),
            pl.BlockSpec(w_out.shape, full), pl.BlockSpec((1, dm), full), pl.BlockSpec((1, dm), full),
        ],
        out_specs=pl.BlockSpec((tm, dm), row),
        out_shape=jax.ShapeDtypeStruct((n, dm), F32),
        compiler_params=pltpu.CompilerParams(dimension_semantics=("parallel",), vmem_limit_bytes=VMEM_LIMIT),
        name="merge_ln1",
    )(x2, attn, y, ga, gs, w_glu, w_ap, w_sp, w_out, ln_g, ln_b)


def _ffn_kernel(h_ref, wup_ref, cw_ref, wd_ref, g_ref, b_ref, o_ref, halo_ref, act_ref,
                *, alpha, tiles_per_seq, dff):
    i = pl.program_id(0)
    h = h_ref[...]
    hb = h.astype(BF16)
    tm = h.shape[0]
    ck = FF_CHUNK
    row8 = lax.broadcasted_iota(jnp.int32, (SUBLANES, ck), 0)

    @pl.when((i % tiles_per_seq) == 0)
    def _():
        halo_ref[...] = jnp.zeros_like(halo_ref)

    def conv(up, c0):
        cols = slice(c0, c0 + ck)
        prev = halo_ref[:, cols]
        halo_ref[:, cols] = up[tm - SUBLANES:]
        p1 = prev[SUBLANES - 1:SUBLANES]
        p2 = prev[SUBLANES - 2:SUBLANES - 1]
        r1 = pltpu.roll(up, 1, 0)
        r2 = pltpu.roll(up, 2, 0)
        d1 = jnp.concatenate([jnp.where(row8 == 0, p1, r1[:SUBLANES]), r1[SUBLANES:]], axis=0)
        d2 = jnp.concatenate([jnp.where(row8 == 0, p2, jnp.where(row8 == 1, p1, r2[:SUBLANES])), r2[SUBLANES:]],
                             axis=0)
        cw = cw_ref[:, cols]
        return cw[3:4] + cw[0:1] * d2 + cw[1:2] * d1 + cw[2:3] * up

    for j in range(dff // ck):
        c0 = j * ck
        val = conv(_dot(hb, wup_ref[:, c0:c0 + ck]), c0)
        gate = conv(_dot(hb, wup_ref[:, dff + c0:dff + c0 + ck]), dff + c0)
        act_ref[:, c0:c0 + ck] = (jax.nn.gelu(gate) * val).astype(BF16)
    ff = _dot(act_ref[...], wd_ref[...])
    o_ref[...] = _layer_norm(alpha * h + ff, g_ref[...], b_ref[...])


def _ffn(h, w_up, conv_w, conv_b, w_down, ln_g, ln_b, alpha, seq):
    n, dm = h.shape
    dff = w_down.shape[0]
    tm = ROW_TILE
    assert dff % FF_CHUNK == 0
    cw = jnp.concatenate([conv_w, conv_b[None, :]], axis=0)
    cw = jnp.pad(cw, ((0, SUBLANES - cw.shape[0]), (0, 0)))
    row = lambda i: (i, 0)
    full = lambda i: (0, 0)
    once = pl.Buffered(1)
    return pl.pallas_call(
        functools.partial(_ffn_kernel, alpha=alpha, tiles_per_seq=seq // tm, dff=dff),
        grid=(n // tm,),
        in_specs=[
            pl.BlockSpec((tm, dm), row),
            pl.BlockSpec(w_up.shape, full, pipeline_mode=once),
            pl.BlockSpec(cw.shape, full),
            pl.BlockSpec(w_down.shape, full, pipeline_mode=once),
            pl.BlockSpec((1, dm), full), pl.BlockSpec((1, dm), full),
        ],
        out_specs=pl.BlockSpec((tm, dm), row),
        out_shape=jax.ShapeDtypeStruct((n, dm), F32),
        scratch_shapes=[
            pltpu.VMEM((SUBLANES, 2 * dff), F32),
            pltpu.VMEM((tm, dff), BF16),
        ],
        compiler_params=pltpu.CompilerParams(dimension_semantics=("arbitrary",), vmem_limit_bytes=VMEM_LIMIT),
        name="conv_ffn_ln2",
    )(h, w_up, cw, w_down, ln_g, ln_b)


def kernel(x, w_in, w_attn_proj, ssm_a_re, ssm_a_im, ssm_log_dt, ssm_b_re, ssm_b_im, ssm_c_re, ssm_c_im, ssm_d,
           w_glu, w_ssm_proj, w_out, ln1_g, ln1_b, w_up, conv_w, conv_b, w_down, ln2_g, ln2_b):
    bsz, seq, dm = x.shape
    depth = w_in.shape[0]
    alpha = (2.0 * depth) ** 0.25
    n = bsz * seq
    assert seq % ROW_TILE == 0 and ROW_TILE % MOBA_BLOCK == 0 and seq % SSM_CHUNK == 0
    h = x.reshape(n, dm)
    for l in range(depth):
        q, k, vt, u, ga, gs, kmean = _in_projection(h, w_in[l].astype(BF16), seq)
        aw = q.shape[1]
        nb = seq // MOBA_BLOCK
        q3 = q.reshape(bsz, seq, aw)
        sel = _moba_select(q3, kmean.reshape(bsz, nb, aw))
        attn = _moba_attention(q3, k.reshape(bsz, seq, aw), vt.reshape(bsz, nb, *vt.shape[1:]), sel).reshape(n, aw)
        tables = _s5_tables(ssm_a_re[l], ssm_a_im[l], ssm_log_dt[l], ssm_b_re[l], ssm_b_im[l],
                            ssm_c_re[l], ssm_c_im[l], ssm_d[l])
        y = _s5_scan(u, tables, bsz, seq)
        h = _merge(h, attn, y, ga, gs, w_glu[l].astype(BF16), w_attn_proj[l].astype(BF16),
                   w_ssm_proj[l].astype(BF16), w_out[l].astype(BF16), ln1_g[l][None], ln1_b[l][None], alpha)
        h = _ffn(h, w_up[l].astype(BF16), conv_w[l], conv_b[l], w_down[l].astype(BF16),
                 ln2_g[l][None], ln2_b[l][None], alpha, seq)
    return h.reshape(bsz, seq, dm)
```

```python
import functools
import math

import jax
import jax.numpy as jnp
from jax import lax
from jax.experimental import pallas as pl
from jax.experimental.pallas import tpu as pltpu

F32 = jnp.float32
BF16 = jnp.bfloat16

N_HEADS = 8
HEAD_DIM = 64
ROT_DIM = HEAD_DIM // 4
ROPE_THETA = 500000.0
MOBA_BLOCK = 256
MOBA_TOP_K = 3
SSM_GROUP = 16
SSM_STATE = 64
CONV_WIDTH = 3
LN_EPS = 1e-5
NEG_INF = -1e30

LANES = 128
SUBLANES = 8
VMEM_LIMIT = 56 * 1024 * 1024

SSM_CHUNK = 16
ROW_TILE = 512
FF_CHUNK = 256

_NT = (((1,), (1,)), ((), ()))


def _dot(a, b):
    return jnp.dot(a, b, preferred_element_type=F32)


def _dot_nt(a, b):
    return lax.dot_general(a, b, _NT, preferred_element_type=F32)


def _layer_norm(t, g, b):
    mu = jnp.mean(t, axis=-1, keepdims=True)
    d = t - mu
    var = jnp.mean(d * d, axis=-1, keepdims=True)
    return d * lax.rsqrt(var + LN_EPS) * g + b


def _inproj_kernel(x_ref, w_ref, wvt_ref, cos_ref, s1_ref, s2_ref,
                   q_ref, k_ref, vt_ref, ga_ref, gs_ref, km_ref, *, aw, sw, dm, scale):
    xb = x_ref[...].astype(BF16)
    cos_t, s1_t, s2_t = cos_ref[...], s1_ref[...], s2_ref[...]
    tm = xb.shape[0]

    def rotary(t):
        return t * cos_t + pltpu.roll(t, ROT_DIM // 2, 1) * s1_t + pltpu.roll(t, LANES - ROT_DIM // 2, 1) * s2_t

    q = _dot(xb, w_ref[:, 0:aw])
    for j in range(aw // LANES):
        sl = slice(j * LANES, (j + 1) * LANES)
        q_ref[:, sl] = (rotary(q[:, sl]) * scale).astype(BF16)
    k = _dot(xb, w_ref[:, aw:2 * aw])
    for j in range(aw // LANES):
        sl = slice(j * LANES, (j + 1) * LANES)
        kr = rotary(k[:, sl])
        k_ref[:, sl] = kr.astype(BF16)
        for r in range(tm // MOBA_BLOCK):
            km_ref[r, :, sl] = jnp.mean(kr[r * MOBA_BLOCK:(r + 1) * MOBA_BLOCK], axis=0, keepdims=True)
    vt = _dot_nt(wvt_ref[...], xb)
    for r in range(tm // MOBA_BLOCK):
        for j in range(aw // LANES):
            vt_ref[r, j] = vt[j * LANES:(j + 1) * LANES, r * MOBA_BLOCK:(r + 1) * MOBA_BLOCK].astype(BF16)
    c0 = 3 * aw + sw
    ga_ref[...] = jax.nn.sigmoid(_dot(xb, w_ref[:, c0:c0 + dm])).astype(BF16)
    gs_ref[...] = jax.nn.sigmoid(_dot(xb, w_ref[:, c0 + dm:c0 + 2 * dm])).astype(BF16)


def _rotary_tables(seq):
    half = ROT_DIM // 2
    inv_freq = ROPE_THETA ** (-jnp.arange(0, ROT_DIM, 2, dtype=F32) / ROT_DIM)
    ang = jnp.arange(seq, dtype=F32)[:, None] * inv_freq[None, :]
    cos, sin = jnp.cos(ang), jnp.sin(ang)
    ones = jnp.ones((seq, HEAD_DIM - ROT_DIM), F32)
    zeros = jnp.zeros((seq, HEAD_DIM - ROT_DIM), F32)
    zh = jnp.zeros((seq, half), F32)
    cos_h = jnp.concatenate([cos, cos, ones], axis=1)
    s1_h = jnp.concatenate([zh, sin, zeros], axis=1)
    s2_h = jnp.concatenate([-sin, zh, zeros], axis=1)
    rep = LANES // HEAD_DIM
    return tuple(jnp.tile(t, (1, rep)) for t in (cos_h, s1_h, s2_h))


def _in_projection(x2, w_in, seq):
    n, dm = x2.shape
    aw, sw = N_HEADS * HEAD_DIM, w_in.shape[1] - 3 * N_HEADS * HEAD_DIM - 2 * dm
    tm = ROW_TILE
    cos_t, s1_t, s2_t = _rotary_tables(seq)
    w_vt = w_in[:, 2 * aw:3 * aw].T
    hp = aw // LANES
    tiles_per_seq = seq // tm
    row = lambda i: (i, 0)
    tab = lambda i: (i % tiles_per_seq, 0)
    out_shape = (
        jax.ShapeDtypeStruct((n, aw), BF16), jax.ShapeDtypeStruct((n, aw), BF16),
        jax.ShapeDtypeStruct((n // MOBA_BLOCK, hp, LANES, MOBA_BLOCK), BF16),
        jax.ShapeDtypeStruct((n, dm), BF16), jax.ShapeDtypeStruct((n, dm), BF16),
        jax.ShapeDtypeStruct((n // MOBA_BLOCK, 1, aw), F32),
    )
    return pl.pallas_call(
        functools.partial(_inproj_kernel, aw=aw, sw=sw, dm=dm, scale=HEAD_DIM ** -0.5 * math.log2(math.e)),
        grid=(n // tm,),
        in_specs=[
            pl.BlockSpec((tm, dm), row),
            pl.BlockSpec(w_in.shape, lambda i: (0, 0)),
            pl.BlockSpec(w_vt.shape, lambda i: (0, 0)),
            pl.BlockSpec((tm, LANES), tab), pl.BlockSpec((tm, LANES), tab), pl.BlockSpec((tm, LANES), tab),
        ],
        out_specs=(
            pl.BlockSpec((tm, aw), row), pl.BlockSpec((tm, aw), row),
            pl.BlockSpec((tm // MOBA_BLOCK, hp, LANES, MOBA_BLOCK), lambda i: (i, 0, 0, 0)),
            pl.BlockSpec((tm, dm), row), pl.BlockSpec((tm, dm), row),
            pl.BlockSpec((tm // MOBA_BLOCK, 1, aw), lambda i: (i, 0, 0)),
        ),
        out_shape=out_shape,
        compiler_params=pltpu.CompilerParams(dimension_semantics=("parallel",), vmem_limit_bytes=VMEM_LIMIT),
        name="in_projection",
    )(x2, w_in, w_vt, cos_t, s1_t, s2_t)


PAIR = 2 * MOBA_BLOCK
Q_TILE = PAIR
MASKED = 2.0 * NEG_INF
V_ROWS = HEAD_DIM + 16
SEL_ROWS = SUBLANES
SEL_CHUNK = 1024


def _head_split(q2):
    qf = q2.astype(F32)
    lane = lax.broadcasted_iota(jnp.int32, qf.shape, 1)
    return (jnp.where(lane < HEAD_DIM, qf, 0.0).astype(BF16), jnp.where(lane >= HEAD_DIM, qf, 0.0).astype(BF16))


def _select_kernel(q_ref, km_ref, sel_ref, *, nb):
    kmb = km_ref[0].astype(BF16)
    seq = q_ref.shape[1]
    shape = (nb, SEL_CHUNK)
    blk = lax.broadcasted_iota(jnp.int32, shape, 0)
    for c in range(seq // SEL_CHUNK):
        cols = slice(c * SEL_CHUNK, (c + 1) * SEL_CHUNK)
        own = (lax.broadcasted_iota(jnp.int32, shape, 1) + c * SEL_CHUNK) // MOBA_BLOCK
        qh = _head_split(q_ref[0, cols, :])
        for h in range(2):
            g = jnp.where(blk < own, _dot_nt(kmb, qh[h]), NEG_INF)
            rows = []
            for _ in range(min(MOBA_TOP_K, nb)):
                mx = jnp.max(g, axis=0, keepdims=True)
                first = jnp.min(jnp.where(g == mx, blk, nb), axis=0, keepdims=True)
                rows.append(jnp.where(first < own[0:1], first, -1))
                g = jnp.where(blk == first, -jnp.inf, g)
            rows.append(jnp.full((SEL_ROWS - len(rows), SEL_CHUNK), -1, jnp.int32))
            sel_ref[0, 0, h, :, cols] = jnp.concatenate(rows, axis=0)


def _moba_select(q, kmean):
    bsz, seq, aw = q.shape
    nb = seq // MOBA_BLOCK
    hp = aw // LANES
    assert seq % SEL_CHUNK == 0
    return pl.pallas_call(
        functools.partial(_select_kernel, nb=nb),
        grid=(bsz, hp),
        in_specs=[
            pl.BlockSpec((1, seq, LANES), lambda b, h: (b, 0, h)),
            pl.BlockSpec((1, nb, LANES), lambda b, h: (b, 0, h)),
        ],
        out_specs=pl.BlockSpec((1, 1, 2, SEL_ROWS, seq), lambda b, h: (b, h, 0, 0, 0)),
        out_shape=jax.ShapeDtypeStruct((bsz, hp, 2, SEL_ROWS, seq), jnp.int32),
        compiler_params=pltpu.CompilerParams(
            dimension_semantics=("parallel", "parallel"), vmem_limit_bytes=VMEM_LIMIT),
        name="moba_select",
    )(q, kmean)


def _attn_kernel(q_ref, k_ref, vt_ref, sel_ref, o_ref,
                 va_ref, vb_ref, s0_ref, s1_ref, p0_ref, p1_ref, cm_ref, al_ref, acc_ref, *, nb):
    i = pl.program_id(2)
    blk = MOBA_BLOCK
    v_refs = (va_ref, vb_ref)
    s_refs = (s0_ref, s1_ref)
    p_refs = (p0_ref, p1_ref)

    @pl.when(i == 0)
    def _():
        ones = jnp.ones((V_ROWS - HEAD_DIM, blk), BF16)

        def fill(j, c):
            for half in range(2):
                vt = vt_ref[0, 2 * j + half, 0]
                cols = slice(half * blk, (half + 1) * blk)
                for h in range(2):
                    v_refs[h][j, 0:HEAD_DIM, cols] = vt[h * HEAD_DIM:(h + 1) * HEAD_DIM]
                    v_refs[h][j, HEAD_DIM:V_ROWS, cols] = ones
            return c
        lax.fori_loop(0, nb // 2, fill, 0)

    qh = _head_split(q_ref[0])
    sel = (sel_ref[0, 0, 0], sel_ref[0, 0, 1])

    def chosen(h, n):
        sv = sel[h]
        hit = sv[0:1] == n
        for r in range(1, MOBA_TOP_K):
            hit = jnp.logical_or(hit, sv[r:r + 1] == n)
        return hit

    def past_bias(pair):
        return [tuple(jnp.where(chosen(h, 2 * pair + half), 0.0, MASKED) for half in range(2)) for h in range(2)]

    key_i = lax.broadcasted_iota(jnp.int32, (blk, blk), 0)
    qry_i = lax.broadcasted_iota(jnp.int32, (blk, blk), 1)
    causal = key_i <= qry_i

    def own_mask(s, h):
        top = jnp.concatenate([jnp.where(causal, s[:blk, :blk], MASKED),
                               jnp.where(chosen(h, 2 * i)[:, blk:], s[:blk, blk:], MASKED)], axis=1)
        bot = jnp.concatenate([jnp.full((blk, blk), MASKED, F32),
                               jnp.where(causal, s[blk:, blk:], MASKED)], axis=1)
        return jnp.concatenate([top, bot], axis=0)

    def stage_a(pair, slot, own):
        kp = k_ref[0, pl.ds(pl.multiple_of(pair * PAIR, PAIR), PAIR), :]
        for h in range(2):
            s = _dot_nt(kp, qh[h])
            if own:
                s = own_mask(s, h)
            s_refs[slot][h] = s
            cm_ref[slot, 2 * h:2 * h + 1, :] = jnp.max(s[:blk], axis=0, keepdims=True)
            cm_ref[slot, 2 * h + 1:2 * h + 2, :] = jnp.max(s[blk:], axis=0, keepdims=True)

    def stage_b(slot, m, bias):
        m_out = []
        for h in range(2):
            cm0 = cm_ref[slot, 2 * h:2 * h + 1, :]
            cm1 = cm_ref[slot, 2 * h + 1:2 * h + 2, :]
            if bias is None:
                m_new = jnp.maximum(m[h], jnp.maximum(cm0, cm1))
                c0 = c1 = m_new
            else:
                b0, b1 = bias[h]
                m_new = jnp.maximum(m[h], jnp.maximum(cm0 + b0, cm1 + b1))
                c0, c1 = m_new - b0, m_new - b1
            al_ref[slot, h:h + 1, :] = jnp.exp2(m[h] - m_new)
            p_refs[slot][h, 0:blk] = jnp.exp2(s_refs[slot][h, 0:blk] - c0).astype(BF16)
            p_refs[slot][h, blk:PAIR] = jnp.exp2(s_refs[slot][h, blk:PAIR] - c1).astype(BF16)
            m_out.append(m_new)
        return tuple(m_out)

    def stage_c(pair, slot):
        for h in range(2):
            pv = _dot(v_refs[h][pair], p_refs[slot][h])
            acc_ref[h] = acc_ref[h] * al_ref[slot, h:h + 1, :] + pv

    def trip(t, slot, m):
        stage_a(t - 1, slot, False)
        m = stage_b(1 - slot, m, past_bias(t - 2))
        stage_c(jnp.where(t == 2, i, t - 3), slot)
        return m

    acc_ref[...] = jnp.zeros_like(acc_ref)
    floor = jnp.full((1, Q_TILE), NEG_INF, F32)
    stage_a(i, 0, True)
    stage_a(0, 1, False)
    m0 = stage_b(0, (floor, floor), None)
    n_list = i + 1

    @pl.when(i == 0)
    def _():
        stage_c(i, 0)

    @pl.when(i > 0)
    def _():
        def double_trip(d, m):
            t = 2 + 2 * d
            return trip(t + 1, 1, trip(t, 0, m))

        m1 = lax.fori_loop(0, (n_list - 2) // 2, double_trip, m0)

        @pl.when(n_list % 2 == 1)
        def _():
            m2 = trip(n_list - 1, 0, m1)
            stage_b(0, m2, past_bias(n_list - 2))
            stage_c(n_list - 3, 1)
            stage_c(n_list - 2, 0)

        @pl.when(n_list % 2 == 0)
        def _():
            stage_b(1, m1, past_bias(n_list - 2))
            stage_c(jnp.where(n_list == 2, i, n_list - 3), 0)
            stage_c(n_list - 2, 1)

    outs = []
    for h in range(2):
        acc = acc_ref[h]
        outs.append(acc[:HEAD_DIM] / acc[HEAD_DIM:HEAD_DIM + 1])
    o_ref[0] = jnp.concatenate(outs, axis=0).T.astype(o_ref.dtype)


def _moba_attention(q, k, vt, sel):
    bsz, seq, aw = q.shape
    nb = seq // MOBA_BLOCK
    hp = aw // LANES
    assert nb % 2 == 0
    return pl.pallas_call(
        functools.partial(_attn_kernel, nb=nb),
        grid=(bsz, hp, nb // 2),
        in_specs=[
            pl.BlockSpec((1, Q_TILE, LANES), lambda b, h, i: (b, i, h)),
            pl.BlockSpec((1, seq, LANES), lambda b, h, i: (b, 0, h)),
            pl.BlockSpec((1, nb, 1, LANES, MOBA_BLOCK), lambda b, h, i: (b, 0, h, 0, 0)),
            pl.BlockSpec((1, 1, 2, SEL_ROWS, Q_TILE), lambda b, h, i: (b, h, 0, 0, i)),
        ],
        out_specs=pl.BlockSpec((1, Q_TILE, LANES), lambda b, h, i: (b, i, h)),
        out_shape=jax.ShapeDtypeStruct((bsz, seq, aw), BF16),
        scratch_shapes=[
            pltpu.VMEM((nb // 2, V_ROWS, PAIR), BF16), pltpu.VMEM((nb // 2, V_ROWS, PAIR), BF16),
            pltpu.VMEM((2, PAIR, Q_TILE), F32), pltpu.VMEM((2, PAIR, Q_TILE), F32),
            pltpu.VMEM((2, PAIR, Q_TILE), BF16), pltpu.VMEM((2, PAIR, Q_TILE), BF16),
            pltpu.VMEM((2, SUBLANES, Q_TILE), F32),
            pltpu.VMEM((2, SUBLANES, Q_TILE), F32),
            pltpu.VMEM((2, V_ROWS, Q_TILE), F32),
        ],
        compiler_params=pltpu.CompilerParams(
            dimension_semantics=("parallel", "parallel", "arbitrary"), vmem_limit_bytes=VMEM_LIMIT),
        name="moba_attention",
    )(q, k, vt, sel)


def _s5_tables(a_re, a_im, log_dt, b_re, b_im, c_re, c_im, d_skip):
    hi = lax.Precision.HIGHEST
    L = SSM_CHUNK
    g, p = a_re.shape
    dt = jnp.exp(log_dt)[:, None]
    mag = jnp.exp(a_re * dt)
    lb_re, lb_im = mag * jnp.cos(a_im * dt), mag * jnp.sin(a_im * dt)
    den = a_re * a_re + a_im * a_im
    n_re, n_im = lb_re - 1.0, lb_im
    z_re = (n_re * a_re + n_im * a_im) / den
    z_im = (n_im * a_re - n_re * a_im) / den
    j = jnp.arange(L + 1, dtype=F32)[None, :, None]
    pmag = jnp.exp(j * (a_re * dt)[:, None, :])
    ang = j * (a_im * dt)[:, None, :]
    pr, pi = pmag * jnp.cos(ang), pmag * jnp.sin(ang)
    zb_re = z_re[:, :, None] * b_re - z_im[:, :, None] * b_im
    zb_im = z_re[:, :, None] * b_im + z_im[:, :, None] * b_re
    cl_re = c_re[:, None] * pr[:, :, None, :] - c_im[:, None] * pi[:, :, None, :]
    cl_im = c_re[:, None] * pi[:, :, None, :] + c_im[:, None] * pr[:, :, None, :]
    kern = (jnp.einsum('gjop,gpi->gjoi', cl_re[:, :L], zb_re, precision=hi)
            - jnp.einsum('gjop,gpi->gjoi', cl_im[:, :L], zb_im, precision=hi))
    kern = kern.at[:, 0].add(jax.vmap(jnp.diag)(d_skip))
    s_i = jnp.arange(L)[:, None]
    t_i = jnp.arange(L)[None, :]
    lag = t_i - s_i
    toep = jnp.where((lag >= 0)[None, :, :, None, None], kern[:, jnp.maximum(lag, 0)], 0.0)
    toep = toep.transpose(0, 1, 4, 2, 3).reshape(g, L * SSM_GROUP, L * SSM_GROUP)
    rev = L - 1 - jnp.arange(L)
    w_re = pr[:, rev][:, :, :, None] * zb_re[:, None] - pi[:, rev][:, :, :, None] * zb_im[:, None]
    w_im = pr[:, rev][:, :, :, None] * zb_im[:, None] + pi[:, rev][:, :, :, None] * zb_re[:, None]
    w_re = w_re.transpose(0, 1, 3, 2).reshape(g, L * SSM_GROUP, p)
    w_im = w_im.transpose(0, 1, 3, 2).reshape(g, L * SSM_GROUP, p)
    b_pow = jnp.concatenate([w_re, w_im], axis=-1)
    b_pow_sw = jnp.concatenate([w_im, w_re], axis=-1)
    cp_re = cl_re[:, 1:].transpose(0, 3, 1, 2).reshape(g, p, L * SSM_GROUP)
    cp_im = cl_im[:, 1:].transpose(0, 3, 1, 2).reshape(g, p, L * SSM_GROUP)
    c_pow = jnp.concatenate([cp_re, -cp_im], axis=1)
    lr, li = pr[:, L], pi[:, L]
    lam = jnp.stack([jnp.concatenate([lr, lr], -1), jnp.concatenate([-li, li], -1)], axis=1)
    tr = lambda t: t.transpose(0, 2, 1).astype(BF16)
    return tr(toep), tr(b_pow), tr(b_pow_sw), tr(c_pow), lam


def _s5_input_kernel(x_ref, wut_ref, ut_ref):
    ut = _dot_nt(wut_ref[...], x_ref[0].astype(BF16))
    ut_ref[...] = ut.reshape(ut_ref.shape).astype(ut_ref.dtype)


def _s5_input(x, w_ut, n_groups):
    bsz, seq, dm = x.shape
    L = SSM_CHUNK
    nc = seq // L
    assert nc % LANES == 0
    xv = x.reshape(bsz, nc, L * dm)
    return pl.pallas_call(
        _s5_input_kernel,
        grid=(bsz, L),
        in_specs=[
            pl.BlockSpec((1, nc, dm), lambda b, s: (b, 0, s)),
            pl.BlockSpec(w_ut.shape, lambda b, s: (0, 0)),
        ],
        out_specs=pl.BlockSpec((n_groups, SSM_GROUP, nc), lambda b, s: (0, s, b)),
        out_shape=jax.ShapeDtypeStruct((n_groups, L * SSM_GROUP, bsz * nc), BF16),
        compiler_params=pltpu.CompilerParams(
            dimension_semantics=("parallel", "parallel"), vmem_limit_bytes=VMEM_LIMIT),
        name="s5_input",
    )(xv, w_ut)


def _s5_kernel(ut_ref, toept_ref, bpt_ref, bpst_ref, cpt_ref, lam_ref, yt_ref, s_ref, ss_ref, hp_ref, *, bsz, nc):
    ut = ut_ref[0]
    s_ref[...] = _dot(bpt_ref[0], ut).T
    ss_ref[...] = _dot(bpst_ref[0], ut).T
    a = lam_ref[0, 0:1, :]
    bv = lam_ref[0, 1:2, :]

    def body(c, carry):
        h, hs = carry
        rows = pl.ds(c, bsz, stride=nc)
        hp_ref[rows, :] = h
        return a * h + bv * hs + s_ref[rows, :], a * hs - bv * h + ss_ref[rows, :]

    zero = jnp.zeros((bsz, 2 * SSM_STATE), F32)
    lax.fori_loop(0, nc, body, (zero, zero), unroll=8)
    yt_ref[0] = _dot(toept_ref[0], ut) + _dot_nt(cpt_ref[0], hp_ref[...].astype(BF16))


def _s5_scan(ut, tables, bsz, seq):
    toept, bpt, bpst, cpt, lam = tables
    g, w, cols = ut.shape
    nc = seq // SSM_CHUNK
    p2 = 2 * SSM_STATE
    grp = lambda i: (i, 0, 0)
    return pl.pallas_call(
        functools.partial(_s5_kernel, bsz=bsz, nc=nc),
        grid=(g,),
        in_specs=[
            pl.BlockSpec((1, w, cols), grp), pl.BlockSpec((1, w, w), grp),
            pl.BlockSpec((1, p2, w), grp), pl.BlockSpec((1, p2, w), grp),
            pl.BlockSpec((1, w, p2), grp), pl.BlockSpec((1, 2, p2), grp),
        ],
        out_specs=pl.BlockSpec((1, w, cols), grp),
        out_shape=jax.ShapeDtypeStruct((g, w, cols), F32),
        scratch_shapes=[pltpu.VMEM((cols, p2), F32), pltpu.VMEM((cols, p2), F32), pltpu.VMEM((cols, p2), F32)],
        compiler_params=pltpu.CompilerParams(dimension_semantics=("parallel",), vmem_limit_bytes=VMEM_LIMIT),
        name="s5_scan",
    )(ut, toept, bpt, bpst, cpt, lam)


def _merge_kernel(x_ref, attn_ref, yt_ref, ga_ref, gs_ref, wglut_ref, wap_ref, wsp_ref, wout_ref, g_ref, b_ref,
                  h_ref, *, alpha):
    gh, nc = yt_ref.shape[0] * yt_ref.shape[1], yt_ref.shape[2]
    gt = jax.nn.gelu(yt_ref[...].reshape(gh, nc))
    ssm = (gt * jax.nn.sigmoid(_dot(wglut_ref[...], gt.astype(BF16)))).T
    merged = (ga_ref[0].astype(F32) * _dot(attn_ref[0], wap_ref[...])
              + gs_ref[0].astype(F32) * _dot(ssm.astype(BF16), wsp_ref[...]))
    mix = _dot(merged.astype(BF16), wout_ref[...])
    h_ref[0] = _layer_norm(alpha * x_ref[0] + mix, g_ref[...], b_ref[...])


def _merge(x, attn, yt, ga, gs, w_glu_t, w_ap, w_sp, w_out, ln_g, ln_b, alpha):
    bsz, seq, dm = x.shape
    aw = attn.shape[1]
    g = yt.shape[0]
    L = SSM_CHUNK
    nc = seq // L
    step = lambda b, s: (b, 0, s)
    full = lambda b, s: (0, 0)
    view = lambda t: t.reshape(bsz, nc, L * t.shape[-1])
    out = pl.pallas_call(
        functools.partial(_merge_kernel, alpha=alpha),
        grid=(bsz, L),
        in_specs=[
            pl.BlockSpec((1, nc, dm), step), pl.BlockSpec((1, nc, aw), step),
            pl.BlockSpec((g, SSM_GROUP, nc), lambda b, s: (0, s, b)),
            pl.BlockSpec((1, nc, dm), step), pl.BlockSpec((1, nc, dm), step),
            pl.BlockSpec(w_glu_t.shape, full), pl.BlockSpec(w_ap.shape, full), pl.BlockSpec(w_sp.shape, full),
            pl.BlockSpec(w_out.shape, full), pl.BlockSpec((1, dm), full), pl.BlockSpec((1, dm), full),
        ],
        out_specs=pl.BlockSpec((1, nc, dm), step),
        out_shape=jax.ShapeDtypeStruct((bsz, nc, L * dm), F32),
        compiler_params=pltpu.CompilerParams(
            dimension_semantics=("parallel", "parallel"), vmem_limit_bytes=VMEM_LIMIT),
        name="merge_ln1",
    )(view(x), view(attn), yt, view(ga), view(gs), w_glu_t, w_ap, w_sp, w_out, ln_g, ln_b)
    return out.reshape(bsz, seq, dm)


def _ffn_kernel(h_ref, wup_ref, cw_ref, wd_ref, g_ref, b_ref, o_ref, halo_ref, act_ref,
                *, alpha, tiles_per_seq, dff):
    i = pl.program_id(0)
    h = h_ref[...]
    hb = h.astype(BF16)
    tm = h.shape[0]
    ck = FF_CHUNK
    row8 = lax.broadcasted_iota(jnp.int32, (SUBLANES, ck), 0)

    @pl.when((i % tiles_per_seq) == 0)
    def _():
        halo_ref[...] = jnp.zeros_like(halo_ref)

    def conv(up, c0):
        cols = slice(c0, c0 + ck)
        prev = halo_ref[:, cols]
        halo_ref[:, cols] = up[tm - SUBLANES:]
        p1 = prev[SUBLANES - 1:SUBLANES]
        p2 = prev[SUBLANES - 2:SUBLANES - 1]
        r1 = pltpu.roll(up, 1, 0)
        r2 = pltpu.roll(up, 2, 0)
        d1 = jnp.concatenate([jnp.where(row8 == 0, p1, r1[:SUBLANES]), r1[SUBLANES:]], axis=0)
        d2 = jnp.concatenate([jnp.where(row8 == 0, p2, jnp.where(row8 == 1, p1, r2[:SUBLANES])), r2[SUBLANES:]],
                             axis=0)
        cw = cw_ref[:, cols]
        return cw[3:4] + cw[0:1] * d2 + cw[1:2] * d1 + cw[2:3] * up

    for j in range(dff // ck):
        c0 = j * ck
        val = conv(_dot(hb, wup_ref[:, c0:c0 + ck]), c0)
        gate = conv(_dot(hb, wup_ref[:, dff + c0:dff + c0 + ck]), dff + c0)
        act_ref[:, c0:c0 + ck] = (jax.nn.gelu(gate) * val).astype(BF16)
    ff = _dot(act_ref[...], wd_ref[...])
    o_ref[...] = _layer_norm(alpha * h + ff, g_ref[...], b_ref[...])


def _ffn(h, w_up, conv_w, conv_b, w_down, ln_g, ln_b, alpha, seq):
    n, dm = h.shape
    dff = w_down.shape[0]
    tm = ROW_TILE
    assert dff % FF_CHUNK == 0
    cw = jnp.concatenate([conv_w, conv_b[None, :]], axis=0)
    cw = jnp.pad(cw, ((0, SUBLANES - cw.shape[0]), (0, 0)))
    row = lambda i: (i, 0)
    full = lambda i: (0, 0)
    once = pl.Buffered(1)
    return pl.pallas_call(
        functools.partial(_ffn_kernel, alpha=alpha, tiles_per_seq=seq // tm, dff=dff),
        grid=(n // tm,),
        in_specs=[
            pl.BlockSpec((tm, dm), row),
            pl.BlockSpec(w_up.shape, full, pipeline_mode=once),
            pl.BlockSpec(cw.shape, full),
            pl.BlockSpec(w_down.shape, full, pipeline_mode=once),
            pl.BlockSpec((1, dm), full), pl.BlockSpec((1, dm), full),
        ],
        out_specs=pl.BlockSpec((tm, dm), row),
        out_shape=jax.ShapeDtypeStruct((n, dm), F32),
        scratch_shapes=[
            pltpu.VMEM((SUBLANES, 2 * dff), F32),
            pltpu.VMEM((tm, dff), BF16),
        ],
        compiler_params=pltpu.CompilerParams(dimension_semantics=("arbitrary",), vmem_limit_bytes=VMEM_LIMIT),
        name="conv_ffn_ln2",
    )(h, w_up, cw, w_down, ln_g, ln_b)


def kernel(x, w_in, w_attn_proj, ssm_a_re, ssm_a_im, ssm_log_dt, ssm_b_re, ssm_b_im, ssm_c_re, ssm_c_im, ssm_d,
           w_glu, w_ssm_proj, w_out, ln1_g, ln1_b, w_up, conv_w, conv_b, w_down, ln2_g, ln2_b):
    bsz, seq, dm = x.shape
    depth = w_in.shape[0]
    alpha = (2.0 * depth) ** 0.25
    n = bsz * seq
    assert seq % ROW_TILE == 0 and ROW_TILE % MOBA_BLOCK == 0 and seq % SSM_CHUNK == 0
    h = x
    for l in range(depth):
        w_in_b = w_in[l].astype(BF16)
        q, k, vt, ga, gs, kmean = _in_projection(h.reshape(n, dm), w_in_b, seq)
        aw = q.shape[1]
        nb = seq // MOBA_BLOCK
        q3 = q.reshape(bsz, seq, aw)
        sel = _moba_select(q3, kmean.reshape(bsz, nb, aw))
        attn = _moba_attention(q3, k.reshape(bsz, seq, aw), vt.reshape(bsz, nb, *vt.shape[1:]), sel).reshape(n, aw)
        tables = _s5_tables(ssm_a_re[l], ssm_a_im[l], ssm_log_dt[l], ssm_b_re[l], ssm_b_im[l],
                            ssm_c_re[l], ssm_c_im[l], ssm_d[l])
        n_groups, ssm_w = tables[0].shape[0], tables[0].shape[0] * SSM_GROUP
        ut = _s5_input(h, w_in_b[:, 3 * aw:3 * aw + ssm_w].T, n_groups)
        yt = _s5_scan(ut, tables, bsz, seq)
        h = _merge(h, attn, yt, ga, gs, w_glu[l].T.astype(BF16), w_attn_proj[l].astype(BF16),
                   w_ssm_proj[l].astype(BF16), w_out[l].astype(BF16), ln1_g[l][None], ln1_b[l][None], alpha)
        h = _ffn(h.reshape(n, dm), w_up[l].astype(BF16), conv_w[l], conv_b[l], w_down[l].astype(BF16),
                 ln2_g[l][None], ln2_b[l][None], alpha, seq).reshape(bsz, seq, dm)
    return h
```

```python
import functools
import math

import jax
import jax.numpy as jnp
from jax import lax
from jax.experimental import pallas as pl
from jax.experimental.pallas import tpu as pltpu

F32 = jnp.float32
BF16 = jnp.bfloat16

N_HEADS = 8
HEAD_DIM = 64
ROT_DIM = HEAD_DIM // 4
ROPE_THETA = 500000.0
MOBA_BLOCK = 256
MOBA_TOP_K = 3
SSM_GROUP = 16
SSM_STATE = 64
CONV_WIDTH = 3
LN_EPS = 1e-5
NEG_INF = -1e30

LANES = 128
SUBLANES = 8
VMEM_LIMIT = 56 * 1024 * 1024

SSM_CHUNK = 16
ROW_TILE = 512
FF_CHUNK = 256

_NT = (((1,), (1,)), ((), ()))


def _dot(a, b):
    return jnp.dot(a, b, preferred_element_type=F32)


def _dot_nt(a, b):
    return lax.dot_general(a, b, _NT, preferred_element_type=F32)


def _layer_norm(t, g, b):
    mu = jnp.mean(t, axis=-1, keepdims=True)
    d = t - mu
    var = jnp.mean(d * d, axis=-1, keepdims=True)
    return d * lax.rsqrt(var + LN_EPS) * g + b


def _to_step_layout(dst_ref, src, slab_ref):
    chunks, width = dst_ref.shape[0], src.shape[1]
    for j in range(width // LANES):
        slab_ref[j] = src[:, j * LANES:(j + 1) * LANES]
    for s in range(SSM_CHUNK):
        for j in range(width // LANES):
            c0 = s * width + j * LANES
            dst_ref[:, c0:c0 + LANES] = slab_ref[j, pl.ds(s, chunks, stride=SSM_CHUNK), :].astype(dst_ref.dtype)


def _from_step_layout(src_ref, slab_ref):
    chunks, width = src_ref.shape[0], slab_ref.shape[0] * LANES
    for s in range(SSM_CHUNK):
        for j in range(width // LANES):
            c0 = s * width + j * LANES
            slab_ref[j, pl.ds(s, chunks, stride=SSM_CHUNK), :] = src_ref[:, c0:c0 + LANES]
    return jnp.concatenate([slab_ref[j] for j in range(width // LANES)], axis=1)


def _inproj_kernel(x_ref, w_ref, wvt_ref, cos_ref, s1_ref, s2_ref,
                   q_ref, k_ref, vt_ref, ga_ref, gs_ref, km_ref, xs_ref, tmp_ref, *, aw, sw, dm, scale):
    xb = x_ref[...].astype(BF16)
    cos_t, s1_t, s2_t = cos_ref[...], s1_ref[...], s2_ref[...]
    tm = xb.shape[0]

    def rotary(t):
        return t * cos_t + pltpu.roll(t, ROT_DIM // 2, 1) * s1_t + pltpu.roll(t, LANES - ROT_DIM // 2, 1) * s2_t

    q = _dot(xb, w_ref[:, 0:aw])
    for j in range(aw // LANES):
        sl = slice(j * LANES, (j + 1) * LANES)
        q_ref[:, sl] = (rotary(q[:, sl]) * scale).astype(BF16)
    k = _dot(xb, w_ref[:, aw:2 * aw])
    for j in range(aw // LANES):
        sl = slice(j * LANES, (j + 1) * LANES)
        kr = rotary(k[:, sl])
        k_ref[:, sl] = kr.astype(BF16)
        for r in range(tm // MOBA_BLOCK):
            km_ref[r, :, sl] = jnp.mean(kr[r * MOBA_BLOCK:(r + 1) * MOBA_BLOCK], axis=0, keepdims=True)
    vt = _dot_nt(wvt_ref[...], xb)
    for r in range(tm // MOBA_BLOCK):
        for j in range(aw // LANES):
            vt_ref[r, j] = vt[j * LANES:(j + 1) * LANES, r * MOBA_BLOCK:(r + 1) * MOBA_BLOCK].astype(BF16)
    c0 = 3 * aw + sw
    _to_step_layout(ga_ref, jax.nn.sigmoid(_dot(xb, w_ref[:, c0:c0 + dm])), tmp_ref)
    _to_step_layout(gs_ref, jax.nn.sigmoid(_dot(xb, w_ref[:, c0 + dm:c0 + 2 * dm])), tmp_ref)
    _to_step_layout(xs_ref, x_ref[...], tmp_ref)


def _rotary_tables(seq):
    half = ROT_DIM // 2
    inv_freq = ROPE_THETA ** (-jnp.arange(0, ROT_DIM, 2, dtype=F32) / ROT_DIM)
    ang = jnp.arange(seq, dtype=F32)[:, None] * inv_freq[None, :]
    cos, sin = jnp.cos(ang), jnp.sin(ang)
    ones = jnp.ones((seq, HEAD_DIM - ROT_DIM), F32)
    zeros = jnp.zeros((seq, HEAD_DIM - ROT_DIM), F32)
    zh = jnp.zeros((seq, half), F32)
    cos_h = jnp.concatenate([cos, cos, ones], axis=1)
    s1_h = jnp.concatenate([zh, sin, zeros], axis=1)
    s2_h = jnp.concatenate([-sin, zh, zeros], axis=1)
    rep = LANES // HEAD_DIM
    return tuple(jnp.tile(t, (1, rep)) for t in (cos_h, s1_h, s2_h))


def _in_projection(x2, w_in, seq):
    n, dm = x2.shape
    aw, sw = N_HEADS * HEAD_DIM, w_in.shape[1] - 3 * N_HEADS * HEAD_DIM - 2 * dm
    tm = ROW_TILE
    cos_t, s1_t, s2_t = _rotary_tables(seq)
    w_vt = w_in[:, 2 * aw:3 * aw].T
    hp = aw // LANES
    tiles_per_seq = seq // tm
    row = lambda i: (i, 0)
    tab = lambda i: (i % tiles_per_seq, 0)
    L = SSM_CHUNK
    step_rows = lambda i: (i, 0)
    out_shape = (
        jax.ShapeDtypeStruct((n, aw), BF16), jax.ShapeDtypeStruct((n, aw), BF16),
        jax.ShapeDtypeStruct((n // MOBA_BLOCK, hp, LANES, MOBA_BLOCK), BF16),
        jax.ShapeDtypeStruct((n // L, L * dm), BF16), jax.ShapeDtypeStruct((n // L, L * dm), BF16),
        jax.ShapeDtypeStruct((n // MOBA_BLOCK, 1, aw), F32),
        jax.ShapeDtypeStruct((n // L, L * dm), F32),
    )
    return pl.pallas_call(
        functools.partial(_inproj_kernel, aw=aw, sw=sw, dm=dm, scale=HEAD_DIM ** -0.5 * math.log2(math.e)),
        grid=(n // tm,),
        in_specs=[
            pl.BlockSpec((tm, dm), row),
            pl.BlockSpec(w_in.shape, lambda i: (0, 0)),
            pl.BlockSpec(w_vt.shape, lambda i: (0, 0)),
            pl.BlockSpec((tm, LANES), tab), pl.BlockSpec((tm, LANES), tab), pl.BlockSpec((tm, LANES), tab),
        ],
        out_specs=(
            pl.BlockSpec((tm, aw), row), pl.BlockSpec((tm, aw), row),
            pl.BlockSpec((tm // MOBA_BLOCK, hp, LANES, MOBA_BLOCK), lambda i: (i, 0, 0, 0)),
            pl.BlockSpec((tm // L, L * dm), step_rows), pl.BlockSpec((tm // L, L * dm), step_rows),
            pl.BlockSpec((tm // MOBA_BLOCK, 1, aw), lambda i: (i, 0, 0)),
            pl.BlockSpec((tm // L, L * dm), step_rows),
        ),
        out_shape=out_shape,
        scratch_shapes=[pltpu.VMEM((dm // LANES, tm, LANES), F32)],
        compiler_params=pltpu.CompilerParams(dimension_semantics=("parallel",), vmem_limit_bytes=VMEM_LIMIT),
        name="in_projection",
    )(x2, w_in, w_vt, cos_t, s1_t, s2_t)


PAIR = 2 * MOBA_BLOCK
Q_TILE = PAIR
MASKED = 2.0 * NEG_INF
V_ROWS = HEAD_DIM + 16
SEL_ROWS = SUBLANES
SEL_CHUNK = 1024


def _head_split(q2):
    qf = q2.astype(F32)
    lane = lax.broadcasted_iota(jnp.int32, qf.shape, 1)
    return (jnp.where(lane < HEAD_DIM, qf, 0.0).astype(BF16), jnp.where(lane >= HEAD_DIM, qf, 0.0).astype(BF16))


def _select_kernel(q_ref, km_ref, sel_ref, *, nb):
    kmb = km_ref[0].astype(BF16)
    seq = q_ref.shape[1]
    shape = (nb, SEL_CHUNK)
    blk = lax.broadcasted_iota(jnp.int32, shape, 0)
    for c in range(seq // SEL_CHUNK):
        cols = slice(c * SEL_CHUNK, (c + 1) * SEL_CHUNK)
        own = (lax.broadcasted_iota(jnp.int32, shape, 1) + c * SEL_CHUNK) // MOBA_BLOCK
        qh = _head_split(q_ref[0, cols, :])
        for h in range(2):
            g = jnp.where(blk < own, _dot_nt(kmb, qh[h]), NEG_INF)
            rows = []
            for _ in range(min(MOBA_TOP_K, nb)):
                mx = jnp.max(g, axis=0, keepdims=True)
                first = jnp.min(jnp.where(g == mx, blk, nb), axis=0, keepdims=True)
                rows.append(jnp.where(first < own[0:1], first, -1))
                g = jnp.where(blk == first, -jnp.inf, g)
            rows.append(jnp.full((SEL_ROWS - len(rows), SEL_CHUNK), -1, jnp.int32))
            sel_ref[0, 0, h, :, cols] = jnp.concatenate(rows, axis=0)


def _moba_select(q, kmean):
    bsz, seq, aw = q.shape
    nb = seq // MOBA_BLOCK
    hp = aw // LANES
    assert seq % SEL_CHUNK == 0
    return pl.pallas_call(
        functools.partial(_select_kernel, nb=nb),
        grid=(bsz, hp),
        in_specs=[
            pl.BlockSpec((1, seq, LANES), lambda b, h: (b, 0, h)),
            pl.BlockSpec((1, nb, LANES), lambda b, h: (b, 0, h)),
        ],
        out_specs=pl.BlockSpec((1, 1, 2, SEL_ROWS, seq), lambda b, h: (b, h, 0, 0, 0)),
        out_shape=jax.ShapeDtypeStruct((bsz, hp, 2, SEL_ROWS, seq), jnp.int32),
        compiler_params=pltpu.CompilerParams(
            dimension_semantics=("parallel", "parallel"), vmem_limit_bytes=VMEM_LIMIT),
        name="moba_select",
    )(q, kmean)


def _attn_kernel(q_ref, k_ref, vt_ref, sel_ref, o_ref,
                 va_ref, vb_ref, s0_ref, s1_ref, p0_ref, p1_ref, cm_ref, al_ref, acc_ref, onat_ref, *, nb):
    i = pl.program_id(2)
    blk = MOBA_BLOCK
    v_refs = (va_ref, vb_ref)
    s_refs = (s0_ref, s1_ref)
    p_refs = (p0_ref, p1_ref)

    @pl.when(i == 0)
    def _():
        ones = jnp.ones((V_ROWS - HEAD_DIM, blk), BF16)

        def fill(j, c):
            for half in range(2):
                vt = vt_ref[0, 2 * j + half, 0]
                cols = slice(half * blk, (half + 1) * blk)
                for h in range(2):
                    v_refs[h][j, 0:HEAD_DIM, cols] = vt[h * HEAD_DIM:(h + 1) * HEAD_DIM]
                    v_refs[h][j, HEAD_DIM:V_ROWS, cols] = ones
            return c
        lax.fori_loop(0, nb // 2, fill, 0)

    qh = _head_split(q_ref[0])
    sel = (sel_ref[0, 0, 0], sel_ref[0, 0, 1])

    def chosen(h, n):
        sv = sel[h]
        hit = sv[0:1] == n
        for r in range(1, MOBA_TOP_K):
            hit = jnp.logical_or(hit, sv[r:r + 1] == n)
        return hit

    def past_bias(pair):
        return [tuple(jnp.where(chosen(h, 2 * pair + half), 0.0, MASKED) for half in range(2)) for h in range(2)]

    key_i = lax.broadcasted_iota(jnp.int32, (blk, blk), 0)
    qry_i = lax.broadcasted_iota(jnp.int32, (blk, blk), 1)
    causal = key_i <= qry_i

    def own_mask(s, h):
        top = jnp.concatenate([jnp.where(causal, s[:blk, :blk], MASKED),
                               jnp.where(chosen(h, 2 * i)[:, blk:], s[:blk, blk:], MASKED)], axis=1)
        bot = jnp.concatenate([jnp.full((blk, blk), MASKED, F32),
                               jnp.where(causal, s[blk:, blk:], MASKED)], axis=1)
        return jnp.concatenate([top, bot], axis=0)

    def stage_a(pair, slot, own):
        kp = k_ref[0, pl.ds(pl.multiple_of(pair * PAIR, PAIR), PAIR), :]
        for h in range(2):
            s = _dot_nt(kp, qh[h])
            if own:
                s = own_mask(s, h)
            s_refs[slot][h] = s
            cm_ref[slot, 2 * h:2 * h + 1, :] = jnp.max(s[:blk], axis=0, keepdims=True)
            cm_ref[slot, 2 * h + 1:2 * h + 2, :] = jnp.max(s[blk:], axis=0, keepdims=True)

    def stage_b(slot, m, bias):
        m_out = []
        for h in range(2):
            cm0 = cm_ref[slot, 2 * h:2 * h + 1, :]
            cm1 = cm_ref[slot, 2 * h + 1:2 * h + 2, :]
            if bias is None:
                m_new = jnp.maximum(m[h], jnp.maximum(cm0, cm1))
                c0 = c1 = m_new
            else:
                b0, b1 = bias[h]
                m_new = jnp.maximum(m[h], jnp.maximum(cm0 + b0, cm1 + b1))
                c0, c1 = m_new - b0, m_new - b1
            al_ref[slot, h:h + 1, :] = jnp.exp2(m[h] - m_new)
            p_refs[slot][h, 0:blk] = jnp.exp2(s_refs[slot][h, 0:blk] - c0).astype(BF16)
            p_refs[slot][h, blk:PAIR] = jnp.exp2(s_refs[slot][h, blk:PAIR] - c1).astype(BF16)
            m_out.append(m_new)
        return tuple(m_out)

    def stage_c(pair, slot):
        for h in range(2):
            pv = _dot(v_refs[h][pair], p_refs[slot][h])
            acc_ref[h] = acc_ref[h] * al_ref[slot, h:h + 1, :] + pv

    def trip(t, slot, m):
        stage_a(t - 1, slot, False)
        m = stage_b(1 - slot, m, past_bias(t - 2))
        stage_c(jnp.where(t == 2, i, t - 3), slot)
        return m

    acc_ref[...] = jnp.zeros_like(acc_ref)
    floor = jnp.full((1, Q_TILE), NEG_INF, F32)
    stage_a(i, 0, True)
    stage_a(0, 1, False)
    m0 = stage_b(0, (floor, floor), None)
    n_list = i + 1

    @pl.when(i == 0)
    def _():
        stage_c(i, 0)

    @pl.when(i > 0)
    def _():
        def double_trip(d, m):
            t = 2 + 2 * d
            return trip(t + 1, 1, trip(t, 0, m))

        m1 = lax.fori_loop(0, (n_list - 2) // 2, double_trip, m0)

        @pl.when(n_list % 2 == 1)
        def _():
            m2 = trip(n_list - 1, 0, m1)
            stage_b(0, m2, past_bias(n_list - 2))
            stage_c(n_list - 3, 1)
            stage_c(n_list - 2, 0)

        @pl.when(n_list % 2 == 0)
        def _():
            stage_b(1, m1, past_bias(n_list - 2))
            stage_c(jnp.where(n_list == 2, i, n_list - 3), 0)
            stage_c(n_list - 2, 1)

    outs = []
    for h in range(2):
        acc = acc_ref[h]
        outs.append(acc[:HEAD_DIM] / acc[HEAD_DIM:HEAD_DIM + 1])
    o_nat = jnp.concatenate(outs, axis=0).T
    _to_step_layout(o_ref.at[0], o_nat, onat_ref)


def _moba_attention(q, k, vt, sel):
    bsz, seq, aw = q.shape
    nb = seq // MOBA_BLOCK
    hp = aw // LANES
    assert nb % 2 == 0
    return pl.pallas_call(
        functools.partial(_attn_kernel, nb=nb),
        grid=(bsz, hp, nb // 2),
        in_specs=[
            pl.BlockSpec((1, Q_TILE, LANES), lambda b, h, i: (b, i, h)),
            pl.BlockSpec((1, seq, LANES), lambda b, h, i: (b, 0, h)),
            pl.BlockSpec((1, nb, 1, LANES, MOBA_BLOCK), lambda b, h, i: (b, 0, h, 0, 0)),
            pl.BlockSpec((1, 1, 2, SEL_ROWS, Q_TILE), lambda b, h, i: (b, h, 0, 0, i)),
        ],
        out_specs=pl.BlockSpec((1, Q_TILE // SSM_CHUNK, SSM_CHUNK * LANES),
                               lambda b, h, i: (h, b * (seq // Q_TILE) + i, 0)),
        out_shape=jax.ShapeDtypeStruct((hp, bsz * seq // SSM_CHUNK, SSM_CHUNK * LANES), BF16),
        scratch_shapes=[
            pltpu.VMEM((nb // 2, V_ROWS, PAIR), BF16), pltpu.VMEM((nb // 2, V_ROWS, PAIR), BF16),
            pltpu.VMEM((2, PAIR, Q_TILE), F32), pltpu.VMEM((2, PAIR, Q_TILE), F32),
            pltpu.VMEM((2, PAIR, Q_TILE), BF16), pltpu.VMEM((2, PAIR, Q_TILE), BF16),
            pltpu.VMEM((2, SUBLANES, Q_TILE), F32),
            pltpu.VMEM((2, SUBLANES, Q_TILE), F32),
            pltpu.VMEM((2, V_ROWS, Q_TILE), F32),
            pltpu.VMEM((1, Q_TILE, LANES), F32),
        ],
        compiler_params=pltpu.CompilerParams(
            dimension_semantics=("parallel", "parallel", "arbitrary"), vmem_limit_bytes=VMEM_LIMIT),
        name="moba_attention",
    )(q, k, vt, sel)


def _s5_tables(a_re, a_im, log_dt, b_re, b_im, c_re, c_im, d_skip):
    hi = lax.Precision.HIGHEST
    L = SSM_CHUNK
    g, p = a_re.shape
    dt = jnp.exp(log_dt)[:, None]
    mag = jnp.exp(a_re * dt)
    lb_re, lb_im = mag * jnp.cos(a_im * dt), mag * jnp.sin(a_im * dt)
    den = a_re * a_re + a_im * a_im
    n_re, n_im = lb_re - 1.0, lb_im
    z_re = (n_re * a_re + n_im * a_im) / den
    z_im = (n_im * a_re - n_re * a_im) / den
    j = jnp.arange(L + 1, dtype=F32)[None, :, None]
    pmag = jnp.exp(j * (a_re * dt)[:, None, :])
    ang = j * (a_im * dt)[:, None, :]
    pr, pi = pmag * jnp.cos(ang), pmag * jnp.sin(ang)
    zb_re = z_re[:, :, None] * b_re - z_im[:, :, None] * b_im
    zb_im = z_re[:, :, None] * b_im + z_im[:, :, None] * b_re
    cl_re = c_re[:, None] * pr[:, :, None, :] - c_im[:, None] * pi[:, :, None, :]
    cl_im = c_re[:, None] * pi[:, :, None, :] + c_im[:, None] * pr[:, :, None, :]
    kern = (jnp.einsum('gjop,gpi->gjoi', cl_re[:, :L], zb_re, precision=hi)
            - jnp.einsum('gjop,gpi->gjoi', cl_im[:, :L], zb_im, precision=hi))
    kern = kern.at[:, 0].add(jax.vmap(jnp.diag)(d_skip))
    s_i = jnp.arange(L)[:, None]
    t_i = jnp.arange(L)[None, :]
    lag = t_i - s_i
    toep = jnp.where((lag >= 0)[None, :, :, None, None], kern[:, jnp.maximum(lag, 0)], 0.0)
    toep = toep.transpose(0, 1, 4, 2, 3).reshape(g, L * SSM_GROUP, L * SSM_GROUP)
    rev = L - 1 - jnp.arange(L)
    w_re = pr[:, rev][:, :, :, None] * zb_re[:, None] - pi[:, rev][:, :, :, None] * zb_im[:, None]
    w_im = pr[:, rev][:, :, :, None] * zb_im[:, None] + pi[:, rev][:, :, :, None] * zb_re[:, None]
    w_re = w_re.transpose(0, 1, 3, 2).reshape(g, L * SSM_GROUP, p)
    w_im = w_im.transpose(0, 1, 3, 2).reshape(g, L * SSM_GROUP, p)
    b_pow = jnp.concatenate([w_re, w_im], axis=-1)
    b_pow_sw = jnp.concatenate([w_im, w_re], axis=-1)
    cp_re = cl_re[:, 1:].transpose(0, 3, 1, 2).reshape(g, p, L * SSM_GROUP)
    cp_im = cl_im[:, 1:].transpose(0, 3, 1, 2).reshape(g, p, L * SSM_GROUP)
    c_pow = jnp.concatenate([cp_re, -cp_im], axis=1)
    lr, li = pr[:, L], pi[:, L]
    lam = jnp.stack([jnp.concatenate([lr, lr], -1), jnp.concatenate([-li, li], -1)], axis=1)
    tr = lambda t: t.transpose(0, 2, 1).astype(BF16)
    return tr(toep), tr(b_pow), tr(b_pow_sw), tr(c_pow), lam


def _s5_input_kernel(x_ref, wut_ref, ut_ref):
    ut = _dot_nt(wut_ref[...], x_ref[...].astype(BF16))
    ut_ref[...] = ut.reshape(ut_ref.shape).astype(ut_ref.dtype)


def _s5_input(xs, w_ut, n_groups, bsz, seq):
    L = SSM_CHUNK
    nc = seq // L
    dm = xs.shape[1] // L
    assert nc % LANES == 0
    return pl.pallas_call(
        _s5_input_kernel,
        grid=(bsz, L),
        in_specs=[
            pl.BlockSpec((nc, dm), lambda b, s: (b, s)),
            pl.BlockSpec(w_ut.shape, lambda b, s: (0, 0)),
        ],
        out_specs=pl.BlockSpec((n_groups, SSM_GROUP, nc), lambda b, s: (0, s, b)),
        out_shape=jax.ShapeDtypeStruct((n_groups, L * SSM_GROUP, bsz * nc), BF16),
        compiler_params=pltpu.CompilerParams(
            dimension_semantics=("parallel", "parallel"), vmem_limit_bytes=VMEM_LIMIT),
        name="s5_input",
    )(xs, w_ut)


def _s5_kernel(ut_ref, toept_ref, bpt_ref, bpst_ref, cpt_ref, lam_ref, yt_ref, s_ref, ss_ref, hp_ref, *, bsz, nc):
    ut = ut_ref[0]
    s_ref[...] = _dot(bpt_ref[0], ut).T
    ss_ref[...] = _dot(bpst_ref[0], ut).T
    a = lam_ref[0, 0:1, :]
    bv = lam_ref[0, 1:2, :]

    def body(c, carry):
        h, hs = carry
        rows = pl.ds(c, bsz, stride=nc)
        hp_ref[rows, :] = h
        return a * h + bv * hs + s_ref[rows, :], a * hs - bv * h + ss_ref[rows, :]

    zero = jnp.zeros((bsz, 2 * SSM_STATE), F32)
    lax.fori_loop(0, nc, body, (zero, zero), unroll=8)
    yt_ref[0] = _dot(toept_ref[0], ut) + _dot_nt(cpt_ref[0], hp_ref[...].astype(BF16))


def _s5_scan(ut, tables, bsz, seq):
    toept, bpt, bpst, cpt, lam = tables
    g, w, cols = ut.shape
    nc = seq // SSM_CHUNK
    p2 = 2 * SSM_STATE
    grp = lambda i: (i, 0, 0)
    return pl.pallas_call(
        functools.partial(_s5_kernel, bsz=bsz, nc=nc),
        grid=(g,),
        in_specs=[
            pl.BlockSpec((1, w, cols), grp), pl.BlockSpec((1, w, w), grp),
            pl.BlockSpec((1, p2, w), grp), pl.BlockSpec((1, p2, w), grp),
            pl.BlockSpec((1, w, p2), grp), pl.BlockSpec((1, 2, p2), grp),
        ],
        out_specs=pl.BlockSpec((1, w, cols), grp),
        out_shape=jax.ShapeDtypeStruct((g, w, cols), F32),
        scratch_shapes=[pltpu.VMEM((cols, p2), F32), pltpu.VMEM((cols, p2), F32), pltpu.VMEM((cols, p2), F32)],
        compiler_params=pltpu.CompilerParams(dimension_semantics=("parallel",), vmem_limit_bytes=VMEM_LIMIT),
        name="s5_scan",
    )(ut, toept, bpt, bpst, cpt, lam)


def _merge_kernel(x_ref, attn_ref, yt_ref, ga_ref, gs_ref, wglut_ref, wap_ref, wsp_ref, wout_ref, g_ref, b_ref,
                  h_ref, *, alpha):
    gh, nc = yt_ref.shape[0] * yt_ref.shape[1], yt_ref.shape[2]
    gt = jax.nn.gelu(yt_ref[...].reshape(gh, nc))
    ssm = (gt * jax.nn.sigmoid(_dot(wglut_ref[...], gt.astype(BF16)))).T
    attn = jnp.concatenate([attn_ref[j] for j in range(attn_ref.shape[0])], axis=1)
    merged = (ga_ref[...].astype(F32) * _dot(attn, wap_ref[...])
              + gs_ref[...].astype(F32) * _dot(ssm.astype(BF16), wsp_ref[...]))
    mix = _dot(merged.astype(BF16), wout_ref[...])
    h_ref[...] = _layer_norm(alpha * x_ref[...] + mix, g_ref[...], b_ref[...])


def _merge(xs, attn, yt, ga, gs, w_glu_t, w_ap, w_sp, w_out, ln_g, ln_b, alpha, bsz, seq):
    L = SSM_CHUNK
    nc = seq // L
    dm = xs.shape[1] // L
    hp, g = attn.shape[0], yt.shape[0]
    step = lambda b, s: (b, s)
    full = lambda b, s: (0, 0)
    return pl.pallas_call(
        functools.partial(_merge_kernel, alpha=alpha),
        grid=(bsz, L),
        in_specs=[
            pl.BlockSpec((nc, dm), step), pl.BlockSpec((hp, nc, LANES), lambda b, s: (0, b, s)),
            pl.BlockSpec((g, SSM_GROUP, nc), lambda b, s: (0, s, b)),
            pl.BlockSpec((nc, dm), step), pl.BlockSpec((nc, dm), step),
            pl.BlockSpec(w_glu_t.shape, full), pl.BlockSpec(w_ap.shape, full), pl.BlockSpec(w_sp.shape, full),
            pl.BlockSpec(w_out.shape, full), pl.BlockSpec((1, dm), full), pl.BlockSpec((1, dm), full),
        ],
        out_specs=pl.BlockSpec((nc, dm), step),
        out_shape=jax.ShapeDtypeStruct(xs.shape, F32),
        compiler_params=pltpu.CompilerParams(
            dimension_semantics=("parallel", "parallel"), vmem_limit_bytes=VMEM_LIMIT),
        name="merge_ln1",
    )(xs, attn, yt, ga, gs, w_glu_t, w_ap, w_sp, w_out, ln_g, ln_b)


def _ffn_kernel(hs_ref, wup_ref, cw_ref, wd_ref, g_ref, b_ref, o_ref, halo_ref, act_ref, h_ref,
                *, alpha, tiles_per_seq, dff):
    i = pl.program_id(0)
    h = _from_step_layout(hs_ref, h_ref)
    hb = h.astype(BF16)
    tm = h.shape[0]
    ck = FF_CHUNK
    row8 = lax.broadcasted_iota(jnp.int32, (SUBLANES, ck), 0)

    @pl.when((i % tiles_per_seq) == 0)
    def _():
        halo_ref[...] = jnp.zeros_like(halo_ref)

    def conv(up, c0):
        cols = slice(c0, c0 + ck)
        prev = halo_ref[:, cols]
        halo_ref[:, cols] = up[tm - SUBLANES:]
        p1 = prev[SUBLANES - 1:SUBLANES]
        p2 = prev[SUBLANES - 2:SUBLANES - 1]
        r1 = pltpu.roll(up, 1, 0)
        r2 = pltpu.roll(up, 2, 0)
        d1 = jnp.concatenate([jnp.where(row8 == 0, p1, r1[:SUBLANES]), r1[SUBLANES:]], axis=0)
        d2 = jnp.concatenate([jnp.where(row8 == 0, p2, jnp.where(row8 == 1, p1, r2[:SUBLANES])), r2[SUBLANES:]],
                             axis=0)
        cw = cw_ref[:, cols]
        return cw[3:4] + cw[0:1] * d2 + cw[1:2] * d1 + cw[2:3] * up

    for j in range(dff // ck):
        c0 = j * ck
        val = conv(_dot(hb, wup_ref[:, c0:c0 + ck]), c0)
        gate = conv(_dot(hb, wup_ref[:, dff + c0:dff + c0 + ck]), dff + c0)
        act_ref[:, c0:c0 + ck] = (jax.nn.gelu(gate) * val).astype(BF16)
    ff = _dot(act_ref[...], wd_ref[...])
    o_ref[...] = _layer_norm(alpha * h + ff, g_ref[...], b_ref[...])


def _ffn(hs, w_up, conv_w, conv_b, w_down, ln_g, ln_b, alpha, seq):
    L = SSM_CHUNK
    n, dm = hs.shape[0] * L, hs.shape[1] // L
    dff = w_down.shape[0]
    tm = ROW_TILE
    assert dff % FF_CHUNK == 0
    cw = jnp.concatenate([conv_w, conv_b[None, :]], axis=0)
    cw = jnp.pad(cw, ((0, SUBLANES - cw.shape[0]), (0, 0)))
    row = lambda i: (i, 0)
    full = lambda i: (0, 0)
    once = pl.Buffered(1)
    return pl.pallas_call(
        functools.partial(_ffn_kernel, alpha=alpha, tiles_per_seq=seq // tm, dff=dff),
        grid=(n // tm,),
        in_specs=[
            pl.BlockSpec((tm // L, L * dm), row),
            pl.BlockSpec(w_up.shape, full, pipeline_mode=once),
            pl.BlockSpec(cw.shape, full),
            pl.BlockSpec(w_down.shape, full, pipeline_mode=once),
            pl.BlockSpec((1, dm), full), pl.BlockSpec((1, dm), full),
        ],
        out_specs=pl.BlockSpec((tm, dm), row),
        out_shape=jax.ShapeDtypeStruct((n, dm), F32),
        scratch_shapes=[
            pltpu.VMEM((SUBLANES, 2 * dff), F32),
            pltpu.VMEM((tm, dff), BF16),
            pltpu.VMEM((dm // LANES, tm, LANES), F32),
        ],
        compiler_params=pltpu.CompilerParams(dimension_semantics=("arbitrary",), vmem_limit_bytes=VMEM_LIMIT),
        name="conv_ffn_ln2",
    )(hs, w_up, cw, w_down, ln_g, ln_b)


def kernel(x, w_in, w_attn_proj, ssm_a_re, ssm_a_im, ssm_log_dt, ssm_b_re, ssm_b_im, ssm_c_re, ssm_c_im, ssm_d,
           w_glu, w_ssm_proj, w_out, ln1_g, ln1_b, w_up, conv_w, conv_b, w_down, ln2_g, ln2_b):
    bsz, seq, dm = x.shape
    depth = w_in.shape[0]
    alpha = (2.0 * depth) ** 0.25
    n = bsz * seq
    assert seq % ROW_TILE == 0 and ROW_TILE % MOBA_BLOCK == 0 and seq % SSM_CHUNK == 0
    h = x
    for l in range(depth):
        w_in_b = w_in[l].astype(BF16)
        q, k, vt, ga, gs, kmean, xs = _in_projection(h.reshape(n, dm), w_in_b, seq)
        aw = q.shape[1]
        nb = seq // MOBA_BLOCK
        q3 = q.reshape(bsz, seq, aw)
        sel = _moba_select(q3, kmean.reshape(bsz, nb, aw))
        attn = _moba_attention(q3, k.reshape(bsz, seq, aw), vt.reshape(bsz, nb, *vt.shape[1:]), sel)
        tables = _s5_tables(ssm_a_re[l], ssm_a_im[l], ssm_log_dt[l], ssm_b_re[l], ssm_b_im[l],
                            ssm_c_re[l], ssm_c_im[l], ssm_d[l])
        n_groups, ssm_w = tables[0].shape[0], tables[0].shape[0] * SSM_GROUP
        ut = _s5_input(xs, w_in_b[:, 3 * aw:3 * aw + ssm_w].T, n_groups, bsz, seq)
        yt = _s5_scan(ut, tables, bsz, seq)
        h1 = _merge(xs, attn, yt, ga, gs, w_glu[l].T.astype(BF16), w_attn_proj[l].astype(BF16),
                    w_ssm_proj[l].astype(BF16), w_out[l].astype(BF16), ln1_g[l][None], ln1_b[l][None], alpha,
                    bsz, seq)
        h = _ffn(h1, w_up[l].astype(BF16), conv_w[l], conv_b[l], w_down[l].astype(BF16),
                 ln2_g[l][None], ln2_b[l][None], alpha, seq).reshape(bsz, seq, dm)
    return h
```

```python
import functools
import math

import jax
import jax.numpy as jnp
from jax import lax
from jax.experimental import pallas as pl
from jax.experimental.pallas import tpu as pltpu

F32 = jnp.float32
BF16 = jnp.bfloat16

N_HEADS = 8
HEAD_DIM = 64
ROT_DIM = HEAD_DIM // 4
ROPE_THETA = 500000.0
MOBA_BLOCK = 256
MOBA_TOP_K = 3
SSM_GROUP = 16
SSM_STATE = 64
CONV_WIDTH = 3
LN_EPS = 1e-5
NEG_INF = -1e30

LANES = 128
SUBLANES = 8
VMEM_LIMIT = 56 * 1024 * 1024

SSM_CHUNK = 16
ROW_TILE = 512
FF_CHUNK = 256

_NT = (((1,), (1,)), ((), ()))


def _dot(a, b):
    return jnp.dot(a, b, preferred_element_type=F32)


def _dot_nt(a, b):
    return lax.dot_general(a, b, _NT, preferred_element_type=F32)


def _layer_norm(t, g, b):
    mu = jnp.mean(t, axis=-1, keepdims=True)
    d = t - mu
    var = jnp.mean(d * d, axis=-1, keepdims=True)
    return d * lax.rsqrt(var + LN_EPS) * g + b


def _to_step_layout(dst_ref, src, slab_ref):
    chunks, width = dst_ref.shape[0], src.shape[1]
    for j in range(width // LANES):
        slab_ref[j] = src[:, j * LANES:(j + 1) * LANES]
    for s in range(SSM_CHUNK):
        for j in range(width // LANES):
            c0 = s * width + j * LANES
            dst_ref[:, c0:c0 + LANES] = slab_ref[j, pl.ds(s, chunks, stride=SSM_CHUNK), :].astype(dst_ref.dtype)


def _from_step_layout(src_ref, slab_ref):
    chunks, width = src_ref.shape[0], slab_ref.shape[0] * LANES
    for s in range(SSM_CHUNK):
        for j in range(width // LANES):
            c0 = s * width + j * LANES
            slab_ref[j, pl.ds(s, chunks, stride=SSM_CHUNK), :] = src_ref[:, c0:c0 + LANES].astype(slab_ref.dtype)
    return jnp.concatenate([slab_ref[j] for j in range(width // LANES)], axis=1)


def _inproj_kernel(x_ref, w_ref, wvt_ref, cos_ref, s1_ref, s2_ref,
                   q_ref, k_ref, vt_ref, us_ref, ga_ref, gs_ref, km_ref, tmp_ref, *, aw, sw, dm, scale):
    xb = x_ref[...].astype(BF16)
    cos_t, s1_t, s2_t = cos_ref[...], s1_ref[...], s2_ref[...]
    tm = xb.shape[0]

    def rotary(t):
        return t * cos_t + pltpu.roll(t, ROT_DIM // 2, 1) * s1_t + pltpu.roll(t, LANES - ROT_DIM // 2, 1) * s2_t

    q = _dot(xb, w_ref[:, 0:aw])
    for j in range(aw // LANES):
        sl = slice(j * LANES, (j + 1) * LANES)
        q_ref[:, sl] = (rotary(q[:, sl]) * scale).astype(BF16)
    k = _dot(xb, w_ref[:, aw:2 * aw])
    for j in range(aw // LANES):
        sl = slice(j * LANES, (j + 1) * LANES)
        kr = rotary(k[:, sl])
        k_ref[:, sl] = kr.astype(BF16)
        for r in range(tm // MOBA_BLOCK):
            km_ref[r, :, sl] = jnp.mean(kr[r * MOBA_BLOCK:(r + 1) * MOBA_BLOCK], axis=0, keepdims=True)
    vt = _dot_nt(wvt_ref[...], xb)
    for r in range(tm // MOBA_BLOCK):
        for j in range(aw // LANES):
            vt_ref[r, j] = vt[j * LANES:(j + 1) * LANES, r * MOBA_BLOCK:(r + 1) * MOBA_BLOCK].astype(BF16)
    _to_step_layout(us_ref, _dot(xb, w_ref[:, 3 * aw:3 * aw + sw]), tmp_ref)
    c0 = 3 * aw + sw
    ga_ref[...] = jax.nn.sigmoid(_dot(xb, w_ref[:, c0:c0 + dm])).astype(BF16)
    gs_ref[...] = jax.nn.sigmoid(_dot(xb, w_ref[:, c0 + dm:c0 + 2 * dm])).astype(BF16)


def _rotary_tables(seq):
    half = ROT_DIM // 2
    inv_freq = ROPE_THETA ** (-jnp.arange(0, ROT_DIM, 2, dtype=F32) / ROT_DIM)
    ang = jnp.arange(seq, dtype=F32)[:, None] * inv_freq[None, :]
    cos, sin = jnp.cos(ang), jnp.sin(ang)
    ones = jnp.ones((seq, HEAD_DIM - ROT_DIM), F32)
    zeros = jnp.zeros((seq, HEAD_DIM - ROT_DIM), F32)
    zh = jnp.zeros((seq, half), F32)
    cos_h = jnp.concatenate([cos, cos, ones], axis=1)
    s1_h = jnp.concatenate([zh, sin, zeros], axis=1)
    s2_h = jnp.concatenate([-sin, zh, zeros], axis=1)
    rep = LANES // HEAD_DIM
    return tuple(jnp.tile(t, (1, rep)) for t in (cos_h, s1_h, s2_h))


def _in_projection(x2, w_in, seq):
    n, dm = x2.shape
    aw, sw = N_HEADS * HEAD_DIM, w_in.shape[1] - 3 * N_HEADS * HEAD_DIM - 2 * dm
    tm = ROW_TILE
    cos_t, s1_t, s2_t = _rotary_tables(seq)
    w_vt = w_in[:, 2 * aw:3 * aw].T
    hp = aw // LANES
    tiles_per_seq = seq // tm
    row = lambda i: (i, 0)
    tab = lambda i: (i % tiles_per_seq, 0)
    L = SSM_CHUNK
    step_rows = lambda i: (i, 0)
    out_shape = (
        jax.ShapeDtypeStruct((n, aw), BF16), jax.ShapeDtypeStruct((n, aw), BF16),
        jax.ShapeDtypeStruct((n // MOBA_BLOCK, hp, LANES, MOBA_BLOCK), BF16),
        jax.ShapeDtypeStruct((n // L, L * sw), BF16),
        jax.ShapeDtypeStruct((n, dm), BF16), jax.ShapeDtypeStruct((n, dm), BF16),
        jax.ShapeDtypeStruct((n // MOBA_BLOCK, 1, aw), F32),
    )
    return pl.pallas_call(
        functools.partial(_inproj_kernel, aw=aw, sw=sw, dm=dm, scale=HEAD_DIM ** -0.5 * math.log2(math.e)),
        grid=(n // tm,),
        in_specs=[
            pl.BlockSpec((tm, dm), row),
            pl.BlockSpec(w_in.shape, lambda i: (0, 0)),
            pl.BlockSpec(w_vt.shape, lambda i: (0, 0)),
            pl.BlockSpec((tm, LANES), tab), pl.BlockSpec((tm, LANES), tab), pl.BlockSpec((tm, LANES), tab),
        ],
        out_specs=(
            pl.BlockSpec((tm, aw), row), pl.BlockSpec((tm, aw), row),
            pl.BlockSpec((tm // MOBA_BLOCK, hp, LANES, MOBA_BLOCK), lambda i: (i, 0, 0, 0)),
            pl.BlockSpec((tm // L, L * sw), step_rows),
            pl.BlockSpec((tm, dm), row), pl.BlockSpec((tm, dm), row),
            pl.BlockSpec((tm // MOBA_BLOCK, 1, aw), lambda i: (i, 0, 0)),
        ),
        out_shape=out_shape,
        scratch_shapes=[pltpu.VMEM((sw // LANES, tm, LANES), F32)],
        compiler_params=pltpu.CompilerParams(dimension_semantics=("parallel",), vmem_limit_bytes=VMEM_LIMIT),
        name="in_projection",
    )(x2, w_in, w_vt, cos_t, s1_t, s2_t)


PAIR = 2 * MOBA_BLOCK
Q_TILE = PAIR
MASKED = 2.0 * NEG_INF
V_ROWS = HEAD_DIM + 16
SEL_ROWS = SUBLANES
SEL_CHUNK = 1024


def _head_split(q2):
    qf = q2.astype(F32)
    lane = lax.broadcasted_iota(jnp.int32, qf.shape, 1)
    return (jnp.where(lane < HEAD_DIM, qf, 0.0).astype(BF16), jnp.where(lane >= HEAD_DIM, qf, 0.0).astype(BF16))


def _select_kernel(q_ref, km_ref, sel_ref, *, nb):
    kmb = km_ref[0].astype(BF16)
    seq = q_ref.shape[1]
    shape = (nb, SEL_CHUNK)
    blk = lax.broadcasted_iota(jnp.int32, shape, 0)
    for c in range(seq // SEL_CHUNK):
        cols = slice(c * SEL_CHUNK, (c + 1) * SEL_CHUNK)
        own = (lax.broadcasted_iota(jnp.int32, shape, 1) + c * SEL_CHUNK) // MOBA_BLOCK
        qh = _head_split(q_ref[0, cols, :])
        for h in range(2):
            g = jnp.where(blk < own, _dot_nt(kmb, qh[h]), NEG_INF)
            rows = []
            for _ in range(min(MOBA_TOP_K, nb)):
                mx = jnp.max(g, axis=0, keepdims=True)
                first = jnp.min(jnp.where(g == mx, blk, nb), axis=0, keepdims=True)
                rows.append(jnp.where(first < own[0:1], first, -1))
                g = jnp.where(blk == first, -jnp.inf, g)
            rows.append(jnp.full((SEL_ROWS - len(rows), SEL_CHUNK), -1, jnp.int32))
            sel_ref[0, 0, h, :, cols] = jnp.concatenate(rows, axis=0)


def _moba_select(q, kmean):
    bsz, seq, aw = q.shape
    nb = seq // MOBA_BLOCK
    hp = aw // LANES
    assert seq % SEL_CHUNK == 0
    return pl.pallas_call(
        functools.partial(_select_kernel, nb=nb),
        grid=(bsz, hp),
        in_specs=[
            pl.BlockSpec((1, seq, LANES), lambda b, h: (b, 0, h)),
            pl.BlockSpec((1, nb, LANES), lambda b, h: (b, 0, h)),
        ],
        out_specs=pl.BlockSpec((1, 1, 2, SEL_ROWS, seq), lambda b, h: (b, h, 0, 0, 0)),
        out_shape=jax.ShapeDtypeStruct((bsz, hp, 2, SEL_ROWS, seq), jnp.int32),
        compiler_params=pltpu.CompilerParams(
            dimension_semantics=("parallel", "parallel"), vmem_limit_bytes=VMEM_LIMIT),
        name="moba_select",
    )(q, kmean)


def _attn_kernel(q_ref, k_ref, vt_ref, sel_ref, o_ref,
                 va_ref, vb_ref, s0_ref, s1_ref, p0_ref, p1_ref, cm_ref, al_ref, acc_ref, *, nb):
    i = pl.program_id(2)
    blk = MOBA_BLOCK
    v_refs = (va_ref, vb_ref)
    s_refs = (s0_ref, s1_ref)
    p_refs = (p0_ref, p1_ref)

    @pl.when(i == 0)
    def _():
        ones = jnp.ones((V_ROWS - HEAD_DIM, blk), BF16)

        def fill(j, c):
            for half in range(2):
                vt = vt_ref[0, 2 * j + half, 0]
                cols = slice(half * blk, (half + 1) * blk)
                for h in range(2):
                    v_refs[h][j, 0:HEAD_DIM, cols] = vt[h * HEAD_DIM:(h + 1) * HEAD_DIM]
                    v_refs[h][j, HEAD_DIM:V_ROWS, cols] = ones
            return c
        lax.fori_loop(0, nb // 2, fill, 0)

    qh = _head_split(q_ref[0])
    sel = (sel_ref[0, 0, 0], sel_ref[0, 0, 1])

    def chosen(h, n):
        sv = sel[h]
        hit = sv[0:1] == n
        for r in range(1, MOBA_TOP_K):
            hit = jnp.logical_or(hit, sv[r:r + 1] == n)
        return hit

    def past_bias(pair):
        return [tuple(jnp.where(chosen(h, 2 * pair + half), 0.0, MASKED) for half in range(2)) for h in range(2)]

    key_i = lax.broadcasted_iota(jnp.int32, (blk, blk), 0)
    qry_i = lax.broadcasted_iota(jnp.int32, (blk, blk), 1)
    causal = key_i <= qry_i

    def own_mask(s, h):
        top = jnp.concatenate([jnp.where(causal, s[:blk, :blk], MASKED),
                               jnp.where(chosen(h, 2 * i)[:, blk:], s[:blk, blk:], MASKED)], axis=1)
        bot = jnp.concatenate([jnp.full((blk, blk), MASKED, F32),
                               jnp.where(causal, s[blk:, blk:], MASKED)], axis=1)
        return jnp.concatenate([top, bot], axis=0)

    def stage_a(pair, slot, own):
        kp = k_ref[0, pl.ds(pl.multiple_of(pair * PAIR, PAIR), PAIR), :]
        for h in range(2):
            s = _dot_nt(kp, qh[h])
            if own:
                s = own_mask(s, h)
            s_refs[slot][h] = s
            cm_ref[slot, 2 * h:2 * h + 1, :] = jnp.max(s[:blk], axis=0, keepdims=True)
            cm_ref[slot, 2 * h + 1:2 * h + 2, :] = jnp.max(s[blk:], axis=0, keepdims=True)

    def stage_b(slot, m, bias):
        m_out = []
        for h in range(2):
            cm0 = cm_ref[slot, 2 * h:2 * h + 1, :]
            cm1 = cm_ref[slot, 2 * h + 1:2 * h + 2, :]
            if bias is None:
                m_new = jnp.maximum(m[h], jnp.maximum(cm0, cm1))
                c0 = c1 = m_new
            else:
                b0, b1 = bias[h]
                m_new = jnp.maximum(m[h], jnp.maximum(cm0 + b0, cm1 + b1))
                c0, c1 = m_new - b0, m_new - b1
            al_ref[slot, h:h + 1, :] = jnp.exp2(m[h] - m_new)
            p_refs[slot][h, 0:blk] = jnp.exp2(s_refs[slot][h, 0:blk] - c0).astype(BF16)
            p_refs[slot][h, blk:PAIR] = jnp.exp2(s_refs[slot][h, blk:PAIR] - c1).astype(BF16)
            m_out.append(m_new)
        return tuple(m_out)

    def stage_c(pair, slot):
        for h in range(2):
            pv = _dot(v_refs[h][pair], p_refs[slot][h])
            acc_ref[h] = acc_ref[h] * al_ref[slot, h:h + 1, :] + pv

    def trip(t, slot, m):
        stage_a(t - 1, slot, False)
        m = stage_b(1 - slot, m, past_bias(t - 2))
        stage_c(jnp.where(t == 2, i, t - 3), slot)
        return m

    acc_ref[...] = jnp.zeros_like(acc_ref)
    floor = jnp.full((1, Q_TILE), NEG_INF, F32)
    stage_a(i, 0, True)
    stage_a(0, 1, False)
    m0 = stage_b(0, (floor, floor), None)
    n_list = i + 1

    @pl.when(i == 0)
    def _():
        stage_c(i, 0)

    @pl.when(i > 0)
    def _():
        def double_trip(d, m):
            t = 2 + 2 * d
            return trip(t + 1, 1, trip(t, 0, m))

        m1 = lax.fori_loop(0, (n_list - 2) // 2, double_trip, m0)

        @pl.when(n_list % 2 == 1)
        def _():
            m2 = trip(n_list - 1, 0, m1)
            stage_b(0, m2, past_bias(n_list - 2))
            stage_c(n_list - 3, 1)
            stage_c(n_list - 2, 0)

        @pl.when(n_list % 2 == 0)
        def _():
            stage_b(1, m1, past_bias(n_list - 2))
            stage_c(jnp.where(n_list == 2, i, n_list - 3), 0)
            stage_c(n_list - 2, 1)

    outs = []
    for h in range(2):
        acc = acc_ref[h]
        outs.append(acc[:HEAD_DIM] / acc[HEAD_DIM:HEAD_DIM + 1])
    o_ref[0] = jnp.concatenate(outs, axis=0).T.astype(o_ref.dtype)


def _moba_attention(q, k, vt, sel):
    bsz, seq, aw = q.shape
    nb = seq // MOBA_BLOCK
    hp = aw // LANES
    assert nb % 2 == 0
    return pl.pallas_call(
        functools.partial(_attn_kernel, nb=nb),
        grid=(bsz, hp, nb // 2),
        in_specs=[
            pl.BlockSpec((1, Q_TILE, LANES), lambda b, h, i: (b, i, h)),
            pl.BlockSpec((1, seq, LANES), lambda b, h, i: (b, 0, h)),
            pl.BlockSpec((1, nb, 1, LANES, MOBA_BLOCK), lambda b, h, i: (b, 0, h, 0, 0)),
            pl.BlockSpec((1, 1, 2, SEL_ROWS, Q_TILE), lambda b, h, i: (b, h, 0, 0, i)),
        ],
        out_specs=pl.BlockSpec((1, Q_TILE, LANES), lambda b, h, i: (b, i, h)),
        out_shape=jax.ShapeDtypeStruct((bsz, seq, aw), BF16),
        scratch_shapes=[
            pltpu.VMEM((nb // 2, V_ROWS, PAIR), BF16), pltpu.VMEM((nb // 2, V_ROWS, PAIR), BF16),
            pltpu.VMEM((2, PAIR, Q_TILE), F32), pltpu.VMEM((2, PAIR, Q_TILE), F32),
            pltpu.VMEM((2, PAIR, Q_TILE), BF16), pltpu.VMEM((2, PAIR, Q_TILE), BF16),
            pltpu.VMEM((2, SUBLANES, Q_TILE), F32),
            pltpu.VMEM((2, SUBLANES, Q_TILE), F32),
            pltpu.VMEM((2, V_ROWS, Q_TILE), F32),
        ],
        compiler_params=pltpu.CompilerParams(
            dimension_semantics=("parallel", "parallel", "arbitrary"), vmem_limit_bytes=VMEM_LIMIT),
        name="moba_attention",
    )(q, k, vt, sel)


def _s5_tables(a_re, a_im, log_dt, b_re, b_im, c_re, c_im, d_skip):
    g, p = a_re.shape
    L, hch = SSM_CHUNK, SSM_GROUP
    w = L * hch
    two = lambda t: jnp.concatenate([t, t], axis=-1)
    operands = (
        log_dt.reshape(g, 1, 1),
        two(a_re).reshape(g, 1, 2 * p), two(a_im).reshape(g, 1, 2 * p),
        a_re.reshape(g, p, 1), a_im.reshape(g, p, 1),
        jnp.tile(b_re, (1, 1, L)), jnp.tile(b_im, (1, 1, L)),
        two(c_re), two(c_im),
        jnp.tile(d_skip, (1, L)).reshape(g, 1, w),
    )
    grp = lambda i: (i, 0, 0)
    out_shape = (
        jax.ShapeDtypeStruct((g, w, w), BF16), jax.ShapeDtypeStruct((g, 2 * p, w), BF16),
        jax.ShapeDtypeStruct((g, 2 * p, w), BF16), jax.ShapeDtypeStruct((g, w, 2 * p), BF16),
        jax.ShapeDtypeStruct((g, 2, 2 * p), F32),
    )
    return pl.pallas_call(
        _s5_tables_kernel,
        grid=(g,),
        in_specs=[pl.BlockSpec((1,) + t.shape[1:], grp) for t in operands],
        out_specs=tuple(pl.BlockSpec((1,) + t.shape[1:], grp) for t in out_shape),
        out_shape=out_shape,
        compiler_params=pltpu.CompilerParams(dimension_semantics=("parallel",), vmem_limit_bytes=VMEM_LIMIT),
        name="s5_tables",
    )(*operands)


def _s5_tables_kernel(ldt_ref, ar2_ref, ai2_ref, arc_ref, aic_ref, btr_ref, bti_ref, c2r_ref, c2i_ref, dskip_ref,
                      toept_ref, bpt_ref, bpst_ref, cpt_ref, lam_ref):
    L, hch, p = SSM_CHUNK, SSM_GROUP, SSM_STATE
    w = L * hch
    dt = jnp.exp(ldt_ref[0])

    ar, ai = ar2_ref[0] * dt, ai2_ref[0] * dt
    re_half = lax.broadcasted_iota(jnp.int32, (1, 2 * p), 1) < p

    def lam_pow_rows(e):
        mag, ang = jnp.exp(e * ar), e * ai
        return mag * jnp.cos(ang), mag * jnp.sin(ang)

    c_re = jnp.concatenate([c2r_ref[0]] * L, axis=0)
    c_im = jnp.concatenate([c2i_ref[0]] * L, axis=0)
    x_c = jnp.where(re_half, c_re, -c_im)
    y_c = jnp.where(re_half, -c_im, -c_re)
    step = (lax.broadcasted_iota(jnp.int32, (w, 1), 0) // hch).astype(F32)
    pr, pi = lam_pow_rows(step + 1.0)
    cpt_ref[0] = (x_c * pr + y_c * pi).astype(cpt_ref.dtype)
    pr, pi = lam_pow_rows(step)
    c_lag = x_c * pr + y_c * pi
    pr, pi = lam_pow_rows(jnp.full((1, 1), float(L), F32))
    lam_ref[0, 0:1, :] = pr
    lam_ref[0, 1:2, :] = jnp.where(re_half, -pi, pi)

    a_r, a_i = arc_ref[0], aic_ref[0]
    mag = jnp.exp(a_r * dt)
    n_re, n_im = mag * jnp.cos(a_i * dt) - 1.0, mag * jnp.sin(a_i * dt)
    den = a_r * a_r + a_i * a_i
    z_re = (n_re * a_r + n_im * a_i) / den
    z_im = (n_im * a_r - n_re * a_i) / den
    b_r, b_i = btr_ref[0], bti_ref[0]
    zb_re = z_re * b_r - z_im * b_i
    zb_im = z_re * b_i + z_im * b_r
    e = (L - 1 - lax.broadcasted_iota(jnp.int32, (1, w), 1) // hch).astype(F32)
    mag, ang = jnp.exp(e * (a_r * dt)), e * (a_i * dt)
    q_re, q_im = mag * jnp.cos(ang), mag * jnp.sin(ang)
    w_re = q_re * zb_re - q_im * zb_im
    w_im = q_re * zb_im + q_im * zb_re
    bpt_ref[0] = jnp.concatenate([w_re, w_im], axis=0).astype(bpt_ref.dtype)
    bpst_ref[0] = jnp.concatenate([w_im, w_re], axis=0).astype(bpst_ref.dtype)

    k_lag = jnp.dot(c_lag, jnp.concatenate([zb_re, zb_im], axis=0),
                    preferred_element_type=F32, precision=lax.Precision.HIGHEST)
    lane_s = lax.broadcasted_iota(jnp.int32, (w, w), 1) // hch
    toep = jnp.zeros((w, w), F32)
    for s in range(L):
        shifted = k_lag if s == 0 else jnp.concatenate([jnp.zeros((s * hch, w), F32), k_lag[:w - s * hch]], axis=0)
        toep = jnp.where(lane_s == s, shifted, toep)
    diag = lax.broadcasted_iota(jnp.int32, (w, w), 0) == lax.broadcasted_iota(jnp.int32, (w, w), 1)
    toept_ref[0] = (toep + jnp.where(diag, dskip_ref[0], 0.0)).astype(toept_ref.dtype)


def _s5_input_kernel(us_ref, ut_ref):
    ut = us_ref[...].astype(F32).T
    ut_ref[...] = ut.reshape(ut_ref.shape).astype(ut_ref.dtype)


def _s5_input(us, n_groups, bsz, seq):
    L = SSM_CHUNK
    nc = seq // L
    sw = us.shape[1] // L
    assert nc % LANES == 0
    return pl.pallas_call(
        _s5_input_kernel,
        grid=(bsz, L),
        in_specs=[pl.BlockSpec((nc, sw), lambda b, s: (b, s))],
        out_specs=pl.BlockSpec((n_groups, SSM_GROUP, nc), lambda b, s: (0, s, b)),
        out_shape=jax.ShapeDtypeStruct((n_groups, L * SSM_GROUP, bsz * nc), BF16),
        compiler_params=pltpu.CompilerParams(
            dimension_semantics=("parallel", "parallel"), vmem_limit_bytes=VMEM_LIMIT),
        name="s5_input",
    )(us)


def _s5_kernel(ut_ref, toept_ref, bpt_ref, bpst_ref, cpt_ref, lam_ref, yt_ref, s_ref, ss_ref, hp_ref, *, bsz, nc):
    ut = ut_ref[0]
    s_ref[...] = _dot(bpt_ref[0], ut).T
    ss_ref[...] = _dot(bpst_ref[0], ut).T
    a = lam_ref[0, 0:1, :]
    bv = lam_ref[0, 1:2, :]

    def body(c, carry):
        h, hs = carry
        rows = pl.ds(c, bsz, stride=nc)
        hp_ref[rows, :] = h
        return a * h + bv * hs + s_ref[rows, :], a * hs - bv * h + ss_ref[rows, :]

    zero = jnp.zeros((bsz, 2 * SSM_STATE), F32)
    lax.fori_loop(0, nc, body, (zero, zero), unroll=8)
    yt_ref[0] = _dot(toept_ref[0], ut) + _dot_nt(cpt_ref[0], hp_ref[...].astype(BF16))


def _s5_scan(ut, tables, bsz, seq):
    toept, bpt, bpst, cpt, lam = tables
    g, w, cols = ut.shape
    nc = seq // SSM_CHUNK
    p2 = 2 * SSM_STATE
    grp = lambda i: (i, 0, 0)
    return pl.pallas_call(
        functools.partial(_s5_kernel, bsz=bsz, nc=nc),
        grid=(g,),
        in_specs=[
            pl.BlockSpec((1, w, cols), grp), pl.BlockSpec((1, w, w), grp),
            pl.BlockSpec((1, p2, w), grp), pl.BlockSpec((1, p2, w), grp),
            pl.BlockSpec((1, w, p2), grp), pl.BlockSpec((1, 2, p2), grp),
        ],
        out_specs=pl.BlockSpec((1, w, cols), grp),
        out_shape=jax.ShapeDtypeStruct((g, w, cols), F32),
        scratch_shapes=[pltpu.VMEM((cols, p2), F32), pltpu.VMEM((cols, p2), F32), pltpu.VMEM((cols, p2), F32)],
        compiler_params=pltpu.CompilerParams(dimension_semantics=("parallel",), vmem_limit_bytes=VMEM_LIMIT),
        name="s5_scan",
    )(ut, toept, bpt, bpst, cpt, lam)


def _s5_post_kernel(yt_ref, wglut_ref, ss_ref):
    gh, nc = yt_ref.shape[0] * yt_ref.shape[1], yt_ref.shape[2]
    gt = jax.nn.gelu(yt_ref[...].reshape(gh, nc))
    ssm_t = gt * jax.nn.sigmoid(_dot(wglut_ref[...], gt.astype(BF16)))
    ss_ref[...] = ssm_t.T.astype(ss_ref.dtype)


def _s5_post(yt, w_glu_t, bsz, seq):
    g = yt.shape[0]
    L = SSM_CHUNK
    nc = seq // L
    sw = g * SSM_GROUP
    return pl.pallas_call(
        _s5_post_kernel,
        grid=(bsz, L),
        in_specs=[
            pl.BlockSpec((g, SSM_GROUP, nc), lambda b, s: (0, s, b)),
            pl.BlockSpec(w_glu_t.shape, lambda b, s: (0, 0)),
        ],
        out_specs=pl.BlockSpec((nc, sw), lambda b, s: (b, s)),
        out_shape=jax.ShapeDtypeStruct((bsz * nc, L * sw), BF16),
        compiler_params=pltpu.CompilerParams(
            dimension_semantics=("parallel", "parallel"), vmem_limit_bytes=VMEM_LIMIT),
        name="s5_post",
    )(yt, w_glu_t)


def _merge_kernel(x_ref, attn_ref, ss_ref, ga_ref, gs_ref, wap_ref, wsp_ref, wout_ref, g_ref, b_ref,
                  h_ref, slab_ref, *, alpha):
    ssm = _from_step_layout(ss_ref, slab_ref)
    merged = (ga_ref[...].astype(F32) * _dot(attn_ref[...], wap_ref[...])
              + gs_ref[...].astype(F32) * _dot(ssm.astype(BF16), wsp_ref[...]))
    mix = _dot(merged.astype(BF16), wout_ref[...])
    h_ref[...] = _layer_norm(alpha * x_ref[...] + mix, g_ref[...], b_ref[...])


def _merge(x2, attn, ss, ga, gs, w_ap, w_sp, w_out, ln_g, ln_b, alpha):
    n, dm = x2.shape
    aw = attn.shape[1]
    L = SSM_CHUNK
    sw = ss.shape[1] // L
    tm = ROW_TILE
    row = lambda i: (i, 0)
    full = lambda i: (0, 0)
    return pl.pallas_call(
        functools.partial(_merge_kernel, alpha=alpha),
        grid=(n // tm,),
        in_specs=[
            pl.BlockSpec((tm, dm), row), pl.BlockSpec((tm, aw), row), pl.BlockSpec((tm // L, L * sw), row),
            pl.BlockSpec((tm, dm), row), pl.BlockSpec((tm, dm), row),
            pl.BlockSpec(w_ap.shape, full), pl.BlockSpec(w_sp.shape, full),
            pl.BlockSpec(w_out.shape, full), pl.BlockSpec((1, dm), full), pl.BlockSpec((1, dm), full),
        ],
        out_specs=pl.BlockSpec((tm, dm), row),
        out_shape=jax.ShapeDtypeStruct((n, dm), F32),
        scratch_shapes=[pltpu.VMEM((sw // LANES, tm, LANES), F32)],
        compiler_params=pltpu.CompilerParams(dimension_semantics=("parallel",), vmem_limit_bytes=VMEM_LIMIT),
        name="merge_ln1",
    )(x2, attn, ss, ga, gs, w_ap, w_sp, w_out, ln_g, ln_b)


def _ffn_kernel(h_ref, wup_ref, cw_ref, wd_ref, g_ref, b_ref, o_ref, halo_ref, act_ref,
                *, alpha, tiles_per_seq, dff):
    i = pl.program_id(0)
    h = h_ref[...]
    hb = h.astype(BF16)
    tm = h.shape[0]
    ck = FF_CHUNK
    row8 = lax.broadcasted_iota(jnp.int32, (SUBLANES, ck), 0)

    @pl.when((i % tiles_per_seq) == 0)
    def _():
        halo_ref[...] = jnp.zeros_like(halo_ref)

    def conv(up, c0):
        cols = slice(c0, c0 + ck)
        prev = halo_ref[:, cols]
        halo_ref[:, cols] = up[tm - SUBLANES:]
        p1 = prev[SUBLANES - 1:SUBLANES]
        p2 = prev[SUBLANES - 2:SUBLANES - 1]
        r1 = pltpu.roll(up, 1, 0)
        r2 = pltpu.roll(up, 2, 0)
        d1 = jnp.concatenate([jnp.where(row8 == 0, p1, r1[:SUBLANES]), r1[SUBLANES:]], axis=0)
        d2 = jnp.concatenate([jnp.where(row8 == 0, p2, jnp.where(row8 == 1, p1, r2[:SUBLANES])), r2[SUBLANES:]],
                             axis=0)
        cw = cw_ref[:, cols]
        return cw[3:4] + cw[0:1] * d2 + cw[1:2] * d1 + cw[2:3] * up

    for j in range(dff // ck):
        c0 = j * ck
        val = conv(_dot(hb, wup_ref[:, c0:c0 + ck]), c0)
        gate = conv(_dot(hb, wup_ref[:, dff + c0:dff + c0 + ck]), dff + c0)
        act_ref[:, c0:c0 + ck] = (jax.nn.gelu(gate) * val).astype(BF16)
    ff = _dot(act_ref[...], wd_ref[...])
    o_ref[...] = _layer_norm(alpha * h + ff, g_ref[...], b_ref[...])


def _ffn(h, w_up, conv_w, conv_b, w_down, ln_g, ln_b, alpha, seq):
    n, dm = h.shape
    dff = w_down.shape[0]
    tm = ROW_TILE
    assert dff % FF_CHUNK == 0
    cw = jnp.concatenate([conv_w, conv_b[None, :]], axis=0)
    cw = jnp.pad(cw, ((0, SUBLANES - cw.shape[0]), (0, 0)))
    row = lambda i: (i, 0)
    full = lambda i: (0, 0)
    once = pl.Buffered(1)
    return pl.pallas_call(
        functools.partial(_ffn_kernel, alpha=alpha, tiles_per_seq=seq // tm, dff=dff),
        grid=(n // tm,),
        in_specs=[
            pl.BlockSpec((tm, dm), row),
            pl.BlockSpec(w_up.shape, full, pipeline_mode=once),
            pl.BlockSpec(cw.shape, full),
            pl.BlockSpec(w_down.shape, full, pipeline_mode=once),
            pl.BlockSpec((1, dm), full), pl.BlockSpec((1, dm), full),
        ],
        out_specs=pl.BlockSpec((tm, dm), row),
        out_shape=jax.ShapeDtypeStruct((n, dm), F32),
        scratch_shapes=[
            pltpu.VMEM((SUBLANES, 2 * dff), F32),
            pltpu.VMEM((tm, dff), BF16),
        ],
        compiler_params=pltpu.CompilerParams(dimension_semantics=("arbitrary",), vmem_limit_bytes=VMEM_LIMIT),
        name="conv_ffn_ln2",
    )(h, w_up, cw, w_down, ln_g, ln_b)


def kernel(x, w_in, w_attn_proj, ssm_a_re, ssm_a_im, ssm_log_dt, ssm_b_re, ssm_b_im, ssm_c_re, ssm_c_im, ssm_d,
           w_glu, w_ssm_proj, w_out, ln1_g, ln1_b, w_up, conv_w, conv_b, w_down, ln2_g, ln2_b):
    bsz, seq, dm = x.shape
    depth = w_in.shape[0]
    alpha = (2.0 * depth) ** 0.25
    n = bsz * seq
    assert seq % ROW_TILE == 0 and ROW_TILE % MOBA_BLOCK == 0 and seq % SSM_CHUNK == 0
    h = x
    for l in range(depth):
        w_in_b = w_in[l].astype(BF16)
        h2 = h.reshape(n, dm)
        q, k, vt, us, ga, gs, kmean = _in_projection(h2, w_in[l].astype(BF16), seq)
        aw = q.shape[1]
        nb = seq // MOBA_BLOCK
        q3 = q.reshape(bsz, seq, aw)
        sel = _moba_select(q3, kmean.reshape(bsz, nb, aw))
        attn = _moba_attention(q3, k.reshape(bsz, seq, aw), vt.reshape(bsz, nb, *vt.shape[1:]), sel).reshape(n, aw)
        tables = _s5_tables(ssm_a_re[l], ssm_a_im[l], ssm_log_dt[l], ssm_b_re[l], ssm_b_im[l],
                            ssm_c_re[l], ssm_c_im[l], ssm_d[l])
        ut = _s5_input(us, ssm_a_re.shape[1], bsz, seq)
        ss = _s5_post(_s5_scan(ut, tables, bsz, seq), w_glu[l].T.astype(BF16), bsz, seq)
        h1 = _merge(h2, attn, ss, ga, gs, w_attn_proj[l].astype(BF16), w_ssm_proj[l].astype(BF16),
                    w_out[l].astype(BF16), ln1_g[l][None], ln1_b[l][None], alpha)
        h = _ffn(h1, w_up[l].astype(BF16), conv_w[l], conv_b[l], w_down[l].astype(BF16),
                 ln2_g[l][None], ln2_b[l][None], alpha, seq).reshape(bsz, seq, dm)
    return h
```

```python
import functools
import math

import jax
import jax.numpy as jnp
from jax import lax
from jax.experimental import pallas as pl
from jax.experimental.pallas import tpu as pltpu

F32 = jnp.float32
BF16 = jnp.bfloat16

N_HEADS = 8
HEAD_DIM = 64
ROT_DIM = HEAD_DIM // 4
ROPE_THETA = 500000.0
MOBA_BLOCK = 256
MOBA_TOP_K = 3
SSM_GROUP = 16
SSM_STATE = 64
CONV_WIDTH = 3
LN_EPS = 1e-5
NEG_INF = -1e30

LANES = 128
SUBLANES = 8
VMEM_LIMIT = 56 * 1024 * 1024

SSM_CHUNK = 16
ROW_TILE = 512
FFN_TILE = 1024
FF_CHUNK = 256

_NT = (((1,), (1,)), ((), ()))


def _dot(a, b):
    return jnp.dot(a, b, preferred_element_type=F32)


def _dot_nt(a, b):
    return lax.dot_general(a, b, _NT, preferred_element_type=F32)


def _layer_norm(t, g, b):
    mu = jnp.mean(t, axis=-1, keepdims=True)
    d = t - mu
    var = jnp.mean(d * d, axis=-1, keepdims=True)
    return d * lax.rsqrt(var + LN_EPS) * g + b


def _to_step_layout(dst_ref, src, slab_ref):
    chunks, width = dst_ref.shape[0], src.shape[1]
    for j in range(width // LANES):
        slab_ref[j] = src[:, j * LANES:(j + 1) * LANES]
    for s in range(SSM_CHUNK):
        for j in range(width // LANES):
            c0 = s * width + j * LANES
            dst_ref[:, c0:c0 + LANES] = slab_ref[j, pl.ds(s, chunks, stride=SSM_CHUNK), :].astype(dst_ref.dtype)


def _from_step_layout(src_ref, slab_ref):
    chunks, width = src_ref.shape[0], slab_ref.shape[0] * LANES
    for s in range(SSM_CHUNK):
        for j in range(width // LANES):
            c0 = s * width + j * LANES
            slab_ref[j, pl.ds(s, chunks, stride=SSM_CHUNK), :] = src_ref[:, c0:c0 + LANES].astype(slab_ref.dtype)
    return jnp.concatenate([slab_ref[j] for j in range(width // LANES)], axis=1)


def _inproj_kernel(x_ref, w_ref, wvt_ref, cos_ref, s1_ref, s2_ref,
                   q_ref, k_ref, vt_ref, us_ref, ga_ref, gs_ref, km_ref, tmp_ref, *, aw, sw, dm, scale):
    xb = x_ref[...].astype(BF16)
    cos_t, s1_t, s2_t = cos_ref[...], s1_ref[...], s2_ref[...]
    tm = xb.shape[0]

    def rotary(t):
        return t * cos_t + pltpu.roll(t, ROT_DIM // 2, 1) * s1_t + pltpu.roll(t, LANES - ROT_DIM // 2, 1) * s2_t

    q = _dot(xb, w_ref[:, 0:aw])
    for j in range(aw // LANES):
        sl = slice(j * LANES, (j + 1) * LANES)
        q_ref[:, sl] = (rotary(q[:, sl]) * scale).astype(BF16)
    k = _dot(xb, w_ref[:, aw:2 * aw])
    for j in range(aw // LANES):
        sl = slice(j * LANES, (j + 1) * LANES)
        kr = rotary(k[:, sl])
        k_ref[:, sl] = kr.astype(BF16)
        for r in range(tm // MOBA_BLOCK):
            km_ref[r, :, sl] = jnp.mean(kr[r * MOBA_BLOCK:(r + 1) * MOBA_BLOCK], axis=0, keepdims=True)
    vt = _dot_nt(wvt_ref[...], xb)
    for r in range(tm // MOBA_BLOCK):
        for j in range(aw // LANES):
            vt_ref[r, j] = vt[j * LANES:(j + 1) * LANES, r * MOBA_BLOCK:(r + 1) * MOBA_BLOCK].astype(BF16)
    _to_step_layout(us_ref, _dot(xb, w_ref[:, 3 * aw:3 * aw + sw]), tmp_ref)
    c0 = 3 * aw + sw
    ga_ref[...] = jax.nn.sigmoid(_dot(xb, w_ref[:, c0:c0 + dm])).astype(BF16)
    gs_ref[...] = jax.nn.sigmoid(_dot(xb, w_ref[:, c0 + dm:c0 + 2 * dm])).astype(BF16)


def _rotary_tables(seq):
    half = ROT_DIM // 2
    inv_freq = ROPE_THETA ** (-jnp.arange(0, ROT_DIM, 2, dtype=F32) / ROT_DIM)
    ang = jnp.arange(seq, dtype=F32)[:, None] * inv_freq[None, :]
    cos, sin = jnp.cos(ang), jnp.sin(ang)
    ones = jnp.ones((seq, HEAD_DIM - ROT_DIM), F32)
    zeros = jnp.zeros((seq, HEAD_DIM - ROT_DIM), F32)
    zh = jnp.zeros((seq, half), F32)
    cos_h = jnp.concatenate([cos, cos, ones], axis=1)
    s1_h = jnp.concatenate([zh, sin, zeros], axis=1)
    s2_h = jnp.concatenate([-sin, zh, zeros], axis=1)
    rep = LANES // HEAD_DIM
    return tuple(jnp.tile(t, (1, rep)) for t in (cos_h, s1_h, s2_h))


def _in_projection(x2, w_in, seq):
    n, dm = x2.shape
    aw, sw = N_HEADS * HEAD_DIM, w_in.shape[1] - 3 * N_HEADS * HEAD_DIM - 2 * dm
    tm = ROW_TILE
    cos_t, s1_t, s2_t = _rotary_tables(seq)
    w_vt = w_in[:, 2 * aw:3 * aw].T
    hp = aw // LANES
    tiles_per_seq = seq // tm
    row = lambda i: (i, 0)
    tab = lambda i: (i % tiles_per_seq, 0)
    L = SSM_CHUNK
    step_rows = lambda i: (i, 0)
    out_shape = (
        jax.ShapeDtypeStruct((n, aw), BF16), jax.ShapeDtypeStruct((n, aw), BF16),
        jax.ShapeDtypeStruct((n // MOBA_BLOCK, hp, LANES, MOBA_BLOCK), BF16),
        jax.ShapeDtypeStruct((n // L, L * sw), BF16),
        jax.ShapeDtypeStruct((n, dm), BF16), jax.ShapeDtypeStruct((n, dm), BF16),
        jax.ShapeDtypeStruct((n // MOBA_BLOCK, 1, aw), F32),
    )
    return pl.pallas_call(
        functools.partial(_inproj_kernel, aw=aw, sw=sw, dm=dm, scale=HEAD_DIM ** -0.5 * math.log2(math.e)),
        grid=(n // tm,),
        in_specs=[
            pl.BlockSpec((tm, dm), row),
            pl.BlockSpec(w_in.shape, lambda i: (0, 0)),
            pl.BlockSpec(w_vt.shape, lambda i: (0, 0)),
            pl.BlockSpec((tm, LANES), tab), pl.BlockSpec((tm, LANES), tab), pl.BlockSpec((tm, LANES), tab),
        ],
        out_specs=(
            pl.BlockSpec((tm, aw), row), pl.BlockSpec((tm, aw), row),
            pl.BlockSpec((tm // MOBA_BLOCK, hp, LANES, MOBA_BLOCK), lambda i: (i, 0, 0, 0)),
            pl.BlockSpec((tm // L, L * sw), step_rows),
            pl.BlockSpec((tm, dm), row), pl.BlockSpec((tm, dm), row),
            pl.BlockSpec((tm // MOBA_BLOCK, 1, aw), lambda i: (i, 0, 0)),
        ),
        out_shape=out_shape,
        scratch_shapes=[pltpu.VMEM((sw // LANES, tm, LANES), F32)],
        compiler_params=pltpu.CompilerParams(dimension_semantics=("parallel",), vmem_limit_bytes=VMEM_LIMIT),
        name="in_projection",
    )(x2, w_in, w_vt, cos_t, s1_t, s2_t)


PAIR = 2 * MOBA_BLOCK
Q_TILE = PAIR
MASKED = 2.0 * NEG_INF
V_ROWS = HEAD_DIM + 16
SEL_ROWS = SUBLANES
SEL_CHUNK = 1024


def _head_split(q2):
    qf = q2.astype(F32)
    lane = lax.broadcasted_iota(jnp.int32, qf.shape, 1)
    return (jnp.where(lane < HEAD_DIM, qf, 0.0).astype(BF16), jnp.where(lane >= HEAD_DIM, qf, 0.0).astype(BF16))


def _select_kernel(q_ref, km_ref, sel_ref, *, nb):
    kmb = km_ref[0].astype(BF16)
    seq = q_ref.shape[1]
    shape = (nb, SEL_CHUNK)
    blk = lax.broadcasted_iota(jnp.int32, shape, 0)
    for c in range(seq // SEL_CHUNK):
        cols = slice(c * SEL_CHUNK, (c + 1) * SEL_CHUNK)
        own = (lax.broadcasted_iota(jnp.int32, shape, 1) + c * SEL_CHUNK) // MOBA_BLOCK
        qh = _head_split(q_ref[0, cols, :])
        for h in range(2):
            g = jnp.where(blk < own, _dot_nt(kmb, qh[h]), NEG_INF)
            rows = []
            for _ in range(min(MOBA_TOP_K, nb)):
                mx = jnp.max(g, axis=0, keepdims=True)
                first = jnp.min(jnp.where(g == mx, blk, nb), axis=0, keepdims=True)
                rows.append(jnp.where(first < own[0:1], first, -1))
                g = jnp.where(blk == first, -jnp.inf, g)
            rows.append(jnp.full((SEL_ROWS - len(rows), SEL_CHUNK), -1, jnp.int32))
            sel_ref[0, 0, h, :, cols] = jnp.concatenate(rows, axis=0)


def _moba_select(q, kmean):
    bsz, seq, aw = q.shape
    nb = seq // MOBA_BLOCK
    hp = aw // LANES
    assert seq % SEL_CHUNK == 0
    return pl.pallas_call(
        functools.partial(_select_kernel, nb=nb),
        grid=(bsz, hp),
        in_specs=[
            pl.BlockSpec((1, seq, LANES), lambda b, h: (b, 0, h)),
            pl.BlockSpec((1, nb, LANES), lambda b, h: (b, 0, h)),
        ],
        out_specs=pl.BlockSpec((1, 1, 2, SEL_ROWS, seq), lambda b, h: (b, h, 0, 0, 0)),
        out_shape=jax.ShapeDtypeStruct((bsz, hp, 2, SEL_ROWS, seq), jnp.int32),
        compiler_params=pltpu.CompilerParams(
            dimension_semantics=("parallel", "parallel"), vmem_limit_bytes=VMEM_LIMIT),
        name="moba_select",
    )(q, kmean)


def _attn_kernel(q_ref, k_ref, vt_ref, sel_ref, o_ref,
                 va_ref, vb_ref, s0_ref, s1_ref, p0_ref, p1_ref, cm_ref, al_ref, acc_ref, *, nb):
    i = pl.program_id(2)
    blk = MOBA_BLOCK
    v_refs = (va_ref, vb_ref)
    s_refs = (s0_ref, s1_ref)
    p_refs = (p0_ref, p1_ref)

    @pl.when(i == 0)
    def _():
        ones = jnp.ones((V_ROWS - HEAD_DIM, blk), BF16)

        def fill(j, c):
            for half in range(2):
                vt = vt_ref[0, 2 * j + half, 0]
                cols = slice(half * blk, (half + 1) * blk)
                for h in range(2):
                    v_refs[h][j, 0:HEAD_DIM, cols] = vt[h * HEAD_DIM:(h + 1) * HEAD_DIM]
                    v_refs[h][j, HEAD_DIM:V_ROWS, cols] = ones
            return c
        lax.fori_loop(0, nb // 2, fill, 0)

    qh = _head_split(q_ref[0])
    sel = (sel_ref[0, 0, 0], sel_ref[0, 0, 1])

    def chosen(h, n):
        sv = sel[h]
        hit = sv[0:1] == n
        for r in range(1, MOBA_TOP_K):
            hit = jnp.logical_or(hit, sv[r:r + 1] == n)
        return hit

    def past_bias(pair):
        return [tuple(jnp.where(chosen(h, 2 * pair + half), 0.0, MASKED) for half in range(2)) for h in range(2)]

    key_i = lax.broadcasted_iota(jnp.int32, (blk, blk), 0)
    qry_i = lax.broadcasted_iota(jnp.int32, (blk, blk), 1)
    causal = key_i <= qry_i

    def own_mask(s, h):
        top = jnp.concatenate([jnp.where(causal, s[:blk, :blk], MASKED),
                               jnp.where(chosen(h, 2 * i)[:, blk:], s[:blk, blk:], MASKED)], axis=1)
        bot = jnp.concatenate([jnp.full((blk, blk), MASKED, F32),
                               jnp.where(causal, s[blk:, blk:], MASKED)], axis=1)
        return jnp.concatenate([top, bot], axis=0)

    def stage_a(pair, slot, own):
        kp = k_ref[0, pl.ds(pl.multiple_of(pair * PAIR, PAIR), PAIR), :]
        for h in range(2):
            s = _dot_nt(kp, qh[h])
            if own:
                s = own_mask(s, h)
            s_refs[slot][h] = s
            cm_ref[slot, 2 * h:2 * h + 1, :] = jnp.max(s[:blk], axis=0, keepdims=True)
            cm_ref[slot, 2 * h + 1:2 * h + 2, :] = jnp.max(s[blk:], axis=0, keepdims=True)

    def stage_b(slot, m, bias):
        m_out = []
        for h in range(2):
            cm0 = cm_ref[slot, 2 * h:2 * h + 1, :]
            cm1 = cm_ref[slot, 2 * h + 1:2 * h + 2, :]
            if bias is None:
                m_new = jnp.maximum(m[h], jnp.maximum(cm0, cm1))
                c0 = c1 = m_new
            else:
                b0, b1 = bias[h]
                m_new = jnp.maximum(m[h], jnp.maximum(cm0 + b0, cm1 + b1))
                c0, c1 = m_new - b0, m_new - b1
            al_ref[slot, h:h + 1, :] = jnp.exp2(m[h] - m_new)
            p_refs[slot][h, 0:blk] = jnp.exp2(s_refs[slot][h, 0:blk] - c0).astype(BF16)
            p_refs[slot][h, blk:PAIR] = jnp.exp2(s_refs[slot][h, blk:PAIR] - c1).astype(BF16)
            m_out.append(m_new)
        return tuple(m_out)

    def stage_c(pair, slot):
        for h in range(2):
            pv = _dot(v_refs[h][pair], p_refs[slot][h])
            acc_ref[h] = acc_ref[h] * al_ref[slot, h:h + 1, :] + pv

    def trip(t, slot, m):
        stage_a(t - 1, slot, False)
        m = stage_b(1 - slot, m, past_bias(t - 2))
        stage_c(jnp.where(t == 2, i, t - 3), slot)
        return m

    acc_ref[...] = jnp.zeros_like(acc_ref)
    floor = jnp.full((1, Q_TILE), NEG_INF, F32)
    stage_a(i, 0, True)
    stage_a(0, 1, False)
    m0 = stage_b(0, (floor, floor), None)
    n_list = i + 1

    @pl.when(i == 0)
    def _():
        stage_c(i, 0)

    @pl.when(i > 0)
    def _():
        def double_trip(d, m):
            t = 2 + 2 * d
            return trip(t + 1, 1, trip(t, 0, m))

        m1 = lax.fori_loop(0, (n_list - 2) // 2, double_trip, m0)

        @pl.when(n_list % 2 == 1)
        def _():
            m2 = trip(n_list - 1, 0, m1)
            stage_b(0, m2, past_bias(n_list - 2))
            stage_c(n_list - 3, 1)
            stage_c(n_list - 2, 0)

        @pl.when(n_list % 2 == 0)
        def _():
            stage_b(1, m1, past_bias(n_list - 2))
            stage_c(jnp.where(n_list == 2, i, n_list - 3), 0)
            stage_c(n_list - 2, 1)

    outs = []
    for h in range(2):
        acc = acc_ref[h]
        outs.append(acc[:HEAD_DIM] / acc[HEAD_DIM:HEAD_DIM + 1])
    o_ref[0] = jnp.concatenate(outs, axis=0).T.astype(o_ref.dtype)


def _moba_attention(q, k, vt, sel):
    bsz, seq, aw = q.shape
    nb = seq // MOBA_BLOCK
    hp = aw // LANES
    assert nb % 2 == 0
    return pl.pallas_call(
        functools.partial(_attn_kernel, nb=nb),
        grid=(bsz, hp, nb // 2),
        in_specs=[
            pl.BlockSpec((1, Q_TILE, LANES), lambda b, h, i: (b, i, h)),
            pl.BlockSpec((1, seq, LANES), lambda b, h, i: (b, 0, h)),
            pl.BlockSpec((1, nb, 1, LANES, MOBA_BLOCK), lambda b, h, i: (b, 0, h, 0, 0)),
            pl.BlockSpec((1, 1, 2, SEL_ROWS, Q_TILE), lambda b, h, i: (b, h, 0, 0, i)),
        ],
        out_specs=pl.BlockSpec((1, Q_TILE, LANES), lambda b, h, i: (b, i, h)),
        out_shape=jax.ShapeDtypeStruct((bsz, seq, aw), BF16),
        scratch_shapes=[
            pltpu.VMEM((nb // 2, V_ROWS, PAIR), BF16), pltpu.VMEM((nb // 2, V_ROWS, PAIR), BF16),
            pltpu.VMEM((2, PAIR, Q_TILE), F32), pltpu.VMEM((2, PAIR, Q_TILE), F32),
            pltpu.VMEM((2, PAIR, Q_TILE), BF16), pltpu.VMEM((2, PAIR, Q_TILE), BF16),
            pltpu.VMEM((2, SUBLANES, Q_TILE), F32),
            pltpu.VMEM((2, SUBLANES, Q_TILE), F32),
            pltpu.VMEM((2, V_ROWS, Q_TILE), F32),
        ],
        compiler_params=pltpu.CompilerParams(
            dimension_semantics=("parallel", "parallel", "arbitrary"), vmem_limit_bytes=VMEM_LIMIT),
        name="moba_attention",
    )(q, k, vt, sel)


def _s5_tables(a_re, a_im, log_dt, b_re, b_im, c_re, c_im, d_skip):
    g, p = a_re.shape
    L, hch = SSM_CHUNK, SSM_GROUP
    w = L * hch
    two = lambda t: jnp.concatenate([t, t], axis=-1)
    operands = (
        log_dt.reshape(g, 1, 1),
        two(a_re).reshape(g, 1, 2 * p), two(a_im).reshape(g, 1, 2 * p),
        a_re.reshape(g, p, 1), a_im.reshape(g, p, 1),
        jnp.tile(b_re, (1, 1, L)), jnp.tile(b_im, (1, 1, L)),
        two(c_re), two(c_im),
        jnp.tile(d_skip, (1, L)).reshape(g, 1, w),
    )
    grp = lambda i: (i, 0, 0)
    out_shape = (
        jax.ShapeDtypeStruct((g, w, w), BF16), jax.ShapeDtypeStruct((g, 2 * p, w), BF16),
        jax.ShapeDtypeStruct((g, 2 * p, w), BF16), jax.ShapeDtypeStruct((g, w, 2 * p), BF16),
        jax.ShapeDtypeStruct((g, 2, 2 * p), F32),
    )
    return pl.pallas_call(
        _s5_tables_kernel,
        grid=(g,),
        in_specs=[pl.BlockSpec((1,) + t.shape[1:], grp) for t in operands],
        out_specs=tuple(pl.BlockSpec((1,) + t.shape[1:], grp) for t in out_shape),
        out_shape=out_shape,
        compiler_params=pltpu.CompilerParams(dimension_semantics=("parallel",), vmem_limit_bytes=VMEM_LIMIT),
        name="s5_tables",
    )(*operands)


def _s5_tables_kernel(ldt_ref, ar2_ref, ai2_ref, arc_ref, aic_ref, btr_ref, bti_ref, c2r_ref, c2i_ref, dskip_ref,
                      toept_ref, bpt_ref, bpst_ref, cpt_ref, lam_ref):
    L, hch, p = SSM_CHUNK, SSM_GROUP, SSM_STATE
    w = L * hch
    dt = jnp.exp(ldt_ref[0])

    ar, ai = ar2_ref[0] * dt, ai2_ref[0] * dt
    re_half = lax.broadcasted_iota(jnp.int32, (1, 2 * p), 1) < p

    def lam_pow_rows(e):
        mag, ang = jnp.exp(e * ar), e * ai
        return mag * jnp.cos(ang), mag * jnp.sin(ang)

    c_re = jnp.concatenate([c2r_ref[0]] * L, axis=0)
    c_im = jnp.concatenate([c2i_ref[0]] * L, axis=0)
    x_c = jnp.where(re_half, c_re, -c_im)
    y_c = jnp.where(re_half, -c_im, -c_re)
    step = lax.broadcasted_iota(jnp.int32, (L, 1), 0).astype(F32)

    def per_channel(t):
        return jnp.broadcast_to(t[:, None, :], (L, hch, 2 * p)).reshape(w, 2 * p)

    pr, pi = lam_pow_rows(step + 1.0)
    cpt_ref[0] = (x_c * per_channel(pr) + y_c * per_channel(pi)).astype(cpt_ref.dtype)
    pr, pi = lam_pow_rows(step)
    c_lag = x_c * per_channel(pr) + y_c * per_channel(pi)
    pr, pi = lam_pow_rows(jnp.full((1, 1), float(L), F32))
    lam_ref[0, 0:1, :] = pr
    lam_ref[0, 1:2, :] = jnp.where(re_half, -pi, pi)

    a_r, a_i = arc_ref[0], aic_ref[0]
    mag = jnp.exp(a_r * dt)
    n_re, n_im = mag * jnp.cos(a_i * dt) - 1.0, mag * jnp.sin(a_i * dt)
    den = a_r * a_r + a_i * a_i
    z_re = (n_re * a_r + n_im * a_i) / den
    z_im = (n_im * a_r - n_re * a_i) / den
    b_r, b_i = btr_ref[0], bti_ref[0]
    zb_re = z_re * b_r - z_im * b_i
    zb_im = z_re * b_i + z_im * b_r
    e = (L - 1 - lax.broadcasted_iota(jnp.int32, (1, L), 1)).astype(F32)
    mag, ang = jnp.exp(e * (a_r * dt)), e * (a_i * dt)
    spread = (lax.broadcasted_iota(jnp.int32, (L, w), 1) // hch
              == lax.broadcasted_iota(jnp.int32, (L, w), 0)).astype(F32)
    q_re = jnp.dot(mag * jnp.cos(ang), spread, preferred_element_type=F32, precision=lax.Precision.HIGHEST)
    q_im = jnp.dot(mag * jnp.sin(ang), spread, preferred_element_type=F32, precision=lax.Precision.HIGHEST)
    w_re = q_re * zb_re - q_im * zb_im
    w_im = q_re * zb_im + q_im * zb_re
    bpt_ref[0] = jnp.concatenate([w_re, w_im], axis=0).astype(bpt_ref.dtype)
    bpst_ref[0] = jnp.concatenate([w_im, w_re], axis=0).astype(bpst_ref.dtype)

    k_lag = jnp.dot(c_lag, jnp.concatenate([zb_re, zb_im], axis=0),
                    preferred_element_type=F32, precision=lax.Precision.HIGHEST)
    lane_s = lax.broadcasted_iota(jnp.int32, (w, w), 1) // hch
    toep = jnp.zeros((w, w), F32)
    for s in range(L):
        shifted = k_lag if s == 0 else jnp.concatenate([jnp.zeros((s * hch, w), F32), k_lag[:w - s * hch]], axis=0)
        toep = jnp.where(lane_s == s, shifted, toep)
    diag = lax.broadcasted_iota(jnp.int32, (w, w), 0) == lax.broadcasted_iota(jnp.int32, (w, w), 1)
    toept_ref[0] = (toep + jnp.where(diag, dskip_ref[0], 0.0)).astype(toept_ref.dtype)


def _s5_input_kernel(us_ref, ut_ref):
    ut = us_ref[...].astype(F32).T
    ut_ref[...] = ut.reshape(ut_ref.shape).astype(ut_ref.dtype)


def _s5_input(us, n_groups, bsz, seq):
    L = SSM_CHUNK
    nc = seq // L
    sw = us.shape[1] // L
    assert nc % LANES == 0
    return pl.pallas_call(
        _s5_input_kernel,
        grid=(bsz, L),
        in_specs=[pl.BlockSpec((nc, sw), lambda b, s: (b, s))],
        out_specs=pl.BlockSpec((n_groups, SSM_GROUP, nc), lambda b, s: (0, s, b)),
        out_shape=jax.ShapeDtypeStruct((n_groups, L * SSM_GROUP, bsz * nc), BF16),
        compiler_params=pltpu.CompilerParams(
            dimension_semantics=("parallel", "parallel"), vmem_limit_bytes=VMEM_LIMIT),
        name="s5_input",
    )(us)


def _s5_kernel(ut_ref, toept_ref, bpt_ref, bpst_ref, cpt_ref, lam_ref, yt_ref, s_ref, ss_ref, hp_ref, *, bsz, nc):
    ut = ut_ref[0]
    s_ref[...] = _dot(bpt_ref[0], ut).T
    ss_ref[...] = _dot(bpst_ref[0], ut).T
    a = lam_ref[0, 0:1, :]
    bv = lam_ref[0, 1:2, :]

    def body(c, carry):
        h, hs = carry
        rows = pl.ds(c, bsz, stride=nc)
        hp_ref[rows, :] = h
        return a * h + bv * hs + s_ref[rows, :], a * hs - bv * h + ss_ref[rows, :]

    zero = jnp.zeros((bsz, 2 * SSM_STATE), F32)
    lax.fori_loop(0, nc, body, (zero, zero), unroll=8)
    yt_ref[0] = _dot(toept_ref[0], ut) + _dot_nt(cpt_ref[0], hp_ref[...].astype(BF16))


def _s5_scan(ut, tables, bsz, seq):
    toept, bpt, bpst, cpt, lam = tables
    g, w, cols = ut.shape
    nc = seq // SSM_CHUNK
    p2 = 2 * SSM_STATE
    grp = lambda i: (i, 0, 0)
    return pl.pallas_call(
        functools.partial(_s5_kernel, bsz=bsz, nc=nc),
        grid=(g,),
        in_specs=[
            pl.BlockSpec((1, w, cols), grp), pl.BlockSpec((1, w, w), grp),
            pl.BlockSpec((1, p2, w), grp), pl.BlockSpec((1, p2, w), grp),
            pl.BlockSpec((1, w, p2), grp), pl.BlockSpec((1, 2, p2), grp),
        ],
        out_specs=pl.BlockSpec((1, w, cols), grp),
        out_shape=jax.ShapeDtypeStruct((g, w, cols), F32),
        scratch_shapes=[pltpu.VMEM((cols, p2), F32), pltpu.VMEM((cols, p2), F32), pltpu.VMEM((cols, p2), F32)],
        compiler_params=pltpu.CompilerParams(dimension_semantics=("parallel",), vmem_limit_bytes=VMEM_LIMIT),
        name="s5_scan",
    )(ut, toept, bpt, bpst, cpt, lam)


def _s5_post_kernel(yt_ref, wglut_ref, ss_ref):
    gh, nc = yt_ref.shape[0] * yt_ref.shape[1], yt_ref.shape[2]
    gt = jax.nn.gelu(yt_ref[...].reshape(gh, nc))
    ssm_t = gt * jax.nn.sigmoid(_dot(wglut_ref[...], gt.astype(BF16)))
    ss_ref[...] = ssm_t.T.astype(ss_ref.dtype)


def _s5_post(yt, w_glu_t, bsz, seq):
    g = yt.shape[0]
    L = SSM_CHUNK
    nc = seq // L
    sw = g * SSM_GROUP
    return pl.pallas_call(
        _s5_post_kernel,
        grid=(bsz, L),
        in_specs=[
            pl.BlockSpec((g, SSM_GROUP, nc), lambda b, s: (0, s, b)),
            pl.BlockSpec(w_glu_t.shape, lambda b, s: (0, 0)),
        ],
        out_specs=pl.BlockSpec((nc, sw), lambda b, s: (b, s)),
        out_shape=jax.ShapeDtypeStruct((bsz * nc, L * sw), BF16),
        compiler_params=pltpu.CompilerParams(
            dimension_semantics=("parallel", "parallel"), vmem_limit_bytes=VMEM_LIMIT),
        name="s5_post",
    )(yt, w_glu_t)


def _merge_kernel(x_ref, attn_ref, ss_ref, ga_ref, gs_ref, wap_ref, wsp_ref, wout_ref, g_ref, b_ref,
                  h_ref, slab_ref, *, alpha):
    ssm = _from_step_layout(ss_ref, slab_ref)
    merged = (ga_ref[...].astype(F32) * _dot(attn_ref[...], wap_ref[...])
              + gs_ref[...].astype(F32) * _dot(ssm.astype(BF16), wsp_ref[...]))
    mix = _dot(merged.astype(BF16), wout_ref[...])
    h_ref[...] = _layer_norm(alpha * x_ref[...] + mix, g_ref[...], b_ref[...])


def _merge(x2, attn, ss, ga, gs, w_ap, w_sp, w_out, ln_g, ln_b, alpha):
    n, dm = x2.shape
    aw = attn.shape[1]
    L = SSM_CHUNK
    sw = ss.shape[1] // L
    tm = ROW_TILE
    row = lambda i: (i, 0)
    full = lambda i: (0, 0)
    return pl.pallas_call(
        functools.partial(_merge_kernel, alpha=alpha),
        grid=(n // tm,),
        in_specs=[
            pl.BlockSpec((tm, dm), row), pl.BlockSpec((tm, aw), row), pl.BlockSpec((tm // L, L * sw), row),
            pl.BlockSpec((tm, dm), row), pl.BlockSpec((tm, dm), row),
            pl.BlockSpec(w_ap.shape, full), pl.BlockSpec(w_sp.shape, full),
            pl.BlockSpec(w_out.shape, full), pl.BlockSpec((1, dm), full), pl.BlockSpec((1, dm), full),
        ],
        out_specs=pl.BlockSpec((tm, dm), row),
        out_shape=jax.ShapeDtypeStruct((n, dm), F32),
        scratch_shapes=[pltpu.VMEM((sw // LANES, tm, LANES), F32)],
        compiler_params=pltpu.CompilerParams(dimension_semantics=("parallel",), vmem_limit_bytes=VMEM_LIMIT),
        name="merge_ln1",
    )(x2, attn, ss, ga, gs, w_ap, w_sp, w_out, ln_g, ln_b)


def _ffn_kernel(h_ref, wup_ref, cw_ref, wd_ref, g_ref, b_ref, o_ref, halo_ref, act_ref,
                *, alpha, tiles_per_seq, dff):
    i = pl.program_id(0)
    h = h_ref[...]
    hb = h.astype(BF16)
    tm = h.shape[0]
    ck = FF_CHUNK
    row8 = lax.broadcasted_iota(jnp.int32, (SUBLANES, ck), 0)

    @pl.when((i % tiles_per_seq) == 0)
    def _():
        halo_ref[...] = jnp.zeros_like(halo_ref)

    def conv(up, c0):
        cols = slice(c0, c0 + ck)
        prev = halo_ref[:, cols]
        halo_ref[:, cols] = up[tm - SUBLANES:]
        p1 = prev[SUBLANES - 1:SUBLANES]
        p2 = prev[SUBLANES - 2:SUBLANES - 1]
        r1 = pltpu.roll(up, 1, 0)
        r2 = pltpu.roll(up, 2, 0)
        d1 = jnp.concatenate([jnp.where(row8 == 0, p1, r1[:SUBLANES]), r1[SUBLANES:]], axis=0)
        d2 = jnp.concatenate([jnp.where(row8 == 0, p2, jnp.where(row8 == 1, p1, r2[:SUBLANES])), r2[SUBLANES:]],
                             axis=0)
        cw = cw_ref[:, cols]
        return cw[3:4] + cw[0:1] * d2 + cw[1:2] * d1 + cw[2:3] * up

    for j in range(dff // ck):
        c0 = j * ck
        val = conv(_dot(hb, wup_ref[:, c0:c0 + ck]), c0)
        gate = conv(_dot(hb, wup_ref[:, dff + c0:dff + c0 + ck]), dff + c0)
        act_ref[:, c0:c0 + ck] = (jax.nn.gelu(gate) * val).astype(BF16)
    ff = _dot(act_ref[...], wd_ref[...])
    o_ref[...] = _layer_norm(alpha * h + ff, g_ref[...], b_ref[...])


def _ffn(h, w_up, conv_w, conv_b, w_down, ln_g, ln_b, alpha, seq):
    n, dm = h.shape
    dff = w_down.shape[0]
    tm = FFN_TILE
    assert dff % FF_CHUNK == 0 and seq % tm == 0
    cw = jnp.concatenate([conv_w, conv_b[None, :]], axis=0)
    cw = jnp.pad(cw, ((0, SUBLANES - cw.shape[0]), (0, 0)))
    row = lambda i: (i, 0)
    full = lambda i: (0, 0)
    once = pl.Buffered(1)
    return pl.pallas_call(
        functools.partial(_ffn_kernel, alpha=alpha, tiles_per_seq=seq // tm, dff=dff),
        grid=(n // tm,),
        in_specs=[
            pl.BlockSpec((tm, dm), row),
            pl.BlockSpec(w_up.shape, full, pipeline_mode=once),
            pl.BlockSpec(cw.shape, full),
            pl.BlockSpec(w_down.shape, full, pipeline_mode=once),
            pl.BlockSpec((1, dm), full), pl.BlockSpec((1, dm), full),
        ],
        out_specs=pl.BlockSpec((tm, dm), row),
        out_shape=jax.ShapeDtypeStruct((n, dm), F32),
        scratch_shapes=[
            pltpu.VMEM((SUBLANES, 2 * dff), F32),
            pltpu.VMEM((tm, dff), BF16),
        ],
        compiler_params=pltpu.CompilerParams(dimension_semantics=("arbitrary",), vmem_limit_bytes=VMEM_LIMIT),
        name="conv_ffn_ln2",
    )(h, w_up, cw, w_down, ln_g, ln_b)


def kernel(x, w_in, w_attn_proj, ssm_a_re, ssm_a_im, ssm_log_dt, ssm_b_re, ssm_b_im, ssm_c_re, ssm_c_im, ssm_d,
           w_glu, w_ssm_proj, w_out, ln1_g, ln1_b, w_up, conv_w, conv_b, w_down, ln2_g, ln2_b):
    bsz, seq, dm = x.shape
    depth = w_in.shape[0]
    alpha = (2.0 * depth) ** 0.25
    n = bsz * seq
    assert seq % ROW_TILE == 0 and ROW_TILE % MOBA_BLOCK == 0 and seq % SSM_CHUNK == 0
    h = x
    for l in range(depth):
        w_in_b = w_in[l].astype(BF16)
        h2 = h.reshape(n, dm)
        q, k, vt, us, ga, gs, kmean = _in_projection(h2, w_in[l].astype(BF16), seq)
        aw = q.shape[1]
        nb = seq // MOBA_BLOCK
        q3 = q.reshape(bsz, seq, aw)
        sel = _moba_select(q3, kmean.reshape(bsz, nb, aw))
        attn = _moba_attention(q3, k.reshape(bsz, seq, aw), vt.reshape(bsz, nb, *vt.shape[1:]), sel).reshape(n, aw)
        tables = _s5_tables(ssm_a_re[l], ssm_a_im[l], ssm_log_dt[l], ssm_b_re[l], ssm_b_im[l],
                            ssm_c_re[l], ssm_c_im[l], ssm_d[l])
        ut = _s5_input(us, ssm_a_re.shape[1], bsz, seq)
        ss = _s5_post(_s5_scan(ut, tables, bsz, seq), w_glu[l].T.astype(BF16), bsz, seq)
        h1 = _merge(h2, attn, ss, ga, gs, w_attn_proj[l].astype(BF16), w_ssm_proj[l].astype(BF16),
                    w_out[l].astype(BF16), ln1_g[l][None], ln1_b[l][None], alpha)
        h = _ffn(h1, w_up[l].astype(BF16), conv_w[l], conv_b[l], w_down[l].astype(BF16),
                 ln2_g[l][None], ln2_b[l][None], alpha, seq).reshape(bsz, seq, dm)
    return h
```

```python
import functools
import math

import jax
import jax.numpy as jnp
from jax import lax
from jax.experimental import pallas as pl
from jax.experimental.pallas import tpu as pltpu

F32 = jnp.float32
BF16 = jnp.bfloat16

N_HEADS = 8
HEAD_DIM = 64
ROT_DIM = HEAD_DIM // 4
ROPE_THETA = 500000.0
MOBA_BLOCK = 256
MOBA_TOP_K = 3
SSM_GROUP = 16
SSM_STATE = 64
CONV_WIDTH = 3
LN_EPS = 1e-5
NEG_INF = -1e30

LANES = 128
SUBLANES = 8
VMEM_LIMIT = 56 * 1024 * 1024

SSM_CHUNK = 16
S5_GROUPS_PER_STEP = 4
ROW_TILE = 512
FFN_TILE = 1024
FF_CHUNK = 256

_NT = (((1,), (1,)), ((), ()))


def _dot(a, b):
    return jnp.dot(a, b, preferred_element_type=F32)


def _dot_nt(a, b):
    return lax.dot_general(a, b, _NT, preferred_element_type=F32)


def _layer_norm(t, g, b):
    mu = jnp.mean(t, axis=-1, keepdims=True)
    d = t - mu
    var = jnp.mean(d * d, axis=-1, keepdims=True)
    return d * lax.rsqrt(var + LN_EPS) * g + b


def _to_step_layout(dst_ref, src, slab_ref):
    chunks, width = dst_ref.shape[0], src.shape[1]
    for j in range(width // LANES):
        slab_ref[j] = src[:, j * LANES:(j + 1) * LANES]
    for s in range(SSM_CHUNK):
        for j in range(width // LANES):
            c0 = s * width + j * LANES
            dst_ref[:, c0:c0 + LANES] = slab_ref[j, pl.ds(s, chunks, stride=SSM_CHUNK), :].astype(dst_ref.dtype)


def _from_step_layout(src_ref, slab_ref):
    chunks, width = src_ref.shape[0], slab_ref.shape[0] * LANES
    for s in range(SSM_CHUNK):
        for j in range(width // LANES):
            c0 = s * width + j * LANES
            slab_ref[j, pl.ds(s, chunks, stride=SSM_CHUNK), :] = src_ref[:, c0:c0 + LANES].astype(slab_ref.dtype)
    return jnp.concatenate([slab_ref[j] for j in range(width // LANES)], axis=1)


def _inproj_kernel(x_ref, w_ref, wvt_ref, cos_ref, s1_ref, s2_ref,
                   q_ref, k_ref, vt_ref, us_ref, ga_ref, gs_ref, km_ref, tmp_ref, *, aw, sw, dm, scale):
    xb = x_ref[...].astype(BF16)
    cos_t, s1_t, s2_t = cos_ref[...], s1_ref[...], s2_ref[...]
    tm = xb.shape[0]

    def rotary(t):
        return t * cos_t + pltpu.roll(t, ROT_DIM // 2, 1) * s1_t + pltpu.roll(t, LANES - ROT_DIM // 2, 1) * s2_t

    q = _dot(xb, w_ref[:, 0:aw])
    for j in range(aw // LANES):
        sl = slice(j * LANES, (j + 1) * LANES)
        q_ref[:, sl] = (rotary(q[:, sl]) * scale).astype(BF16)
    k = _dot(xb, w_ref[:, aw:2 * aw])
    for j in range(aw // LANES):
        sl = slice(j * LANES, (j + 1) * LANES)
        kr = rotary(k[:, sl])
        k_ref[:, sl] = kr.astype(BF16)
        for r in range(tm // MOBA_BLOCK):
            km_ref[r, :, sl] = jnp.mean(kr[r * MOBA_BLOCK:(r + 1) * MOBA_BLOCK], axis=0, keepdims=True)
    vt = _dot_nt(wvt_ref[...], xb)
    for r in range(tm // MOBA_BLOCK):
        for j in range(aw // LANES):
            vt_ref[r, j] = vt[j * LANES:(j + 1) * LANES, r * MOBA_BLOCK:(r + 1) * MOBA_BLOCK].astype(BF16)
    _to_step_layout(us_ref, _dot(xb, w_ref[:, 3 * aw:3 * aw + sw]), tmp_ref)
    c0 = 3 * aw + sw
    ga_ref[...] = jax.nn.sigmoid(_dot(xb, w_ref[:, c0:c0 + dm])).astype(BF16)
    gs_ref[...] = jax.nn.sigmoid(_dot(xb, w_ref[:, c0 + dm:c0 + 2 * dm])).astype(BF16)


def _rotary_tables(seq):
    half = ROT_DIM // 2
    inv_freq = ROPE_THETA ** (-jnp.arange(0, ROT_DIM, 2, dtype=F32) / ROT_DIM)
    ang = jnp.arange(seq, dtype=F32)[:, None] * inv_freq[None, :]
    cos, sin = jnp.cos(ang), jnp.sin(ang)
    ones = jnp.ones((seq, HEAD_DIM - ROT_DIM), F32)
    zeros = jnp.zeros((seq, HEAD_DIM - ROT_DIM), F32)
    zh = jnp.zeros((seq, half), F32)
    cos_h = jnp.concatenate([cos, cos, ones], axis=1)
    s1_h = jnp.concatenate([zh, sin, zeros], axis=1)
    s2_h = jnp.concatenate([-sin, zh, zeros], axis=1)
    rep = LANES // HEAD_DIM
    return tuple(jnp.tile(t, (1, rep)) for t in (cos_h, s1_h, s2_h))


def _in_projection(x2, w_in, seq):
    n, dm = x2.shape
    aw, sw = N_HEADS * HEAD_DIM, w_in.shape[1] - 3 * N_HEADS * HEAD_DIM - 2 * dm
    tm = ROW_TILE
    cos_t, s1_t, s2_t = _rotary_tables(seq)
    w_vt = w_in[:, 2 * aw:3 * aw].T
    hp = aw // LANES
    tiles_per_seq = seq // tm
    row = lambda i: (i, 0)
    tab = lambda i: (i % tiles_per_seq, 0)
    L = SSM_CHUNK
    step_rows = lambda i: (i, 0)
    out_shape = (
        jax.ShapeDtypeStruct((n, aw), BF16), jax.ShapeDtypeStruct((n, aw), BF16),
        jax.ShapeDtypeStruct((n // MOBA_BLOCK, hp, LANES, MOBA_BLOCK), BF16),
        jax.ShapeDtypeStruct((n // L, L * sw), BF16),
        jax.ShapeDtypeStruct((n, dm), BF16), jax.ShapeDtypeStruct((n, dm), BF16),
        jax.ShapeDtypeStruct((n // MOBA_BLOCK, 1, aw), F32),
    )
    return pl.pallas_call(
        functools.partial(_inproj_kernel, aw=aw, sw=sw, dm=dm, scale=HEAD_DIM ** -0.5 * math.log2(math.e)),
        grid=(n // tm,),
        in_specs=[
            pl.BlockSpec((tm, dm), row),
            pl.BlockSpec(w_in.shape, lambda i: (0, 0)),
            pl.BlockSpec(w_vt.shape, lambda i: (0, 0)),
            pl.BlockSpec((tm, LANES), tab), pl.BlockSpec((tm, LANES), tab), pl.BlockSpec((tm, LANES), tab),
        ],
        out_specs=(
            pl.BlockSpec((tm, aw), row), pl.BlockSpec((tm, aw), row),
            pl.BlockSpec((tm // MOBA_BLOCK, hp, LANES, MOBA_BLOCK), lambda i: (i, 0, 0, 0)),
            pl.BlockSpec((tm // L, L * sw), step_rows),
            pl.BlockSpec((tm, dm), row), pl.BlockSpec((tm, dm), row),
            pl.BlockSpec((tm // MOBA_BLOCK, 1, aw), lambda i: (i, 0, 0)),
        ),
        out_shape=out_shape,
        scratch_shapes=[pltpu.VMEM((sw // LANES, tm, LANES), F32)],
        compiler_params=pltpu.CompilerParams(dimension_semantics=("parallel",), vmem_limit_bytes=VMEM_LIMIT),
        name="in_projection",
    )(x2, w_in, w_vt, cos_t, s1_t, s2_t)


PAIR = 2 * MOBA_BLOCK
Q_TILE = PAIR
MASKED = 2.0 * NEG_INF
V_ROWS = HEAD_DIM + 16
SEL_ROWS = SUBLANES
SEL_CHUNK = 1024


def _head_split(q2):
    qf = q2.astype(F32)
    lane = lax.broadcasted_iota(jnp.int32, qf.shape, 1)
    return (jnp.where(lane < HEAD_DIM, qf, 0.0).astype(BF16), jnp.where(lane >= HEAD_DIM, qf, 0.0).astype(BF16))


def _select_kernel(q_ref, km_ref, sel_ref, *, nb):
    kmb = km_ref[0].astype(BF16)
    seq = q_ref.shape[1]
    shape = (nb, SEL_CHUNK)
    blk = lax.broadcasted_iota(jnp.int32, shape, 0)
    for c in range(seq // SEL_CHUNK):
        cols = slice(c * SEL_CHUNK, (c + 1) * SEL_CHUNK)
        own = (lax.broadcasted_iota(jnp.int32, shape, 1) + c * SEL_CHUNK) // MOBA_BLOCK
        qh = _head_split(q_ref[0, cols, :])
        for h in range(2):
            g = jnp.where(blk < own, _dot_nt(kmb, qh[h]), NEG_INF)
            rows = []
            for _ in range(min(MOBA_TOP_K, nb)):
                mx = jnp.max(g, axis=0, keepdims=True)
                first = jnp.min(jnp.where(g == mx, blk, nb), axis=0, keepdims=True)
                rows.append(jnp.where(first < own[0:1], first, -1))
                g = jnp.where(blk == first, -jnp.inf, g)
            rows.append(jnp.full((SEL_ROWS - len(rows), SEL_CHUNK), -1, jnp.int32))
            sel_ref[0, 0, h, :, cols] = jnp.concatenate(rows, axis=0)


def _moba_select(q, kmean):
    bsz, seq, aw = q.shape
    nb = seq // MOBA_BLOCK
    hp = aw // LANES
    assert seq % SEL_CHUNK == 0
    return pl.pallas_call(
        functools.partial(_select_kernel, nb=nb),
        grid=(bsz, hp),
        in_specs=[
            pl.BlockSpec((1, seq, LANES), lambda b, h: (b, 0, h)),
            pl.BlockSpec((1, nb, LANES), lambda b, h: (b, 0, h)),
        ],
        out_specs=pl.BlockSpec((1, 1, 2, SEL_ROWS, seq), lambda b, h: (b, h, 0, 0, 0)),
        out_shape=jax.ShapeDtypeStruct((bsz, hp, 2, SEL_ROWS, seq), jnp.int32),
        compiler_params=pltpu.CompilerParams(
            dimension_semantics=("parallel", "parallel"), vmem_limit_bytes=VMEM_LIMIT),
        name="moba_select",
    )(q, kmean)


def _attn_kernel(q_ref, k_ref, vt_ref, sel_ref, o_ref,
                 va_ref, vb_ref, s0_ref, s1_ref, p0_ref, p1_ref, cm_ref, al_ref, acc_ref, *, nb):
    i = pl.program_id(2)
    blk = MOBA_BLOCK
    v_refs = (va_ref, vb_ref)
    s_refs = (s0_ref, s1_ref)
    p_refs = (p0_ref, p1_ref)

    @pl.when(i == 0)
    def _():
        ones = jnp.ones((V_ROWS - HEAD_DIM, blk), BF16)

        def fill(j, c):
            for half in range(2):
                vt = vt_ref[0, 2 * j + half, 0]
                cols = slice(half * blk, (half + 1) * blk)
                for h in range(2):
                    v_refs[h][j, 0:HEAD_DIM, cols] = vt[h * HEAD_DIM:(h + 1) * HEAD_DIM]
                    v_refs[h][j, HEAD_DIM:V_ROWS, cols] = ones
            return c
        lax.fori_loop(0, nb // 2, fill, 0)

    qh = _head_split(q_ref[0])
    sel = (sel_ref[0, 0, 0], sel_ref[0, 0, 1])

    def chosen(h, n):
        sv = sel[h]
        hit = sv[0:1] == n
        for r in range(1, MOBA_TOP_K):
            hit = jnp.logical_or(hit, sv[r:r + 1] == n)
        return hit

    def past_bias(pair):
        return [tuple(jnp.where(chosen(h, 2 * pair + half), 0.0, MASKED) for half in range(2)) for h in range(2)]

    key_i = lax.broadcasted_iota(jnp.int32, (blk, blk), 0)
    qry_i = lax.broadcasted_iota(jnp.int32, (blk, blk), 1)
    causal = key_i <= qry_i

    def own_mask(s, h):
        top = jnp.concatenate([jnp.where(causal, s[:blk, :blk], MASKED),
                               jnp.where(chosen(h, 2 * i)[:, blk:], s[:blk, blk:], MASKED)], axis=1)
        bot = jnp.concatenate([jnp.full((blk, blk), MASKED, F32),
                               jnp.where(causal, s[blk:, blk:], MASKED)], axis=1)
        return jnp.concatenate([top, bot], axis=0)

    def stage_a(pair, slot, own):
        kp = k_ref[0, pl.ds(pl.multiple_of(pair * PAIR, PAIR), PAIR), :]
        for h in range(2):
            s = _dot_nt(kp, qh[h])
            if own:
                s = own_mask(s, h)
            s_refs[slot][h] = s
            cm_ref[slot, 2 * h:2 * h + 1, :] = jnp.max(s[:blk], axis=0, keepdims=True)
            cm_ref[slot, 2 * h + 1:2 * h + 2, :] = jnp.max(s[blk:], axis=0, keepdims=True)

    def stage_b(slot, m, bias):
        m_out = []
        for h in range(2):
            cm0 = cm_ref[slot, 2 * h:2 * h + 1, :]
            cm1 = cm_ref[slot, 2 * h + 1:2 * h + 2, :]
            if bias is None:
                m_new = jnp.maximum(m[h], jnp.maximum(cm0, cm1))
                c0 = c1 = m_new
            else:
                b0, b1 = bias[h]
                m_new = jnp.maximum(m[h], jnp.maximum(cm0 + b0, cm1 + b1))
                c0, c1 = m_new - b0, m_new - b1
            al_ref[slot, h:h + 1, :] = jnp.exp2(m[h] - m_new)
            p_refs[slot][h, 0:blk] = jnp.exp2(s_refs[slot][h, 0:blk] - c0).astype(BF16)
            p_refs[slot][h, blk:PAIR] = jnp.exp2(s_refs[slot][h, blk:PAIR] - c1).astype(BF16)
            m_out.append(m_new)
        return tuple(m_out)

    def stage_c(pair, slot):
        for h in range(2):
            pv = _dot(v_refs[h][pair], p_refs[slot][h])
            acc_ref[h] = acc_ref[h] * al_ref[slot, h:h + 1, :] + pv

    def trip(t, slot, m):
        stage_a(t - 1, slot, False)
        m = stage_b(1 - slot, m, past_bias(t - 2))
        stage_c(jnp.where(t == 2, i, t - 3), slot)
        return m

    acc_ref[...] = jnp.zeros_like(acc_ref)
    floor = jnp.full((1, Q_TILE), NEG_INF, F32)
    stage_a(i, 0, True)
    stage_a(0, 1, False)
    m0 = stage_b(0, (floor, floor), None)
    n_list = i + 1

    @pl.when(i == 0)
    def _():
        stage_c(i, 0)

    @pl.when(i > 0)
    def _():
        def double_trip(d, m):
            t = 2 + 2 * d
            return trip(t + 1, 1, trip(t, 0, m))

        m1 = lax.fori_loop(0, (n_list - 2) // 2, double_trip, m0)

        @pl.when(n_list % 2 == 1)
        def _():
            m2 = trip(n_list - 1, 0, m1)
            stage_b(0, m2, past_bias(n_list - 2))
            stage_c(n_list - 3, 1)
            stage_c(n_list - 2, 0)

        @pl.when(n_list % 2 == 0)
        def _():
            stage_b(1, m1, past_bias(n_list - 2))
            stage_c(jnp.where(n_list == 2, i, n_list - 3), 0)
            stage_c(n_list - 2, 1)

    outs = []
    for h in range(2):
        acc = acc_ref[h]
        outs.append(acc[:HEAD_DIM] / acc[HEAD_DIM:HEAD_DIM + 1])
    o_ref[0] = jnp.concatenate(outs, axis=0).T.astype(o_ref.dtype)


def _moba_attention(q, k, vt, sel):
    bsz, seq, aw = q.shape
    nb = seq // MOBA_BLOCK
    hp = aw // LANES
    assert nb % 2 == 0
    return pl.pallas_call(
        functools.partial(_attn_kernel, nb=nb),
        grid=(bsz, hp, nb // 2),
        in_specs=[
            pl.BlockSpec((1, Q_TILE, LANES), lambda b, h, i: (b, i, h)),
            pl.BlockSpec((1, seq, LANES), lambda b, h, i: (b, 0, h)),
            pl.BlockSpec((1, nb, 1, LANES, MOBA_BLOCK), lambda b, h, i: (b, 0, h, 0, 0)),
            pl.BlockSpec((1, 1, 2, SEL_ROWS, Q_TILE), lambda b, h, i: (b, h, 0, 0, i)),
        ],
        out_specs=pl.BlockSpec((1, Q_TILE, LANES), lambda b, h, i: (b, i, h)),
        out_shape=jax.ShapeDtypeStruct((bsz, seq, aw), BF16),
        scratch_shapes=[
            pltpu.VMEM((nb // 2, V_ROWS, PAIR), BF16), pltpu.VMEM((nb // 2, V_ROWS, PAIR), BF16),
            pltpu.VMEM((2, PAIR, Q_TILE), F32), pltpu.VMEM((2, PAIR, Q_TILE), F32),
            pltpu.VMEM((2, PAIR, Q_TILE), BF16), pltpu.VMEM((2, PAIR, Q_TILE), BF16),
            pltpu.VMEM((2, SUBLANES, Q_TILE), F32),
            pltpu.VMEM((2, SUBLANES, Q_TILE), F32),
            pltpu.VMEM((2, V_ROWS, Q_TILE), F32),
        ],
        compiler_params=pltpu.CompilerParams(
            dimension_semantics=("parallel", "parallel", "arbitrary"), vmem_limit_bytes=VMEM_LIMIT),
        name="moba_attention",
    )(q, k, vt, sel)


def _s5_tables(a_re, a_im, log_dt, b_re, b_im, c_re, c_im, d_skip):
    g, p = a_re.shape
    L, hch = SSM_CHUNK, SSM_GROUP
    w = L * hch
    two = lambda t: jnp.concatenate([t, t], axis=-1)
    operands = (
        log_dt.reshape(g, 1, 1),
        two(a_re).reshape(g, 1, 2 * p), two(a_im).reshape(g, 1, 2 * p),
        a_re.reshape(g, p, 1), a_im.reshape(g, p, 1),
        jnp.tile(b_re, (1, 1, L)), jnp.tile(b_im, (1, 1, L)),
        two(c_re), two(c_im),
        jnp.tile(d_skip, (1, L)).reshape(g, 1, w),
    )
    grp = lambda i: (i, 0, 0)
    out_shape = (
        jax.ShapeDtypeStruct((g, w, w), BF16), jax.ShapeDtypeStruct((g, 2 * p, w), BF16),
        jax.ShapeDtypeStruct((g, 2 * p, w), BF16), jax.ShapeDtypeStruct((g, w, 2 * p), BF16),
        jax.ShapeDtypeStruct((g, 2, 2 * p), F32),
    )
    return pl.pallas_call(
        _s5_tables_kernel,
        grid=(g,),
        in_specs=[pl.BlockSpec((1,) + t.shape[1:], grp) for t in operands],
        out_specs=tuple(pl.BlockSpec((1,) + t.shape[1:], grp) for t in out_shape),
        out_shape=out_shape,
        compiler_params=pltpu.CompilerParams(dimension_semantics=("parallel",), vmem_limit_bytes=VMEM_LIMIT),
        name="s5_tables",
    )(*operands)


def _s5_tables_kernel(ldt_ref, ar2_ref, ai2_ref, arc_ref, aic_ref, btr_ref, bti_ref, c2r_ref, c2i_ref, dskip_ref,
                      toept_ref, bpt_ref, bpst_ref, cpt_ref, lam_ref):
    L, hch, p = SSM_CHUNK, SSM_GROUP, SSM_STATE
    w = L * hch
    dt = jnp.exp(ldt_ref[0])

    ar, ai = ar2_ref[0] * dt, ai2_ref[0] * dt
    re_half = lax.broadcasted_iota(jnp.int32, (1, 2 * p), 1) < p

    def lam_pow_rows(e):
        mag, ang = jnp.exp(e * ar), e * ai
        return mag * jnp.cos(ang), mag * jnp.sin(ang)

    c_re = jnp.concatenate([c2r_ref[0]] * L, axis=0)
    c_im = jnp.concatenate([c2i_ref[0]] * L, axis=0)
    x_c = jnp.where(re_half, c_re, -c_im)
    y_c = jnp.where(re_half, -c_im, -c_re)
    step = lax.broadcasted_iota(jnp.int32, (L, 1), 0).astype(F32)

    def per_channel(t):
        return jnp.broadcast_to(t[:, None, :], (L, hch, 2 * p)).reshape(w, 2 * p)

    pr, pi = lam_pow_rows(step + 1.0)
    cpt_ref[0] = (x_c * per_channel(pr) + y_c * per_channel(pi)).astype(cpt_ref.dtype)
    pr, pi = lam_pow_rows(step)
    c_lag = x_c * per_channel(pr) + y_c * per_channel(pi)
    pr, pi = lam_pow_rows(jnp.full((1, 1), float(L), F32))
    lam_ref[0, 0:1, :] = pr
    lam_ref[0, 1:2, :] = jnp.where(re_half, -pi, pi)

    a_r, a_i = arc_ref[0], aic_ref[0]
    mag = jnp.exp(a_r * dt)
    n_re, n_im = mag * jnp.cos(a_i * dt) - 1.0, mag * jnp.sin(a_i * dt)
    den = a_r * a_r + a_i * a_i
    z_re = (n_re * a_r + n_im * a_i) / den
    z_im = (n_im * a_r - n_re * a_i) / den
    b_r, b_i = btr_ref[0], bti_ref[0]
    zb_re = z_re * b_r - z_im * b_i
    zb_im = z_re * b_i + z_im * b_r
    e = (L - 1 - lax.broadcasted_iota(jnp.int32, (1, L), 1)).astype(F32)
    mag, ang = jnp.exp(e * (a_r * dt)), e * (a_i * dt)
    spread = (lax.broadcasted_iota(jnp.int32, (L, w), 1) // hch
              == lax.broadcasted_iota(jnp.int32, (L, w), 0)).astype(F32)
    q_re = jnp.dot(mag * jnp.cos(ang), spread, preferred_element_type=F32, precision=lax.Precision.HIGHEST)
    q_im = jnp.dot(mag * jnp.sin(ang), spread, preferred_element_type=F32, precision=lax.Precision.HIGHEST)
    w_re = q_re * zb_re - q_im * zb_im
    w_im = q_re * zb_im + q_im * zb_re
    bpt_ref[0] = jnp.concatenate([w_re, w_im], axis=0).astype(bpt_ref.dtype)
    bpst_ref[0] = jnp.concatenate([w_im, w_re], axis=0).astype(bpst_ref.dtype)

    k_lag = jnp.dot(c_lag, jnp.concatenate([zb_re, zb_im], axis=0),
                    preferred_element_type=F32, precision=lax.Precision.HIGHEST)
    lane_s = lax.broadcasted_iota(jnp.int32, (w, w), 1) // hch
    toep = jnp.zeros((w, w), F32)
    for s in range(L):
        shifted = k_lag if s == 0 else jnp.concatenate([jnp.zeros((s * hch, w), F32), k_lag[:w - s * hch]], axis=0)
        toep = jnp.where(lane_s == s, shifted, toep)
    diag = lax.broadcasted_iota(jnp.int32, (w, w), 0) == lax.broadcasted_iota(jnp.int32, (w, w), 1)
    toept_ref[0] = (toep + jnp.where(diag, dskip_ref[0], 0.0)).astype(toept_ref.dtype)


def _s5_input_kernel(us_ref, ut_ref):
    ut = us_ref[...].astype(F32).T
    ut_ref[...] = ut.reshape(ut_ref.shape).astype(ut_ref.dtype)


def _s5_input(us, n_groups, bsz, seq):
    L = SSM_CHUNK
    nc = seq // L
    sw = us.shape[1] // L
    assert nc % LANES == 0
    return pl.pallas_call(
        _s5_input_kernel,
        grid=(bsz, L),
        in_specs=[pl.BlockSpec((nc, sw), lambda b, s: (b, s))],
        out_specs=pl.BlockSpec((n_groups, SSM_GROUP, nc), lambda b, s: (0, s, b)),
        out_shape=jax.ShapeDtypeStruct((n_groups, L * SSM_GROUP, bsz * nc), BF16),
        compiler_params=pltpu.CompilerParams(
            dimension_semantics=("parallel", "parallel"), vmem_limit_bytes=VMEM_LIMIT),
        name="s5_input",
    )(us)


def _s5_kernel(ut_ref, toept_ref, bpt_ref, bpst_ref, cpt_ref, lam_ref, yt_ref, s_ref, ss_ref, hp_ref, *, bsz, nc):
    groups = range(ut_ref.shape[0])
    for j in groups:
        ut = ut_ref[j]
        s_ref[j] = _dot(bpt_ref[j], ut).T
        ss_ref[j] = _dot(bpst_ref[j], ut).T
    a = [lam_ref[j, 0:1, :] for j in groups]
    bv = [lam_ref[j, 1:2, :] for j in groups]

    def body(c, carry):
        rows = pl.ds(c, bsz, stride=nc)
        out = []
        for j in groups:
            h, hs = carry[j]
            hp_ref[j, rows, :] = h
            out.append((a[j] * h + bv[j] * hs + s_ref[j, rows, :], a[j] * hs - bv[j] * h + ss_ref[j, rows, :]))
        return tuple(out)

    zero = jnp.zeros((bsz, 2 * SSM_STATE), F32)
    lax.fori_loop(0, nc, body, tuple((zero, zero) for _ in groups), unroll=8)
    for j in groups:
        y = _dot(toept_ref[j], ut_ref[j]) + _dot_nt(cpt_ref[j], hp_ref[j].astype(BF16))
        yt_ref[j] = y.astype(yt_ref.dtype)


def _s5_scan(ut, tables, bsz, seq):
    toept, bpt, bpst, cpt, lam = tables
    g, w, cols = ut.shape
    nc = seq // SSM_CHUNK
    p2 = 2 * SSM_STATE
    gs = S5_GROUPS_PER_STEP
    assert g % gs == 0
    grp = lambda i: (i, 0, 0)
    return pl.pallas_call(
        functools.partial(_s5_kernel, bsz=bsz, nc=nc),
        grid=(g // gs,),
        in_specs=[
            pl.BlockSpec((gs, w, cols), grp), pl.BlockSpec((gs, w, w), grp),
            pl.BlockSpec((gs, p2, w), grp), pl.BlockSpec((gs, p2, w), grp),
            pl.BlockSpec((gs, w, p2), grp), pl.BlockSpec((gs, 2, p2), grp),
        ],
        out_specs=pl.BlockSpec((gs, w, cols), grp),
        out_shape=jax.ShapeDtypeStruct((g, w, cols), BF16),
        scratch_shapes=[pltpu.VMEM((gs, cols, p2), F32), pltpu.VMEM((gs, cols, p2), F32),
                        pltpu.VMEM((gs, cols, p2), F32)],
        compiler_params=pltpu.CompilerParams(dimension_semantics=("parallel",), vmem_limit_bytes=VMEM_LIMIT),
        name="s5_scan",
    )(ut, toept, bpt, bpst, cpt, lam)


def _s5_post_kernel(yt_ref, wglut_ref, ss_ref):
    gh, nc = yt_ref.shape[0] * yt_ref.shape[1], yt_ref.shape[2]
    gt = jax.nn.gelu(yt_ref[...].astype(F32).reshape(gh, nc))
    ssm_t = gt * jax.nn.sigmoid(_dot(wglut_ref[...], gt.astype(BF16)))
    ss_ref[...] = ssm_t.T.astype(ss_ref.dtype)


def _s5_post(yt, w_glu_t, bsz, seq):
    g = yt.shape[0]
    L = SSM_CHUNK
    nc = seq // L
    sw = g * SSM_GROUP
    return pl.pallas_call(
        _s5_post_kernel,
        grid=(bsz, L),
        in_specs=[
            pl.BlockSpec((g, SSM_GROUP, nc), lambda b, s: (0, s, b)),
            pl.BlockSpec(w_glu_t.shape, lambda b, s: (0, 0)),
        ],
        out_specs=pl.BlockSpec((nc, sw), lambda b, s: (b, s)),
        out_shape=jax.ShapeDtypeStruct((bsz * nc, L * sw), BF16),
        compiler_params=pltpu.CompilerParams(
            dimension_semantics=("parallel", "parallel"), vmem_limit_bytes=VMEM_LIMIT),
        name="s5_post",
    )(yt, w_glu_t)


def _merge_kernel(x_ref, attn_ref, ss_ref, ga_ref, gs_ref, wap_ref, wsp_ref, wout_ref, g_ref, b_ref,
                  h_ref, slab_ref, *, alpha):
    ssm = _from_step_layout(ss_ref, slab_ref)
    merged = (ga_ref[...].astype(F32) * _dot(attn_ref[...], wap_ref[...])
              + gs_ref[...].astype(F32) * _dot(ssm.astype(BF16), wsp_ref[...]))
    mix = _dot(merged.astype(BF16), wout_ref[...])
    h_ref[...] = _layer_norm(alpha * x_ref[...] + mix, g_ref[...], b_ref[...])


def _merge(x2, attn, ss, ga, gs, w_ap, w_sp, w_out, ln_g, ln_b, alpha):
    n, dm = x2.shape
    aw = attn.shape[1]
    L = SSM_CHUNK
    sw = ss.shape[1] // L
    tm = ROW_TILE
    row = lambda i: (i, 0)
    full = lambda i: (0, 0)
    return pl.pallas_call(
        functools.partial(_merge_kernel, alpha=alpha),
        grid=(n // tm,),
        in_specs=[
            pl.BlockSpec((tm, dm), row), pl.BlockSpec((tm, aw), row), pl.BlockSpec((tm // L, L * sw), row),
            pl.BlockSpec((tm, dm), row), pl.BlockSpec((tm, dm), row),
            pl.BlockSpec(w_ap.shape, full), pl.BlockSpec(w_sp.shape, full),
            pl.BlockSpec(w_out.shape, full), pl.BlockSpec((1, dm), full), pl.BlockSpec((1, dm), full),
        ],
        out_specs=pl.BlockSpec((tm, dm), row),
        out_shape=jax.ShapeDtypeStruct((n, dm), F32),
        scratch_shapes=[pltpu.VMEM((sw // LANES, tm, LANES), F32)],
        compiler_params=pltpu.CompilerParams(dimension_semantics=("parallel",), vmem_limit_bytes=VMEM_LIMIT),
        name="merge_ln1",
    )(x2, attn, ss, ga, gs, w_ap, w_sp, w_out, ln_g, ln_b)


def _ffn_kernel(h_ref, wup_ref, cw_ref, wd_ref, g_ref, b_ref, o_ref, halo_ref, act_ref,
                *, alpha, tiles_per_seq, dff):
    i = pl.program_id(0)
    h = h_ref[...]
    hb = h.astype(BF16)
    tm = h.shape[0]
    ck = FF_CHUNK
    row8 = lax.broadcasted_iota(jnp.int32, (SUBLANES, ck), 0)

    @pl.when((i % tiles_per_seq) == 0)
    def _():
        halo_ref[...] = jnp.zeros_like(halo_ref)

    def conv(up, c0):
        cols = slice(c0, c0 + ck)
        prev = halo_ref[:, cols]
        halo_ref[:, cols] = up[tm - SUBLANES:]
        p1 = prev[SUBLANES - 1:SUBLANES]
        p2 = prev[SUBLANES - 2:SUBLANES - 1]
        r1 = pltpu.roll(up, 1, 0)
        r2 = pltpu.roll(up, 2, 0)
        d1 = jnp.concatenate([jnp.where(row8 == 0, p1, r1[:SUBLANES]), r1[SUBLANES:]], axis=0)
        d2 = jnp.concatenate([jnp.where(row8 == 0, p2, jnp.where(row8 == 1, p1, r2[:SUBLANES])), r2[SUBLANES:]],
                             axis=0)
        cw = cw_ref[:, cols]
        return cw[3:4] + cw[0:1] * d2 + cw[1:2] * d1 + cw[2:3] * up

    for j in range(dff // ck):
        c0 = j * ck
        val = conv(_dot(hb, wup_ref[:, c0:c0 + ck]), c0)
        gate = conv(_dot(hb, wup_ref[:, dff + c0:dff + c0 + ck]), dff + c0)
        act_ref[:, c0:c0 + ck] = (jax.nn.gelu(gate) * val).astype(BF16)
    ff = _dot(act_ref[...], wd_ref[...])
    o_ref[...] = _layer_norm(alpha * h + ff, g_ref[...], b_ref[...])


def _ffn(h, w_up, conv_w, conv_b, w_down, ln_g, ln_b, alpha, seq):
    n, dm = h.shape
    dff = w_down.shape[0]
    tm = FFN_TILE
    assert dff % FF_CHUNK == 0 and seq % tm == 0
    cw = jnp.concatenate([conv_w, conv_b[None, :]], axis=0)
    cw = jnp.pad(cw, ((0, SUBLANES - cw.shape[0]), (0, 0)))
    row = lambda i: (i, 0)
    full = lambda i: (0, 0)
    once = pl.Buffered(1)
    return pl.pallas_call(
        functools.partial(_ffn_kernel, alpha=alpha, tiles_per_seq=seq // tm, dff=dff),
        grid=(n // tm,),
        in_specs=[
            pl.BlockSpec((tm, dm), row),
            pl.BlockSpec(w_up.shape, full, pipeline_mode=once),
            pl.BlockSpec(cw.shape, full),
            pl.BlockSpec(w_down.shape, full, pipeline_mode=once),
            pl.BlockSpec((1, dm), full), pl.BlockSpec((1, dm), full),
        ],
        out_specs=pl.BlockSpec((tm, dm), row),
        out_shape=jax.ShapeDtypeStruct((n, dm), F32),
        scratch_shapes=[
            pltpu.VMEM((SUBLANES, 2 * dff), F32),
            pltpu.VMEM((tm, dff), BF16),
        ],
        compiler_params=pltpu.CompilerParams(dimension_semantics=("arbitrary",), vmem_limit_bytes=VMEM_LIMIT),
        name="conv_ffn_ln2",
    )(h, w_up, cw, w_down, ln_g, ln_b)


def kernel(x, w_in, w_attn_proj, ssm_a_re, ssm_a_im, ssm_log_dt, ssm_b_re, ssm_b_im, ssm_c_re, ssm_c_im, ssm_d,
           w_glu, w_ssm_proj, w_out, ln1_g, ln1_b, w_up, conv_w, conv_b, w_down, ln2_g, ln2_b):
    bsz, seq, dm = x.shape
    depth = w_in.shape[0]
    alpha = (2.0 * depth) ** 0.25
    n = bsz * seq
    assert seq % ROW_TILE == 0 and ROW_TILE % MOBA_BLOCK == 0 and seq % SSM_CHUNK == 0
    h = x
    for l in range(depth):
        w_in_b = w_in[l].astype(BF16)
        h2 = h.reshape(n, dm)
        q, k, vt, us, ga, gs, kmean = _in_projection(h2, w_in[l].astype(BF16), seq)
        aw = q.shape[1]
        nb = seq // MOBA_BLOCK
        q3 = q.reshape(bsz, seq, aw)
        sel = _moba_select(q3, kmean.reshape(bsz, nb, aw))
        attn = _moba_attention(q3, k.reshape(bsz, seq, aw), vt.reshape(bsz, nb, *vt.shape[1:]), sel).reshape(n, aw)
        tables = _s5_tables(ssm_a_re[l], ssm_a_im[l], ssm_log_dt[l], ssm_b_re[l], ssm_b_im[l],
                            ssm_c_re[l], ssm_c_im[l], ssm_d[l])
        ut = _s5_input(us, ssm_a_re.shape[1], bsz, seq)
        ss = _s5_post(_s5_scan(ut, tables, bsz, seq), w_glu[l].T.astype(BF16), bsz, seq)
        h1 = _merge(h2, attn, ss, ga, gs, w_attn_proj[l].astype(BF16), w_ssm_proj[l].astype(BF16),
                    w_out[l].astype(BF16), ln1_g[l][None], ln1_b[l][None], alpha)
        h = _ffn(h1, w_up[l].astype(BF16), conv_w[l], conv_b[l], w_down[l].astype(BF16),
                 ln2_g[l][None], ln2_b[l][None], alpha, seq).reshape(bsz, seq, dm)
    return h
```

```python
import functools
import math

import jax
import jax.numpy as jnp
from jax import lax
from jax.experimental import pallas as pl
from jax.experimental.pallas import tpu as pltpu

F32 = jnp.float32
BF16 = jnp.bfloat16

N_HEADS = 8
HEAD_DIM = 64
ROT_DIM = HEAD_DIM // 4
ROPE_THETA = 500000.0
MOBA_BLOCK = 256
MOBA_TOP_K = 3
SSM_GROUP = 16
SSM_STATE = 64
CONV_WIDTH = 3
LN_EPS = 1e-5
NEG_INF = -1e30

LANES = 128
SUBLANES = 8
VMEM_LIMIT = 56 * 1024 * 1024

SSM_CHUNK = 16
S5_GROUPS_PER_STEP = 4
ROW_TILE = 512
IN_TILE = 1024
FFN_TILE = 1024
FF_CHUNK = 256

_NT = (((1,), (1,)), ((), ()))


def _dot(a, b):
    return jnp.dot(a, b, preferred_element_type=F32)


def _dot_nt(a, b):
    return lax.dot_general(a, b, _NT, preferred_element_type=F32)


def _layer_norm(t, g, b):
    mu = jnp.mean(t, axis=-1, keepdims=True)
    d = t - mu
    var = jnp.mean(d * d, axis=-1, keepdims=True)
    return d * lax.rsqrt(var + LN_EPS) * g + b


def _to_step_layout(dst_ref, src, slab_ref):
    chunks, width = dst_ref.shape[0], src.shape[1]
    for j in range(width // LANES):
        slab_ref[j] = src[:, j * LANES:(j + 1) * LANES]
    for s in range(SSM_CHUNK):
        for j in range(width // LANES):
            c0 = s * width + j * LANES
            dst_ref[:, c0:c0 + LANES] = slab_ref[j, pl.ds(s, chunks, stride=SSM_CHUNK), :].astype(dst_ref.dtype)


def _from_step_layout(src_ref, slab_ref):
    chunks, width = src_ref.shape[0], slab_ref.shape[0] * LANES
    for s in range(SSM_CHUNK):
        for j in range(width // LANES):
            c0 = s * width + j * LANES
            slab_ref[j, pl.ds(s, chunks, stride=SSM_CHUNK), :] = src_ref[:, c0:c0 + LANES].astype(slab_ref.dtype)
    return jnp.concatenate([slab_ref[j] for j in range(width // LANES)], axis=1)


def _inproj_kernel(x_ref, w_ref, wvt_ref, cos_ref, s1_ref, s2_ref,
                   q_ref, k_ref, vt_ref, us_ref, ga_ref, gs_ref, km_ref, tmp_ref, *, aw, sw, dm, scale):
    xb = x_ref[...].astype(BF16)
    cos_t, s1_t, s2_t = cos_ref[...], s1_ref[...], s2_ref[...]
    tm = xb.shape[0]

    def rotary(t):
        return t * cos_t + pltpu.roll(t, ROT_DIM // 2, 1) * s1_t + pltpu.roll(t, LANES - ROT_DIM // 2, 1) * s2_t

    q = _dot(xb, w_ref[:, 0:aw])
    for j in range(aw // LANES):
        sl = slice(j * LANES, (j + 1) * LANES)
        q_ref[:, sl] = (rotary(q[:, sl]) * scale).astype(BF16)
    k = _dot(xb, w_ref[:, aw:2 * aw])
    for j in range(aw // LANES):
        sl = slice(j * LANES, (j + 1) * LANES)
        kr = rotary(k[:, sl])
        k_ref[:, sl] = kr.astype(BF16)
        for r in range(tm // MOBA_BLOCK):
            km_ref[r, :, sl] = jnp.mean(kr[r * MOBA_BLOCK:(r + 1) * MOBA_BLOCK], axis=0, keepdims=True)
    vt = _dot_nt(wvt_ref[...], xb)
    for r in range(tm // MOBA_BLOCK):
        for j in range(aw // LANES):
            vt_ref[r, j] = vt[j * LANES:(j + 1) * LANES, r * MOBA_BLOCK:(r + 1) * MOBA_BLOCK].astype(BF16)
    _to_step_layout(us_ref, _dot(xb, w_ref[:, 3 * aw:3 * aw + sw]), tmp_ref)
    c0 = 3 * aw + sw
    ga_ref[...] = jax.nn.sigmoid(_dot(xb, w_ref[:, c0:c0 + dm])).astype(BF16)
    gs_ref[...] = jax.nn.sigmoid(_dot(xb, w_ref[:, c0 + dm:c0 + 2 * dm])).astype(BF16)


def _rotary_tables(seq):
    half = ROT_DIM // 2
    inv_freq = ROPE_THETA ** (-jnp.arange(0, ROT_DIM, 2, dtype=F32) / ROT_DIM)
    ang = jnp.arange(seq, dtype=F32)[:, None] * inv_freq[None, :]
    cos, sin = jnp.cos(ang), jnp.sin(ang)
    ones = jnp.ones((seq, HEAD_DIM - ROT_DIM), F32)
    zeros = jnp.zeros((seq, HEAD_DIM - ROT_DIM), F32)
    zh = jnp.zeros((seq, half), F32)
    cos_h = jnp.concatenate([cos, cos, ones], axis=1)
    s1_h = jnp.concatenate([zh, sin, zeros], axis=1)
    s2_h = jnp.concatenate([-sin, zh, zeros], axis=1)
    rep = LANES // HEAD_DIM
    return tuple(jnp.tile(t, (1, rep)) for t in (cos_h, s1_h, s2_h))


def _in_projection(x2, w_in, seq):
    n, dm = x2.shape
    aw, sw = N_HEADS * HEAD_DIM, w_in.shape[1] - 3 * N_HEADS * HEAD_DIM - 2 * dm
    tm = IN_TILE
    assert seq % tm == 0 and tm % MOBA_BLOCK == 0
    cos_t, s1_t, s2_t = _rotary_tables(seq)
    w_vt = w_in[:, 2 * aw:3 * aw].T
    hp = aw // LANES
    tiles_per_seq = seq // tm
    row = lambda i: (i, 0)
    tab = lambda i: (i % tiles_per_seq, 0)
    L = SSM_CHUNK
    step_rows = lambda i: (i, 0)
    out_shape = (
        jax.ShapeDtypeStruct((n, aw), BF16), jax.ShapeDtypeStruct((n, aw), BF16),
        jax.ShapeDtypeStruct((n // MOBA_BLOCK, hp, LANES, MOBA_BLOCK), BF16),
        jax.ShapeDtypeStruct((n // L, L * sw), BF16),
        jax.ShapeDtypeStruct((n, dm), BF16), jax.ShapeDtypeStruct((n, dm), BF16),
        jax.ShapeDtypeStruct((n // MOBA_BLOCK, 1, aw), F32),
    )
    return pl.pallas_call(
        functools.partial(_inproj_kernel, aw=aw, sw=sw, dm=dm, scale=HEAD_DIM ** -0.5 * math.log2(math.e)),
        grid=(n // tm,),
        in_specs=[
            pl.BlockSpec((tm, dm), row),
            pl.BlockSpec(w_in.shape, lambda i: (0, 0)),
            pl.BlockSpec(w_vt.shape, lambda i: (0, 0)),
            pl.BlockSpec((tm, LANES), tab), pl.BlockSpec((tm, LANES), tab), pl.BlockSpec((tm, LANES), tab),
        ],
        out_specs=(
            pl.BlockSpec((tm, aw), row), pl.BlockSpec((tm, aw), row),
            pl.BlockSpec((tm // MOBA_BLOCK, hp, LANES, MOBA_BLOCK), lambda i: (i, 0, 0, 0)),
            pl.BlockSpec((tm // L, L * sw), step_rows),
            pl.BlockSpec((tm, dm), row), pl.BlockSpec((tm, dm), row),
            pl.BlockSpec((tm // MOBA_BLOCK, 1, aw), lambda i: (i, 0, 0)),
        ),
        out_shape=out_shape,
        scratch_shapes=[pltpu.VMEM((sw // LANES, tm, LANES), F32)],
        compiler_params=pltpu.CompilerParams(dimension_semantics=("parallel",), vmem_limit_bytes=VMEM_LIMIT),
        name="in_projection",
    )(x2, w_in, w_vt, cos_t, s1_t, s2_t)


PAIR = 2 * MOBA_BLOCK
Q_TILE = PAIR
MASKED = 2.0 * NEG_INF
V_ROWS = HEAD_DIM + 16
SEL_ROWS = SUBLANES
SEL_CHUNK = 1024


def _head_split(q2):
    qf = q2.astype(F32)
    lane = lax.broadcasted_iota(jnp.int32, qf.shape, 1)
    return (jnp.where(lane < HEAD_DIM, qf, 0.0).astype(BF16), jnp.where(lane >= HEAD_DIM, qf, 0.0).astype(BF16))


def _select_kernel(q_ref, km_ref, sel_ref, *, nb):
    kmb = km_ref[0].astype(BF16)
    seq = q_ref.shape[1]
    shape = (nb, SEL_CHUNK)
    blk = lax.broadcasted_iota(jnp.int32, shape, 0)
    for c in range(seq // SEL_CHUNK):
        cols = slice(c * SEL_CHUNK, (c + 1) * SEL_CHUNK)
        own = (lax.broadcasted_iota(jnp.int32, shape, 1) + c * SEL_CHUNK) // MOBA_BLOCK
        qh = _head_split(q_ref[0, cols, :])
        for h in range(2):
            g = jnp.where(blk < own, _dot_nt(kmb, qh[h]), NEG_INF)
            rows = []
            for _ in range(min(MOBA_TOP_K, nb)):
                mx = jnp.max(g, axis=0, keepdims=True)
                first = jnp.min(jnp.where(g == mx, blk, nb), axis=0, keepdims=True)
                rows.append(jnp.where(first < own[0:1], first, -1))
                g = jnp.where(blk == first, -jnp.inf, g)
            rows.append(jnp.full((SEL_ROWS - len(rows), SEL_CHUNK), -1, jnp.int32))
            sel_ref[0, 0, h, :, cols] = jnp.concatenate(rows, axis=0)


def _moba_select(q, kmean):
    bsz, seq, aw = q.shape
    nb = seq // MOBA_BLOCK
    hp = aw // LANES
    assert seq % SEL_CHUNK == 0
    return pl.pallas_call(
        functools.partial(_select_kernel, nb=nb),
        grid=(bsz, hp),
        in_specs=[
            pl.BlockSpec((1, seq, LANES), lambda b, h: (b, 0, h)),
            pl.BlockSpec((1, nb, LANES), lambda b, h: (b, 0, h)),
        ],
        out_specs=pl.BlockSpec((1, 1, 2, SEL_ROWS, seq), lambda b, h: (b, h, 0, 0, 0)),
        out_shape=jax.ShapeDtypeStruct((bsz, hp, 2, SEL_ROWS, seq), jnp.int32),
        compiler_params=pltpu.CompilerParams(
            dimension_semantics=("parallel", "parallel"), vmem_limit_bytes=VMEM_LIMIT),
        name="moba_select",
    )(q, kmean)


def _attn_kernel(q_ref, k_ref, vt_ref, sel_ref, o_ref,
                 va_ref, vb_ref, s0_ref, s1_ref, p0_ref, p1_ref, cm_ref, al_ref, acc_ref, *, nb):
    i = pl.program_id(2)
    blk = MOBA_BLOCK
    v_refs = (va_ref, vb_ref)
    s_refs = (s0_ref, s1_ref)
    p_refs = (p0_ref, p1_ref)

    @pl.when(i == 0)
    def _():
        ones = jnp.ones((V_ROWS - HEAD_DIM, blk), BF16)

        def fill(j, c):
            for half in range(2):
                vt = vt_ref[0, 2 * j + half, 0]
                cols = slice(half * blk, (half + 1) * blk)
                for h in range(2):
                    v_refs[h][j, 0:HEAD_DIM, cols] = vt[h * HEAD_DIM:(h + 1) * HEAD_DIM]
                    v_refs[h][j, HEAD_DIM:V_ROWS, cols] = ones
            return c
        lax.fori_loop(0, nb // 2, fill, 0)

    qh = _head_split(q_ref[0])
    sel = (sel_ref[0, 0, 0], sel_ref[0, 0, 1])

    def chosen(h, n):
        sv = sel[h]
        hit = sv[0:1] == n
        for r in range(1, MOBA_TOP_K):
            hit = jnp.logical_or(hit, sv[r:r + 1] == n)
        return hit

    def past_bias(pair):
        return [tuple(jnp.where(chosen(h, 2 * pair + half), 0.0, MASKED) for half in range(2)) for h in range(2)]

    key_i = lax.broadcasted_iota(jnp.int32, (blk, blk), 0)
    qry_i = lax.broadcasted_iota(jnp.int32, (blk, blk), 1)
    causal = key_i <= qry_i

    def own_mask(s, h):
        top = jnp.concatenate([jnp.where(causal, s[:blk, :blk], MASKED),
                               jnp.where(chosen(h, 2 * i)[:, blk:], s[:blk, blk:], MASKED)], axis=1)
        bot = jnp.concatenate([jnp.full((blk, blk), MASKED, F32),
                               jnp.where(causal, s[blk:, blk:], MASKED)], axis=1)
        return jnp.concatenate([top, bot], axis=0)

    def stage_a(pair, slot, own):
        kp = k_ref[0, pl.ds(pl.multiple_of(pair * PAIR, PAIR), PAIR), :]
        for h in range(2):
            s = _dot_nt(kp, qh[h])
            if own:
                s = own_mask(s, h)
            s_refs[slot][h] = s
            cm_ref[slot, 2 * h:2 * h + 1, :] = jnp.max(s[:blk], axis=0, keepdims=True)
            cm_ref[slot, 2 * h + 1:2 * h + 2, :] = jnp.max(s[blk:], axis=0, keepdims=True)

    def stage_b(slot, m, bias):
        m_out = []
        for h in range(2):
            cm0 = cm_ref[slot, 2 * h:2 * h + 1, :]
            cm1 = cm_ref[slot, 2 * h + 1:2 * h + 2, :]
            if bias is None:
                m_new = jnp.maximum(m[h], jnp.maximum(cm0, cm1))
                c0 = c1 = m_new
            else:
                b0, b1 = bias[h]
                m_new = jnp.maximum(m[h], jnp.maximum(cm0 + b0, cm1 + b1))
                c0, c1 = m_new - b0, m_new - b1
            al_ref[slot, h:h + 1, :] = jnp.exp2(m[h] - m_new)
            p_refs[slot][h, 0:blk] = jnp.exp2(s_refs[slot][h, 0:blk] - c0).astype(BF16)
            p_refs[slot][h, blk:PAIR] = jnp.exp2(s_refs[slot][h, blk:PAIR] - c1).astype(BF16)
            m_out.append(m_new)
        return tuple(m_out)

    def stage_c(pair, slot):
        for h in range(2):
            pv = _dot(v_refs[h][pair], p_refs[slot][h])
            acc_ref[h] = acc_ref[h] * al_ref[slot, h:h + 1, :] + pv

    def trip(t, slot, m):
        stage_a(t - 1, slot, False)
        m = stage_b(1 - slot, m, past_bias(t - 2))
        stage_c(jnp.where(t == 2, i, t - 3), slot)
        return m

    acc_ref[...] = jnp.zeros_like(acc_ref)
    floor = jnp.full((1, Q_TILE), NEG_INF, F32)
    stage_a(i, 0, True)
    stage_a(0, 1, False)
    m0 = stage_b(0, (floor, floor), None)
    n_list = i + 1

    @pl.when(i == 0)
    def _():
        stage_c(i, 0)

    @pl.when(i > 0)
    def _():
        def double_trip(d, m):
            t = 2 + 2 * d
            return trip(t + 1, 1, trip(t, 0, m))

        m1 = lax.fori_loop(0, (n_list - 2) // 2, double_trip, m0)

        @pl.when(n_list % 2 == 1)
        def _():
            m2 = trip(n_list - 1, 0, m1)
            stage_b(0, m2, past_bias(n_list - 2))
            stage_c(n_list - 3, 1)
            stage_c(n_list - 2, 0)

        @pl.when(n_list % 2 == 0)
        def _():
            stage_b(1, m1, past_bias(n_list - 2))
            stage_c(jnp.where(n_list == 2, i, n_list - 3), 0)
            stage_c(n_list - 2, 1)

    outs = []
    for h in range(2):
        acc = acc_ref[h]
        outs.append(acc[:HEAD_DIM] / acc[HEAD_DIM:HEAD_DIM + 1])
    o_ref[0] = jnp.concatenate(outs, axis=0).T.astype(o_ref.dtype)


def _moba_attention(q, k, vt, sel):
    bsz, seq, aw = q.shape
    nb = seq // MOBA_BLOCK
    hp = aw // LANES
    assert nb % 2 == 0
    return pl.pallas_call(
        functools.partial(_attn_kernel, nb=nb),
        grid=(bsz, hp, nb // 2),
        in_specs=[
            pl.BlockSpec((1, Q_TILE, LANES), lambda b, h, i: (b, i, h)),
            pl.BlockSpec((1, seq, LANES), lambda b, h, i: (b, 0, h)),
            pl.BlockSpec((1, nb, 1, LANES, MOBA_BLOCK), lambda b, h, i: (b, 0, h, 0, 0)),
            pl.BlockSpec((1, 1, 2, SEL_ROWS, Q_TILE), lambda b, h, i: (b, h, 0, 0, i)),
        ],
        out_specs=pl.BlockSpec((1, Q_TILE, LANES), lambda b, h, i: (b, i, h)),
        out_shape=jax.ShapeDtypeStruct((bsz, seq, aw), BF16),
        scratch_shapes=[
            pltpu.VMEM((nb // 2, V_ROWS, PAIR), BF16), pltpu.VMEM((nb // 2, V_ROWS, PAIR), BF16),
            pltpu.VMEM((2, PAIR, Q_TILE), F32), pltpu.VMEM((2, PAIR, Q_TILE), F32),
            pltpu.VMEM((2, PAIR, Q_TILE), BF16), pltpu.VMEM((2, PAIR, Q_TILE), BF16),
            pltpu.VMEM((2, SUBLANES, Q_TILE), F32),
            pltpu.VMEM((2, SUBLANES, Q_TILE), F32),
            pltpu.VMEM((2, V_ROWS, Q_TILE), F32),
        ],
        compiler_params=pltpu.CompilerParams(
            dimension_semantics=("parallel", "parallel", "arbitrary"), vmem_limit_bytes=VMEM_LIMIT),
        name="moba_attention",
    )(q, k, vt, sel)


def _s5_tables(a_re, a_im, log_dt, b_re, b_im, c_re, c_im, d_skip):
    g, p = a_re.shape
    L, hch = SSM_CHUNK, SSM_GROUP
    w = L * hch
    two = lambda t: jnp.concatenate([t, t], axis=-1)
    operands = (
        log_dt.reshape(g, 1, 1),
        two(a_re).reshape(g, 1, 2 * p), two(a_im).reshape(g, 1, 2 * p),
        a_re.reshape(g, p, 1), a_im.reshape(g, p, 1),
        jnp.tile(b_re, (1, 1, L)), jnp.tile(b_im, (1, 1, L)),
        two(c_re), two(c_im),
        jnp.tile(d_skip, (1, L)).reshape(g, 1, w),
    )
    grp = lambda i: (i, 0, 0)
    out_shape = (
        jax.ShapeDtypeStruct((g, w, w), BF16), jax.ShapeDtypeStruct((g, 2 * p, w), BF16),
        jax.ShapeDtypeStruct((g, 2 * p, w), BF16), jax.ShapeDtypeStruct((g, w, 2 * p), BF16),
        jax.ShapeDtypeStruct((g, 2, 2 * p), F32),
    )
    return pl.pallas_call(
        _s5_tables_kernel,
        grid=(g,),
        in_specs=[pl.BlockSpec((1,) + t.shape[1:], grp) for t in operands],
        out_specs=tuple(pl.BlockSpec((1,) + t.shape[1:], grp) for t in out_shape),
        out_shape=out_shape,
        compiler_params=pltpu.CompilerParams(dimension_semantics=("parallel",), vmem_limit_bytes=VMEM_LIMIT),
        name="s5_tables",
    )(*operands)


def _s5_tables_kernel(ldt_ref, ar2_ref, ai2_ref, arc_ref, aic_ref, btr_ref, bti_ref, c2r_ref, c2i_ref, dskip_ref,
                      toept_ref, bpt_ref, bpst_ref, cpt_ref, lam_ref):
    L, hch, p = SSM_CHUNK, SSM_GROUP, SSM_STATE
    w = L * hch
    dt = jnp.exp(ldt_ref[0])

    ar, ai = ar2_ref[0] * dt, ai2_ref[0] * dt
    re_half = lax.broadcasted_iota(jnp.int32, (1, 2 * p), 1) < p

    def lam_pow_rows(e):
        mag, ang = jnp.exp(e * ar), e * ai
        return mag * jnp.cos(ang), mag * jnp.sin(ang)

    c_re = jnp.concatenate([c2r_ref[0]] * L, axis=0)
    c_im = jnp.concatenate([c2i_ref[0]] * L, axis=0)
    x_c = jnp.where(re_half, c_re, -c_im)
    y_c = jnp.where(re_half, -c_im, -c_re)
    step = lax.broadcasted_iota(jnp.int32, (L, 1), 0).astype(F32)

    def per_channel(t):
        return jnp.broadcast_to(t[:, None, :], (L, hch, 2 * p)).reshape(w, 2 * p)

    pr, pi = lam_pow_rows(step + 1.0)
    cpt_ref[0] = (x_c * per_channel(pr) + y_c * per_channel(pi)).astype(cpt_ref.dtype)
    pr, pi = lam_pow_rows(step)
    c_lag = x_c * per_channel(pr) + y_c * per_channel(pi)
    pr, pi = lam_pow_rows(jnp.full((1, 1), float(L), F32))
    lam_ref[0, 0:1, :] = pr
    lam_ref[0, 1:2, :] = jnp.where(re_half, -pi, pi)

    a_r, a_i = arc_ref[0], aic_ref[0]
    mag = jnp.exp(a_r * dt)
    n_re, n_im = mag * jnp.cos(a_i * dt) - 1.0, mag * jnp.sin(a_i * dt)
    den = a_r * a_r + a_i * a_i
    z_re = (n_re * a_r + n_im * a_i) / den
    z_im = (n_im * a_r - n_re * a_i) / den
    b_r, b_i = btr_ref[0], bti_ref[0]
    zb_re = z_re * b_r - z_im * b_i
    zb_im = z_re * b_i + z_im * b_r
    e = (L - 1 - lax.broadcasted_iota(jnp.int32, (1, L), 1)).astype(F32)
    mag, ang = jnp.exp(e * (a_r * dt)), e * (a_i * dt)
    spread = (lax.broadcasted_iota(jnp.int32, (L, w), 1) // hch
              == lax.broadcasted_iota(jnp.int32, (L, w), 0)).astype(F32)
    q_re = jnp.dot(mag * jnp.cos(ang), spread, preferred_element_type=F32, precision=lax.Precision.HIGHEST)
    q_im = jnp.dot(mag * jnp.sin(ang), spread, preferred_element_type=F32, precision=lax.Precision.HIGHEST)
    w_re = q_re * zb_re - q_im * zb_im
    w_im = q_re * zb_im + q_im * zb_re
    bpt_ref[0] = jnp.concatenate([w_re, w_im], axis=0).astype(bpt_ref.dtype)
    bpst_ref[0] = jnp.concatenate([w_im, w_re], axis=0).astype(bpst_ref.dtype)

    k_lag = jnp.dot(c_lag, jnp.concatenate([zb_re, zb_im], axis=0),
                    preferred_element_type=F32, precision=lax.Precision.HIGHEST)
    lane_s = lax.broadcasted_iota(jnp.int32, (w, w), 1) // hch
    toep = jnp.zeros((w, w), F32)
    for s in range(L):
        shifted = k_lag if s == 0 else jnp.concatenate([jnp.zeros((s * hch, w), F32), k_lag[:w - s * hch]], axis=0)
        toep = jnp.where(lane_s == s, shifted, toep)
    diag = lax.broadcasted_iota(jnp.int32, (w, w), 0) == lax.broadcasted_iota(jnp.int32, (w, w), 1)
    toept_ref[0] = (toep + jnp.where(diag, dskip_ref[0], 0.0)).astype(toept_ref.dtype)


def _s5_input_kernel(us_ref, ut_ref):
    ut = us_ref[...].astype(F32).T
    ut_ref[...] = ut.reshape(ut_ref.shape).astype(ut_ref.dtype)


def _s5_input(us, n_groups, bsz, seq):
    L = SSM_CHUNK
    nc = seq // L
    sw = us.shape[1] // L
    assert nc % LANES == 0
    return pl.pallas_call(
        _s5_input_kernel,
        grid=(bsz, L),
        in_specs=[pl.BlockSpec((nc, sw), lambda b, s: (b, s))],
        out_specs=pl.BlockSpec((n_groups, SSM_GROUP, nc), lambda b, s: (0, s, b)),
        out_shape=jax.ShapeDtypeStruct((n_groups, L * SSM_GROUP, bsz * nc), BF16),
        compiler_params=pltpu.CompilerParams(
            dimension_semantics=("parallel", "parallel"), vmem_limit_bytes=VMEM_LIMIT),
        name="s5_input",
    )(us)


def _s5_kernel(ut_ref, toept_ref, bpt_ref, bpst_ref, cpt_ref, lam_ref, yt_ref, s_ref, ss_ref, hp_ref, *, bsz, nc):
    groups = range(ut_ref.shape[0])
    for j in groups:
        ut = ut_ref[j]
        s_ref[j] = _dot(bpt_ref[j], ut).T
        ss_ref[j] = _dot(bpst_ref[j], ut).T
    a = [lam_ref[j, 0:1, :] for j in groups]
    bv = [lam_ref[j, 1:2, :] for j in groups]

    def body(c, carry):
        rows = pl.ds(c, bsz, stride=nc)
        out = []
        for j in groups:
            h, hs = carry[j]
            hp_ref[j, rows, :] = h
            out.append((a[j] * h + bv[j] * hs + s_ref[j, rows, :], a[j] * hs - bv[j] * h + ss_ref[j, rows, :]))
        return tuple(out)

    zero = jnp.zeros((bsz, 2 * SSM_STATE), F32)
    lax.fori_loop(0, nc, body, tuple((zero, zero) for _ in groups), unroll=8)
    for j in groups:
        y = _dot(toept_ref[j], ut_ref[j]) + _dot_nt(cpt_ref[j], hp_ref[j].astype(BF16))
        yt_ref[j] = y.astype(yt_ref.dtype)


def _s5_scan(ut, tables, bsz, seq):
    toept, bpt, bpst, cpt, lam = tables
    g, w, cols = ut.shape
    nc = seq // SSM_CHUNK
    p2 = 2 * SSM_STATE
    gs = S5_GROUPS_PER_STEP
    assert g % gs == 0
    grp = lambda i: (i, 0, 0)
    return pl.pallas_call(
        functools.partial(_s5_kernel, bsz=bsz, nc=nc),
        grid=(g // gs,),
        in_specs=[
            pl.BlockSpec((gs, w, cols), grp), pl.BlockSpec((gs, w, w), grp),
            pl.BlockSpec((gs, p2, w), grp), pl.BlockSpec((gs, p2, w), grp),
            pl.BlockSpec((gs, w, p2), grp), pl.BlockSpec((gs, 2, p2), grp),
        ],
        out_specs=pl.BlockSpec((gs, w, cols), grp),
        out_shape=jax.ShapeDtypeStruct((g, w, cols), BF16),
        scratch_shapes=[pltpu.VMEM((gs, cols, p2), F32), pltpu.VMEM((gs, cols, p2), F32),
                        pltpu.VMEM((gs, cols, p2), F32)],
        compiler_params=pltpu.CompilerParams(dimension_semantics=("parallel",), vmem_limit_bytes=VMEM_LIMIT),
        name="s5_scan",
    )(ut, toept, bpt, bpst, cpt, lam)


def _s5_post_kernel(yt_ref, wglut_ref, ss_ref):
    gh, nc = yt_ref.shape[0] * yt_ref.shape[1], yt_ref.shape[2]
    gt = jax.nn.gelu(yt_ref[...].astype(F32).reshape(gh, nc))
    ssm_t = gt * jax.nn.sigmoid(_dot(wglut_ref[...], gt.astype(BF16)))
    ss_ref[...] = ssm_t.T.astype(ss_ref.dtype)


def _s5_post(yt, w_glu_t, bsz, seq):
    g = yt.shape[0]
    L = SSM_CHUNK
    nc = seq // L
    sw = g * SSM_GROUP
    return pl.pallas_call(
        _s5_post_kernel,
        grid=(bsz, L),
        in_specs=[
            pl.BlockSpec((g, SSM_GROUP, nc), lambda b, s: (0, s, b)),
            pl.BlockSpec(w_glu_t.shape, lambda b, s: (0, 0)),
        ],
        out_specs=pl.BlockSpec((nc, sw), lambda b, s: (b, s)),
        out_shape=jax.ShapeDtypeStruct((bsz * nc, L * sw), BF16),
        compiler_params=pltpu.CompilerParams(
            dimension_semantics=("parallel", "parallel"), vmem_limit_bytes=VMEM_LIMIT),
        name="s5_post",
    )(yt, w_glu_t)


def _merge_kernel(x_ref, attn_ref, ss_ref, ga_ref, gs_ref, wap_ref, wsp_ref, wout_ref, g_ref, b_ref,
                  h_ref, slab_ref, *, alpha):
    ssm = _from_step_layout(ss_ref, slab_ref)
    merged = (ga_ref[...].astype(F32) * _dot(attn_ref[...], wap_ref[...])
              + gs_ref[...].astype(F32) * _dot(ssm.astype(BF16), wsp_ref[...]))
    mix = _dot(merged.astype(BF16), wout_ref[...])
    h_ref[...] = _layer_norm(alpha * x_ref[...] + mix, g_ref[...], b_ref[...])


def _merge(x2, attn, ss, ga, gs, w_ap, w_sp, w_out, ln_g, ln_b, alpha):
    n, dm = x2.shape
    aw = attn.shape[1]
    L = SSM_CHUNK
    sw = ss.shape[1] // L
    tm = ROW_TILE
    row = lambda i: (i, 0)
    full = lambda i: (0, 0)
    return pl.pallas_call(
        functools.partial(_merge_kernel, alpha=alpha),
        grid=(n // tm,),
        in_specs=[
            pl.BlockSpec((tm, dm), row), pl.BlockSpec((tm, aw), row), pl.BlockSpec((tm // L, L * sw), row),
            pl.BlockSpec((tm, dm), row), pl.BlockSpec((tm, dm), row),
            pl.BlockSpec(w_ap.shape, full), pl.BlockSpec(w_sp.shape, full),
            pl.BlockSpec(w_out.shape, full), pl.BlockSpec((1, dm), full), pl.BlockSpec((1, dm), full),
        ],
        out_specs=pl.BlockSpec((tm, dm), row),
        out_shape=jax.ShapeDtypeStruct((n, dm), F32),
        scratch_shapes=[pltpu.VMEM((sw // LANES, tm, LANES), F32)],
        compiler_params=pltpu.CompilerParams(dimension_semantics=("parallel",), vmem_limit_bytes=VMEM_LIMIT),
        name="merge_ln1",
    )(x2, attn, ss, ga, gs, w_ap, w_sp, w_out, ln_g, ln_b)


def _ffn_kernel(h_ref, wup_ref, cw_ref, wd_ref, g_ref, b_ref, o_ref, halo_ref, act_ref,
                *, alpha, tiles_per_seq, dff):
    i = pl.program_id(0)
    h = h_ref[...]
    hb = h.astype(BF16)
    tm = h.shape[0]
    ck = FF_CHUNK
    row8 = lax.broadcasted_iota(jnp.int32, (SUBLANES, ck), 0)

    @pl.when((i % tiles_per_seq) == 0)
    def _():
        halo_ref[...] = jnp.zeros_like(halo_ref)

    def conv(up, c0):
        cols = slice(c0, c0 + ck)
        prev = halo_ref[:, cols]
        halo_ref[:, cols] = up[tm - SUBLANES:]
        p1 = prev[SUBLANES - 1:SUBLANES]
        p2 = prev[SUBLANES - 2:SUBLANES - 1]
        r1 = pltpu.roll(up, 1, 0)
        r2 = pltpu.roll(up, 2, 0)
        d1 = jnp.concatenate([jnp.where(row8 == 0, p1, r1[:SUBLANES]), r1[SUBLANES:]], axis=0)
        d2 = jnp.concatenate([jnp.where(row8 == 0, p2, jnp.where(row8 == 1, p1, r2[:SUBLANES])), r2[SUBLANES:]],
                             axis=0)
        cw = cw_ref[:, cols]
        return cw[3:4] + cw[0:1] * d2 + cw[1:2] * d1 + cw[2:3] * up

    for j in range(dff // ck):
        c0 = j * ck
        val = conv(_dot(hb, wup_ref[:, c0:c0 + ck]), c0)
        gate = conv(_dot(hb, wup_ref[:, dff + c0:dff + c0 + ck]), dff + c0)
        act_ref[:, c0:c0 + ck] = (jax.nn.gelu(gate) * val).astype(BF16)
    ff = _dot(act_ref[...], wd_ref[...])
    o_ref[...] = _layer_norm(alpha * h + ff, g_ref[...], b_ref[...])


def _ffn(h, w_up, conv_w, conv_b, w_down, ln_g, ln_b, alpha, seq):
    n, dm = h.shape
    dff = w_down.shape[0]
    tm = FFN_TILE
    assert dff % FF_CHUNK == 0 and seq % tm == 0
    cw = jnp.concatenate([conv_w, conv_b[None, :]], axis=0)
    cw = jnp.pad(cw, ((0, SUBLANES - cw.shape[0]), (0, 0)))
    row = lambda i: (i, 0)
    full = lambda i: (0, 0)
    once = pl.Buffered(1)
    return pl.pallas_call(
        functools.partial(_ffn_kernel, alpha=alpha, tiles_per_seq=seq // tm, dff=dff),
        grid=(n // tm,),
        in_specs=[
            pl.BlockSpec((tm, dm), row),
            pl.BlockSpec(w_up.shape, full, pipeline_mode=once),
            pl.BlockSpec(cw.shape, full),
            pl.BlockSpec(w_down.shape, full, pipeline_mode=once),
            pl.BlockSpec((1, dm), full), pl.BlockSpec((1, dm), full),
        ],
        out_specs=pl.BlockSpec((tm, dm), row),
        out_shape=jax.ShapeDtypeStruct((n, dm), F32),
        scratch_shapes=[
            pltpu.VMEM((SUBLANES, 2 * dff), F32),
            pltpu.VMEM((tm, dff), BF16),
        ],
        compiler_params=pltpu.CompilerParams(dimension_semantics=("arbitrary",), vmem_limit_bytes=VMEM_LIMIT),
        name="conv_ffn_ln2",
    )(h, w_up, cw, w_down, ln_g, ln_b)


def kernel(x, w_in, w_attn_proj, ssm_a_re, ssm_a_im, ssm_log_dt, ssm_b_re, ssm_b_im, ssm_c_re, ssm_c_im, ssm_d,
           w_glu, w_ssm_proj, w_out, ln1_g, ln1_b, w_up, conv_w, conv_b, w_down, ln2_g, ln2_b):
    bsz, seq, dm = x.shape
    depth = w_in.shape[0]
    alpha = (2.0 * depth) ** 0.25
    n = bsz * seq
    assert seq % ROW_TILE == 0 and ROW_TILE % MOBA_BLOCK == 0 and seq % SSM_CHUNK == 0
    h = x
    for l in range(depth):
        w_in_b = w_in[l].astype(BF16)
        h2 = h.reshape(n, dm)
        q, k, vt, us, ga, gs, kmean = _in_projection(h2, w_in[l].astype(BF16), seq)
        aw = q.shape[1]
        nb = seq // MOBA_BLOCK
        q3 = q.reshape(bsz, seq, aw)
        sel = _moba_select(q3, kmean.reshape(bsz, nb, aw))
        attn = _moba_attention(q3, k.reshape(bsz, seq, aw), vt.reshape(bsz, nb, *vt.shape[1:]), sel).reshape(n, aw)
        tables = _s5_tables(ssm_a_re[l], ssm_a_im[l], ssm_log_dt[l], ssm_b_re[l], ssm_b_im[l],
                            ssm_c_re[l], ssm_c_im[l], ssm_d[l])
        ut = _s5_input(us, ssm_a_re.shape[1], bsz, seq)
        ss = _s5_post(_s5_scan(ut, tables, bsz, seq), w_glu[l].T.astype(BF16), bsz, seq)
        h1 = _merge(h2, attn, ss, ga, gs, w_attn_proj[l].astype(BF16), w_ssm_proj[l].astype(BF16),
                    w_out[l].astype(BF16), ln1_g[l][None], ln1_b[l][None], alpha)
        h = _ffn(h1, w_up[l].astype(BF16), conv_w[l], conv_b[l], w_down[l].astype(BF16),
                 ln2_g[l][None], ln2_b[l][None], alpha, seq).reshape(bsz, seq, dm)
    return h
```

```python
import functools
import math

import jax
import jax.numpy as jnp
from jax import lax
from jax.experimental import pallas as pl
from jax.experimental.pallas import tpu as pltpu

F32 = jnp.float32
BF16 = jnp.bfloat16

N_HEADS = 8
HEAD_DIM = 64
ROT_DIM = HEAD_DIM // 4
ROPE_THETA = 500000.0
MOBA_BLOCK = 256
MOBA_TOP_K = 3
SSM_GROUP = 16
SSM_STATE = 64
CONV_WIDTH = 3
LN_EPS = 1e-5
NEG_INF = -1e30

LANES = 128
SUBLANES = 8
BF16_ROWS = 16
VMEM_LIMIT = 56 * 1024 * 1024

SSM_CHUNK = 16
S5_GROUPS_PER_STEP = 4
ROW_TILE = 512
IN_TILE = 1024
FFN_TILE = 1024
FF_CHUNK = 256

_NT = (((1,), (1,)), ((), ()))


def _dot(a, b):
    return jnp.dot(a, b, preferred_element_type=F32)


def _dot_nt(a, b):
    return lax.dot_general(a, b, _NT, preferred_element_type=F32)


def _layer_norm(t, g, b):
    mu = jnp.mean(t, axis=-1, keepdims=True)
    d = t - mu
    var = jnp.mean(d * d, axis=-1, keepdims=True)
    return d * lax.rsqrt(var + LN_EPS) * g + b


def _to_step_layout(dst_ref, src, slab_ref):
    chunks, width = dst_ref.shape[0], src.shape[1]
    for j in range(width // LANES):
        slab_ref[j] = src[:, j * LANES:(j + 1) * LANES]
    for s in range(SSM_CHUNK):
        for j in range(width // LANES):
            c0 = s * width + j * LANES
            dst_ref[:, c0:c0 + LANES] = slab_ref[j, pl.ds(s, chunks, stride=SSM_CHUNK), :].astype(dst_ref.dtype)


def _from_step_layout(src_ref, slab_ref):
    chunks, width = src_ref.shape[0], slab_ref.shape[0] * LANES
    for s in range(SSM_CHUNK):
        for j in range(width // LANES):
            c0 = s * width + j * LANES
            slab_ref[j, pl.ds(s, chunks, stride=SSM_CHUNK), :] = src_ref[:, c0:c0 + LANES].astype(slab_ref.dtype)
    return jnp.concatenate([slab_ref[j] for j in range(width // LANES)], axis=1)


def _inproj_kernel(x_ref, w_ref, wvt_ref, cos_ref, s1_ref, s2_ref,
                   q_ref, k_ref, vt_ref, us_ref, ga_ref, gs_ref, km_ref, tmp_ref, *, aw, sw, dm, scale):
    xb = x_ref[...].astype(BF16)
    cos_t, s1_t, s2_t = cos_ref[...], s1_ref[...], s2_ref[...]
    tm = xb.shape[0]

    def rotary(t):
        return t * cos_t + pltpu.roll(t, ROT_DIM // 2, 1) * s1_t + pltpu.roll(t, LANES - ROT_DIM // 2, 1) * s2_t

    q = _dot(xb, w_ref[:, 0:aw])
    for j in range(aw // LANES):
        sl = slice(j * LANES, (j + 1) * LANES)
        q_ref[:, sl] = (rotary(q[:, sl]) * scale).astype(BF16)
    k = _dot(xb, w_ref[:, aw:2 * aw])
    for j in range(aw // LANES):
        sl = slice(j * LANES, (j + 1) * LANES)
        kr = rotary(k[:, sl])
        k_ref[:, sl] = kr.astype(BF16)
        for r in range(tm // MOBA_BLOCK):
            km_ref[r, :, sl] = jnp.mean(kr[r * MOBA_BLOCK:(r + 1) * MOBA_BLOCK], axis=0, keepdims=True)
    vt = _dot_nt(wvt_ref[...], xb)
    for r in range(tm // MOBA_BLOCK):
        for j in range(aw // LANES):
            vt_ref[r, j] = vt[j * LANES:(j + 1) * LANES, r * MOBA_BLOCK:(r + 1) * MOBA_BLOCK].astype(BF16)
    _to_step_layout(us_ref, _dot(xb, w_ref[:, 3 * aw:3 * aw + sw]), tmp_ref)
    c0 = 3 * aw + sw
    ga_ref[...] = jax.nn.sigmoid(_dot(xb, w_ref[:, c0:c0 + dm])).astype(BF16)
    gs_ref[...] = jax.nn.sigmoid(_dot(xb, w_ref[:, c0 + dm:c0 + 2 * dm])).astype(BF16)


def _rotary_tables(seq):
    half = ROT_DIM // 2
    inv_freq = ROPE_THETA ** (-jnp.arange(0, ROT_DIM, 2, dtype=F32) / ROT_DIM)
    ang = jnp.arange(seq, dtype=F32)[:, None] * inv_freq[None, :]
    cos, sin = jnp.cos(ang), jnp.sin(ang)
    ones = jnp.ones((seq, HEAD_DIM - ROT_DIM), F32)
    zeros = jnp.zeros((seq, HEAD_DIM - ROT_DIM), F32)
    zh = jnp.zeros((seq, half), F32)
    cos_h = jnp.concatenate([cos, cos, ones], axis=1)
    s1_h = jnp.concatenate([zh, sin, zeros], axis=1)
    s2_h = jnp.concatenate([-sin, zh, zeros], axis=1)
    rep = LANES // HEAD_DIM
    return tuple(jnp.tile(t, (1, rep)) for t in (cos_h, s1_h, s2_h))


def _in_projection(x2, w_in, seq):
    n, dm = x2.shape
    aw, sw = N_HEADS * HEAD_DIM, w_in.shape[1] - 3 * N_HEADS * HEAD_DIM - 2 * dm
    tm = IN_TILE
    assert seq % tm == 0 and tm % MOBA_BLOCK == 0
    cos_t, s1_t, s2_t = _rotary_tables(seq)
    w_vt = w_in[:, 2 * aw:3 * aw].T
    hp = aw // LANES
    tiles_per_seq = seq // tm
    row = lambda i: (i, 0)
    tab = lambda i: (i % tiles_per_seq, 0)
    L = SSM_CHUNK
    step_rows = lambda i: (i, 0)
    out_shape = (
        jax.ShapeDtypeStruct((n, aw), BF16), jax.ShapeDtypeStruct((n, aw), BF16),
        jax.ShapeDtypeStruct((n // MOBA_BLOCK, hp, LANES, MOBA_BLOCK), BF16),
        jax.ShapeDtypeStruct((n // L, L * sw), BF16),
        jax.ShapeDtypeStruct((n, dm), BF16), jax.ShapeDtypeStruct((n, dm), BF16),
        jax.ShapeDtypeStruct((n // MOBA_BLOCK, 1, aw), F32),
    )
    return pl.pallas_call(
        functools.partial(_inproj_kernel, aw=aw, sw=sw, dm=dm, scale=HEAD_DIM ** -0.5 * math.log2(math.e)),
        grid=(n // tm,),
        in_specs=[
            pl.BlockSpec((tm, dm), row),
            pl.BlockSpec(w_in.shape, lambda i: (0, 0)),
            pl.BlockSpec(w_vt.shape, lambda i: (0, 0)),
            pl.BlockSpec((tm, LANES), tab), pl.BlockSpec((tm, LANES), tab), pl.BlockSpec((tm, LANES), tab),
        ],
        out_specs=(
            pl.BlockSpec((tm, aw), row), pl.BlockSpec((tm, aw), row),
            pl.BlockSpec((tm // MOBA_BLOCK, hp, LANES, MOBA_BLOCK), lambda i: (i, 0, 0, 0)),
            pl.BlockSpec((tm // L, L * sw), step_rows),
            pl.BlockSpec((tm, dm), row), pl.BlockSpec((tm, dm), row),
            pl.BlockSpec((tm // MOBA_BLOCK, 1, aw), lambda i: (i, 0, 0)),
        ),
        out_shape=out_shape,
        scratch_shapes=[pltpu.VMEM((sw // LANES, tm, LANES), F32)],
        compiler_params=pltpu.CompilerParams(dimension_semantics=("parallel",), vmem_limit_bytes=VMEM_LIMIT),
        name="in_projection",
    )(x2, w_in, w_vt, cos_t, s1_t, s2_t)


PAIR = 2 * MOBA_BLOCK
Q_TILE = PAIR
MASKED = 2.0 * NEG_INF
V_ROWS = HEAD_DIM + BF16_ROWS
SEL_ROWS = SUBLANES
SEL_CHUNK = 1024


def _head_split(q2):
    qf = q2.astype(F32)
    lane = lax.broadcasted_iota(jnp.int32, qf.shape, 1)
    return (jnp.where(lane < HEAD_DIM, qf, 0.0).astype(BF16), jnp.where(lane >= HEAD_DIM, qf, 0.0).astype(BF16))


def _select_kernel(q_ref, km_ref, sel_ref, *, nb):
    kmb = km_ref[0].astype(BF16)
    seq = q_ref.shape[1]
    shape = (nb, SEL_CHUNK)
    blk = lax.broadcasted_iota(jnp.int32, shape, 0)
    for c in range(seq // SEL_CHUNK):
        cols = slice(c * SEL_CHUNK, (c + 1) * SEL_CHUNK)
        own = (lax.broadcasted_iota(jnp.int32, shape, 1) + c * SEL_CHUNK) // MOBA_BLOCK
        qh = _head_split(q_ref[0, cols, :])
        for h in range(2):
            g = jnp.where(blk < own, _dot_nt(kmb, qh[h]), NEG_INF)
            rows = []
            for _ in range(min(MOBA_TOP_K, nb)):
                mx = jnp.max(g, axis=0, keepdims=True)
                first = jnp.min(jnp.where(g == mx, blk, nb), axis=0, keepdims=True)
                rows.append(jnp.where(first < own[0:1], first, -1))
                g = jnp.where(blk == first, -jnp.inf, g)
            rows.append(jnp.full((SEL_ROWS - len(rows), SEL_CHUNK), -1, jnp.int32))
            sel_ref[0, 0, h, :, cols] = jnp.concatenate(rows, axis=0)


def _moba_select(q, kmean):
    bsz, seq, aw = q.shape
    nb = seq // MOBA_BLOCK
    hp = aw // LANES
    assert seq % SEL_CHUNK == 0
    return pl.pallas_call(
        functools.partial(_select_kernel, nb=nb),
        grid=(bsz, hp),
        in_specs=[
            pl.BlockSpec((1, seq, LANES), lambda b, h: (b, 0, h)),
            pl.BlockSpec((1, nb, LANES), lambda b, h: (b, 0, h)),
        ],
        out_specs=pl.BlockSpec((1, 1, 2, SEL_ROWS, seq), lambda b, h: (b, h, 0, 0, 0)),
        out_shape=jax.ShapeDtypeStruct((bsz, hp, 2, SEL_ROWS, seq), jnp.int32),
        compiler_params=pltpu.CompilerParams(
            dimension_semantics=("parallel", "parallel"), vmem_limit_bytes=VMEM_LIMIT),
        name="moba_select",
    )(q, kmean)


def _attn_kernel(q_ref, k_ref, vt_ref, sel_ref, o_ref,
                 va_ref, vb_ref, s0_ref, s1_ref, p0_ref, p1_ref, cm_ref, al_ref, acc_ref, *, nb):
    i = pl.program_id(2)
    blk = MOBA_BLOCK
    v_refs = (va_ref, vb_ref)
    s_refs = (s0_ref, s1_ref)
    p_refs = (p0_ref, p1_ref)

    @pl.when(i == 0)
    def _():
        ones = jnp.ones((V_ROWS - HEAD_DIM, blk), BF16)

        def fill(j, c):
            for half in range(2):
                vt = vt_ref[0, 2 * j + half, 0]
                cols = slice(half * blk, (half + 1) * blk)
                for h in range(2):
                    v_refs[h][j, 0:HEAD_DIM, cols] = vt[h * HEAD_DIM:(h + 1) * HEAD_DIM]
                    v_refs[h][j, HEAD_DIM:V_ROWS, cols] = ones
            return c
        lax.fori_loop(0, nb // 2, fill, 0)

    qh = _head_split(q_ref[0])
    sel = (sel_ref[0, 0, 0], sel_ref[0, 0, 1])

    def chosen(h, n):
        sv = sel[h]
        hit = sv[0:1] == n
        for r in range(1, MOBA_TOP_K):
            hit = jnp.logical_or(hit, sv[r:r + 1] == n)
        return hit

    def past_bias(pair):
        return [tuple(jnp.where(chosen(h, 2 * pair + half), 0.0, MASKED) for half in range(2)) for h in range(2)]

    key_i = lax.broadcasted_iota(jnp.int32, (blk, blk), 0)
    qry_i = lax.broadcasted_iota(jnp.int32, (blk, blk), 1)
    causal = key_i <= qry_i

    def own_mask(s, h):
        top = jnp.concatenate([jnp.where(causal, s[:blk, :blk], MASKED),
                               jnp.where(chosen(h, 2 * i)[:, blk:], s[:blk, blk:], MASKED)], axis=1)
        bot = jnp.concatenate([jnp.full((blk, blk), MASKED, F32),
                               jnp.where(causal, s[blk:, blk:], MASKED)], axis=1)
        return jnp.concatenate([top, bot], axis=0)

    def stage_a(pair, slot, own):
        kp = k_ref[0, pl.ds(pl.multiple_of(pair * PAIR, PAIR), PAIR), :]
        for h in range(2):
            s = _dot_nt(kp, qh[h])
            if own:
                s = own_mask(s, h)
            s_refs[slot][h] = s
            cm_ref[slot, 2 * h:2 * h + 1, :] = jnp.max(s[:blk], axis=0, keepdims=True)
            cm_ref[slot, 2 * h + 1:2 * h + 2, :] = jnp.max(s[blk:], axis=0, keepdims=True)

    def stage_b(slot, m, bias):
        m_out = []
        for h in range(2):
            cm0 = cm_ref[slot, 2 * h:2 * h + 1, :]
            cm1 = cm_ref[slot, 2 * h + 1:2 * h + 2, :]
            if bias is None:
                m_new = jnp.maximum(m[h], jnp.maximum(cm0, cm1))
                c0 = c1 = m_new
            else:
                b0, b1 = bias[h]
                m_new = jnp.maximum(m[h], jnp.maximum(cm0 + b0, cm1 + b1))
                c0, c1 = m_new - b0, m_new - b1
            al_ref[slot, h:h + 1, :] = jnp.exp2(m[h] - m_new)
            p_refs[slot][h, 0:blk] = jnp.exp2(s_refs[slot][h, 0:blk] - c0).astype(BF16)
            p_refs[slot][h, blk:PAIR] = jnp.exp2(s_refs[slot][h, blk:PAIR] - c1).astype(BF16)
            m_out.append(m_new)
        return tuple(m_out)

    def stage_c(pair, slot):
        for h in range(2):
            pv = _dot(v_refs[h][pair], p_refs[slot][h])
            acc_ref[h] = acc_ref[h] * al_ref[slot, h:h + 1, :] + pv

    def trip(t, slot, m):
        stage_a(t - 1, slot, False)
        m = stage_b(1 - slot, m, past_bias(t - 2))
        stage_c(jnp.where(t == 2, i, t - 3), slot)
        return m

    acc_ref[...] = jnp.zeros_like(acc_ref)
    floor = jnp.full((1, Q_TILE), NEG_INF, F32)
    stage_a(i, 0, True)
    stage_a(0, 1, False)
    m0 = stage_b(0, (floor, floor), None)
    n_list = i + 1

    @pl.when(i == 0)
    def _():
        stage_c(i, 0)

    @pl.when(i > 0)
    def _():
        def double_trip(d, m):
            t = 2 + 2 * d
            return trip(t + 1, 1, trip(t, 0, m))

        m1 = lax.fori_loop(0, (n_list - 2) // 2, double_trip, m0)

        @pl.when(n_list % 2 == 1)
        def _():
            m2 = trip(n_list - 1, 0, m1)
            stage_b(0, m2, past_bias(n_list - 2))
            stage_c(n_list - 3, 1)
            stage_c(n_list - 2, 0)

        @pl.when(n_list % 2 == 0)
        def _():
            stage_b(1, m1, past_bias(n_list - 2))
            stage_c(jnp.where(n_list == 2, i, n_list - 3), 0)
            stage_c(n_list - 2, 1)

    outs = []
    for h in range(2):
        acc = acc_ref[h]
        outs.append(acc[:HEAD_DIM] / acc[HEAD_DIM:HEAD_DIM + 1])
    o_ref[0] = jnp.concatenate(outs, axis=0).T.astype(o_ref.dtype)


def _moba_attention(q, k, vt, sel):
    bsz, seq, aw = q.shape
    nb = seq // MOBA_BLOCK
    hp = aw // LANES
    assert nb % 2 == 0
    return pl.pallas_call(
        functools.partial(_attn_kernel, nb=nb),
        grid=(bsz, hp, nb // 2),
        in_specs=[
            pl.BlockSpec((1, Q_TILE, LANES), lambda b, h, i: (b, i, h)),
            pl.BlockSpec((1, seq, LANES), lambda b, h, i: (b, 0, h)),
            pl.BlockSpec((1, nb, 1, LANES, MOBA_BLOCK), lambda b, h, i: (b, 0, h, 0, 0)),
            pl.BlockSpec((1, 1, 2, SEL_ROWS, Q_TILE), lambda b, h, i: (b, h, 0, 0, i)),
        ],
        out_specs=pl.BlockSpec((1, Q_TILE, LANES), lambda b, h, i: (b, i, h)),
        out_shape=jax.ShapeDtypeStruct((bsz, seq, aw), BF16),
        scratch_shapes=[
            pltpu.VMEM((nb // 2, V_ROWS, PAIR), BF16), pltpu.VMEM((nb // 2, V_ROWS, PAIR), BF16),
            pltpu.VMEM((2, PAIR, Q_TILE), F32), pltpu.VMEM((2, PAIR, Q_TILE), F32),
            pltpu.VMEM((2, PAIR, Q_TILE), BF16), pltpu.VMEM((2, PAIR, Q_TILE), BF16),
            pltpu.VMEM((2, SUBLANES, Q_TILE), F32),
            pltpu.VMEM((2, SUBLANES, Q_TILE), F32),
            pltpu.VMEM((2, V_ROWS, Q_TILE), F32),
        ],
        compiler_params=pltpu.CompilerParams(
            dimension_semantics=("parallel", "parallel", "arbitrary"), vmem_limit_bytes=VMEM_LIMIT),
        name="moba_attention",
    )(q, k, vt, sel)


def _s5_tables(a_re, a_im, log_dt, b_re, b_im, c_re, c_im, d_skip):
    g, p = a_re.shape
    L, hch = SSM_CHUNK, SSM_GROUP
    w = L * hch
    two = lambda t: jnp.concatenate([t, t], axis=-1)
    operands = (
        log_dt.reshape(g, 1, 1),
        two(a_re).reshape(g, 1, 2 * p), two(a_im).reshape(g, 1, 2 * p),
        a_re.reshape(g, p, 1), a_im.reshape(g, p, 1),
        jnp.tile(b_re, (1, 1, L)), jnp.tile(b_im, (1, 1, L)),
        two(c_re), two(c_im),
        jnp.tile(d_skip, (1, L)).reshape(g, 1, w),
    )
    grp = lambda i: (i, 0, 0)
    out_shape = (
        jax.ShapeDtypeStruct((g, w, w), BF16), jax.ShapeDtypeStruct((g, 2 * p, w), BF16),
        jax.ShapeDtypeStruct((g, 2 * p, w), BF16), jax.ShapeDtypeStruct((g, w, 2 * p), BF16),
        jax.ShapeDtypeStruct((g, 2, 2 * p), F32),
    )
    return pl.pallas_call(
        _s5_tables_kernel,
        grid=(g,),
        in_specs=[pl.BlockSpec((1,) + t.shape[1:], grp) for t in operands],
        out_specs=tuple(pl.BlockSpec((1,) + t.shape[1:], grp) for t in out_shape),
        out_shape=out_shape,
        compiler_params=pltpu.CompilerParams(dimension_semantics=("parallel",), vmem_limit_bytes=VMEM_LIMIT),
        name="s5_tables",
    )(*operands)


def _s5_tables_kernel(ldt_ref, ar2_ref, ai2_ref, arc_ref, aic_ref, btr_ref, bti_ref, c2r_ref, c2i_ref, dskip_ref,
                      toept_ref, bpt_ref, bpst_ref, cpt_ref, lam_ref):
    L, hch, p = SSM_CHUNK, SSM_GROUP, SSM_STATE
    w = L * hch
    dt = jnp.exp(ldt_ref[0])

    ar, ai = ar2_ref[0] * dt, ai2_ref[0] * dt
    re_half = lax.broadcasted_iota(jnp.int32, (1, 2 * p), 1) < p

    def lam_pow_rows(e):
        mag, ang = jnp.exp(e * ar), e * ai
        return mag * jnp.cos(ang), mag * jnp.sin(ang)

    c_re = jnp.concatenate([c2r_ref[0]] * L, axis=0)
    c_im = jnp.concatenate([c2i_ref[0]] * L, axis=0)
    x_c = jnp.where(re_half, c_re, -c_im)
    y_c = jnp.where(re_half, -c_im, -c_re)
    step = lax.broadcasted_iota(jnp.int32, (L, 1), 0).astype(F32)

    def per_channel(t):
        return jnp.broadcast_to(t[:, None, :], (L, hch, 2 * p)).reshape(w, 2 * p)

    pr, pi = lam_pow_rows(step + 1.0)
    cpt_ref[0] = (x_c * per_channel(pr) + y_c * per_channel(pi)).astype(cpt_ref.dtype)
    pr, pi = lam_pow_rows(step)
    c_lag = x_c * per_channel(pr) + y_c * per_channel(pi)
    pr, pi = lam_pow_rows(jnp.full((1, 1), float(L), F32))
    lam_ref[0, 0:1, :] = pr
    lam_ref[0, 1:2, :] = jnp.where(re_half, -pi, pi)

    a_r, a_i = arc_ref[0], aic_ref[0]
    mag = jnp.exp(a_r * dt)
    n_re, n_im = mag * jnp.cos(a_i * dt) - 1.0, mag * jnp.sin(a_i * dt)
    den = a_r * a_r + a_i * a_i
    z_re = (n_re * a_r + n_im * a_i) / den
    z_im = (n_im * a_r - n_re * a_i) / den
    b_r, b_i = btr_ref[0], bti_ref[0]
    zb_re = z_re * b_r - z_im * b_i
    zb_im = z_re * b_i + z_im * b_r
    e = (L - 1 - lax.broadcasted_iota(jnp.int32, (1, L), 1)).astype(F32)
    mag, ang = jnp.exp(e * (a_r * dt)), e * (a_i * dt)
    spread = (lax.broadcasted_iota(jnp.int32, (L, w), 1) // hch
              == lax.broadcasted_iota(jnp.int32, (L, w), 0)).astype(F32)
    q_re = jnp.dot(mag * jnp.cos(ang), spread, preferred_element_type=F32, precision=lax.Precision.HIGHEST)
    q_im = jnp.dot(mag * jnp.sin(ang), spread, preferred_element_type=F32, precision=lax.Precision.HIGHEST)
    w_re = q_re * zb_re - q_im * zb_im
    w_im = q_re * zb_im + q_im * zb_re
    bpt_ref[0] = jnp.concatenate([w_re, w_im], axis=0).astype(bpt_ref.dtype)
    bpst_ref[0] = jnp.concatenate([w_im, w_re], axis=0).astype(bpst_ref.dtype)

    k_lag = jnp.dot(c_lag, jnp.concatenate([zb_re, zb_im], axis=0),
                    preferred_element_type=F32, precision=lax.Precision.HIGHEST)
    lane_s = lax.broadcasted_iota(jnp.int32, (w, w), 1) // hch
    toep = jnp.zeros((w, w), F32)
    for s in range(L):
        shifted = k_lag if s == 0 else jnp.concatenate([jnp.zeros((s * hch, w), F32), k_lag[:w - s * hch]], axis=0)
        toep = jnp.where(lane_s == s, shifted, toep)
    diag = lax.broadcasted_iota(jnp.int32, (w, w), 0) == lax.broadcasted_iota(jnp.int32, (w, w), 1)
    toept_ref[0] = (toep + jnp.where(diag, dskip_ref[0], 0.0)).astype(toept_ref.dtype)


def _s5_input_kernel(us_ref, ut_ref):
    ut = us_ref[...].astype(F32).T
    ut_ref[...] = ut.reshape(ut_ref.shape).astype(ut_ref.dtype)


def _s5_input(us, n_groups, bsz, seq):
    L = SSM_CHUNK
    nc = seq // L
    sw = us.shape[1] // L
    assert nc % LANES == 0
    return pl.pallas_call(
        _s5_input_kernel,
        grid=(bsz, L),
        in_specs=[pl.BlockSpec((nc, sw), lambda b, s: (b, s))],
        out_specs=pl.BlockSpec((n_groups, SSM_GROUP, nc), lambda b, s: (0, s, b)),
        out_shape=jax.ShapeDtypeStruct((n_groups, L * SSM_GROUP, bsz * nc), BF16),
        compiler_params=pltpu.CompilerParams(
            dimension_semantics=("parallel", "parallel"), vmem_limit_bytes=VMEM_LIMIT),
        name="s5_input",
    )(us)


def _s5_kernel(ut_ref, toept_ref, bpt_ref, bpst_ref, cpt_ref, lam_ref, yt_ref, s_ref, ss_ref, hp_ref, *, bsz, nc):
    groups = range(ut_ref.shape[0])
    for j in groups:
        ut = ut_ref[j]
        s_ref[j] = _dot(bpt_ref[j], ut).T
        ss_ref[j] = _dot(bpst_ref[j], ut).T
    a = [lam_ref[j, 0:1, :] for j in groups]
    bv = [lam_ref[j, 1:2, :] for j in groups]

    def body(c, carry):
        rows = pl.ds(c, bsz, stride=nc)
        out = []
        for j in groups:
            h, hs = carry[j]
            hp_ref[j, rows, :] = h
            out.append((a[j] * h + bv[j] * hs + s_ref[j, rows, :], a[j] * hs - bv[j] * h + ss_ref[j, rows, :]))
        return tuple(out)

    zero = jnp.zeros((bsz, 2 * SSM_STATE), F32)
    lax.fori_loop(0, nc, body, tuple((zero, zero) for _ in groups), unroll=8)
    for j in groups:
        y = _dot(toept_ref[j], ut_ref[j]) + _dot_nt(cpt_ref[j], hp_ref[j].astype(BF16))
        yt_ref[j] = y.astype(yt_ref.dtype)


def _s5_scan(ut, tables, bsz, seq):
    toept, bpt, bpst, cpt, lam = tables
    g, w, cols = ut.shape
    nc = seq // SSM_CHUNK
    p2 = 2 * SSM_STATE
    gs = S5_GROUPS_PER_STEP
    assert g % gs == 0
    grp = lambda i: (i, 0, 0)
    return pl.pallas_call(
        functools.partial(_s5_kernel, bsz=bsz, nc=nc),
        grid=(g // gs,),
        in_specs=[
            pl.BlockSpec((gs, w, cols), grp), pl.BlockSpec((gs, w, w), grp),
            pl.BlockSpec((gs, p2, w), grp), pl.BlockSpec((gs, p2, w), grp),
            pl.BlockSpec((gs, w, p2), grp), pl.BlockSpec((gs, 2, p2), grp),
        ],
        out_specs=pl.BlockSpec((gs, w, cols), grp),
        out_shape=jax.ShapeDtypeStruct((g, w, cols), BF16),
        scratch_shapes=[pltpu.VMEM((gs, cols, p2), F32), pltpu.VMEM((gs, cols, p2), F32),
                        pltpu.VMEM((gs, cols, p2), F32)],
        compiler_params=pltpu.CompilerParams(dimension_semantics=("parallel",), vmem_limit_bytes=VMEM_LIMIT),
        name="s5_scan",
    )(ut, toept, bpt, bpst, cpt, lam)


def _s5_post_kernel(yt_ref, wglut_ref, ss_ref):
    gh, nc = yt_ref.shape[0] * yt_ref.shape[1], yt_ref.shape[2]
    gt = jax.nn.gelu(yt_ref[...].astype(F32).reshape(gh, nc))
    ssm_t = gt * jax.nn.sigmoid(_dot(wglut_ref[...], gt.astype(BF16)))
    ss_ref[...] = ssm_t.T.astype(ss_ref.dtype)


def _s5_post(yt, w_glu_t, bsz, seq):
    g = yt.shape[0]
    L = SSM_CHUNK
    nc = seq // L
    sw = g * SSM_GROUP
    return pl.pallas_call(
        _s5_post_kernel,
        grid=(bsz, L),
        in_specs=[
            pl.BlockSpec((g, SSM_GROUP, nc), lambda b, s: (0, s, b)),
            pl.BlockSpec(w_glu_t.shape, lambda b, s: (0, 0)),
        ],
        out_specs=pl.BlockSpec((nc, sw), lambda b, s: (b, s)),
        out_shape=jax.ShapeDtypeStruct((bsz * nc, L * sw), BF16),
        compiler_params=pltpu.CompilerParams(
            dimension_semantics=("parallel", "parallel"), vmem_limit_bytes=VMEM_LIMIT),
        name="s5_post",
    )(yt, w_glu_t)


def _merge_kernel(x_ref, attn_ref, ss_ref, ga_ref, gs_ref, wap_ref, wsp_ref, wout_ref, g_ref, b_ref,
                  h_ref, slab_ref, *, alpha):
    ssm = _from_step_layout(ss_ref, slab_ref)
    merged = (ga_ref[...].astype(F32) * _dot(attn_ref[...], wap_ref[...])
              + gs_ref[...].astype(F32) * _dot(ssm.astype(BF16), wsp_ref[...]))
    mix = _dot(merged.astype(BF16), wout_ref[...])
    h_ref[...] = _layer_norm(alpha * x_ref[...] + mix, g_ref[...], b_ref[...])


def _merge(x2, attn, ss, ga, gs, w_ap, w_sp, w_out, ln_g, ln_b, alpha):
    n, dm = x2.shape
    aw = attn.shape[1]
    L = SSM_CHUNK
    sw = ss.shape[1] // L
    tm = ROW_TILE
    row = lambda i: (i, 0)
    full = lambda i: (0, 0)
    return pl.pallas_call(
        functools.partial(_merge_kernel, alpha=alpha),
        grid=(n // tm,),
        in_specs=[
            pl.BlockSpec((tm, dm), row), pl.BlockSpec((tm, aw), row), pl.BlockSpec((tm // L, L * sw), row),
            pl.BlockSpec((tm, dm), row), pl.BlockSpec((tm, dm), row),
            pl.BlockSpec(w_ap.shape, full), pl.BlockSpec(w_sp.shape, full),
            pl.BlockSpec(w_out.shape, full), pl.BlockSpec((1, dm), full), pl.BlockSpec((1, dm), full),
        ],
        out_specs=pl.BlockSpec((tm, dm), row),
        out_shape=jax.ShapeDtypeStruct((n, dm), F32),
        scratch_shapes=[pltpu.VMEM((sw // LANES, tm, LANES), F32)],
        compiler_params=pltpu.CompilerParams(dimension_semantics=("parallel",), vmem_limit_bytes=VMEM_LIMIT),
        name="merge_ln1",
    )(x2, attn, ss, ga, gs, w_ap, w_sp, w_out, ln_g, ln_b)


def _ffn_kernel(h_ref, wup_ref, cw_ref, wd_ref, g_ref, b_ref, o_ref, halo_ref, act_ref,
                *, alpha, tiles_per_seq, dff):
    i = pl.program_id(0)
    h = h_ref[...]
    hb = h.astype(BF16)
    tm = h.shape[0]
    ck = FF_CHUNK
    row8 = lax.broadcasted_iota(jnp.int32, (SUBLANES, ck), 0)

    @pl.when((i % tiles_per_seq) == 0)
    def _():
        halo_ref[...] = jnp.zeros_like(halo_ref)

    def conv(up, c0):
        cols = slice(c0, c0 + ck)
        prev = halo_ref[:, cols]
        halo_ref[:, cols] = up[tm - SUBLANES:]
        p1 = prev[SUBLANES - 1:SUBLANES]
        p2 = prev[SUBLANES - 2:SUBLANES - 1]
        r1 = pltpu.roll(up, 1, 0)
        r2 = pltpu.roll(up, 2, 0)
        d1 = jnp.concatenate([jnp.where(row8 == 0, p1, r1[:SUBLANES]), r1[SUBLANES:]], axis=0)
        d2 = jnp.concatenate([jnp.where(row8 == 0, p2, jnp.where(row8 == 1, p1, r2[:SUBLANES])), r2[SUBLANES:]],
                             axis=0)
        cw = cw_ref[:, cols]
        return cw[3:4] + cw[0:1] * d2 + cw[1:2] * d1 + cw[2:3] * up

    for j in range(dff // ck):
        c0 = j * ck
        val = conv(_dot(hb, wup_ref[:, c0:c0 + ck]), c0)
        gate = conv(_dot(hb, wup_ref[:, dff + c0:dff + c0 + ck]), dff + c0)
        act_ref[:, c0:c0 + ck] = (jax.nn.gelu(gate) * val).astype(BF16)
    ff = _dot(act_ref[...], wd_ref[...])
    o_ref[...] = _layer_norm(alpha * h + ff, g_ref[...], b_ref[...])


def _ffn(h, w_up, conv_w, conv_b, w_down, ln_g, ln_b, alpha, seq):
    n, dm = h.shape
    dff = w_down.shape[0]
    tm = FFN_TILE
    assert dff % FF_CHUNK == 0 and seq % tm == 0
    assert conv_w.shape[0] == CONV_WIDTH
    cw = jnp.concatenate([conv_w, conv_b[None, :]], axis=0)
    cw = jnp.pad(cw, ((0, SUBLANES - cw.shape[0]), (0, 0)))
    row = lambda i: (i, 0)
    full = lambda i: (0, 0)
    once = pl.Buffered(1)
    return pl.pallas_call(
        functools.partial(_ffn_kernel, alpha=alpha, tiles_per_seq=seq // tm, dff=dff),
        grid=(n // tm,),
        in_specs=[
            pl.BlockSpec((tm, dm), row),
            pl.BlockSpec(w_up.shape, full, pipeline_mode=once),
            pl.BlockSpec(cw.shape, full),
            pl.BlockSpec(w_down.shape, full, pipeline_mode=once),
            pl.BlockSpec((1, dm), full), pl.BlockSpec((1, dm), full),
        ],
        out_specs=pl.BlockSpec((tm, dm), row),
        out_shape=jax.ShapeDtypeStruct((n, dm), F32),
        scratch_shapes=[
            pltpu.VMEM((SUBLANES, 2 * dff), F32),
            pltpu.VMEM((tm, dff), BF16),
        ],
        compiler_params=pltpu.CompilerParams(dimension_semantics=("arbitrary",), vmem_limit_bytes=VMEM_LIMIT),
        name="conv_ffn_ln2",
    )(h, w_up, cw, w_down, ln_g, ln_b)


def kernel(x, w_in, w_attn_proj, ssm_a_re, ssm_a_im, ssm_log_dt, ssm_b_re, ssm_b_im, ssm_c_re, ssm_c_im, ssm_d,
           w_glu, w_ssm_proj, w_out, ln1_g, ln1_b, w_up, conv_w, conv_b, w_down, ln2_g, ln2_b):
    bsz, seq, dm = x.shape
    depth = w_in.shape[0]
    alpha = (2.0 * depth) ** 0.25
    n = bsz * seq
    assert seq % ROW_TILE == 0 and ROW_TILE % MOBA_BLOCK == 0 and seq % SSM_CHUNK == 0
    h = x
    for l in range(depth):
        w_in_b = w_in[l].astype(BF16)
        h2 = h.reshape(n, dm)
        q, k, vt, us, ga, gs, kmean = _in_projection(h2, w_in[l].astype(BF16), seq)
        aw = q.shape[1]
        nb = seq // MOBA_BLOCK
        q3 = q.reshape(bsz, seq, aw)
        sel = _moba_select(q3, kmean.reshape(bsz, nb, aw))
        attn = _moba_attention(q3, k.reshape(bsz, seq, aw), vt.reshape(bsz, nb, *vt.shape[1:]), sel).reshape(n, aw)
        tables = _s5_tables(ssm_a_re[l], ssm_a_im[l], ssm_log_dt[l], ssm_b_re[l], ssm_b_im[l],
                            ssm_c_re[l], ssm_c_im[l], ssm_d[l])
        ut = _s5_input(us, ssm_a_re.shape[1], bsz, seq)
        ss = _s5_post(_s5_scan(ut, tables, bsz, seq), w_glu[l].T.astype(BF16), bsz, seq)
        h1 = _merge(h2, attn, ss, ga, gs, w_attn_proj[l].astype(BF16), w_ssm_proj[l].astype(BF16),
                    w_out[l].astype(BF16), ln1_g[l][None], ln1_b[l][None], alpha)
        h = _ffn(h1, w_up[l].astype(BF16), conv_w[l], conv_b[l], w_down[l].astype(BF16),
                 ln2_g[l][None], ln2_b[l][None], alpha, seq).reshape(bsz, seq, dm)
    return h
```

```python
import functools
import math

import jax
import jax.numpy as jnp
from jax import lax
from jax.experimental import pallas as pl
from jax.experimental.pallas import tpu as pltpu

F32 = jnp.float32
BF16 = jnp.bfloat16

N_HEADS = 8
HEAD_DIM = 64
ROT_DIM = HEAD_DIM // 4
ROPE_THETA = 500000.0
MOBA_BLOCK = 256
MOBA_TOP_K = 3
SSM_GROUP = 16
SSM_STATE = 64
CONV_WIDTH = 3
LN_EPS = 1e-5
NEG_INF = -1e30

LANES = 128
SUBLANES = 8
BF16_ROWS = 16
VMEM_LIMIT = 56 * 1024 * 1024

SSM_CHUNK = 16
S5_GROUPS_PER_STEP = 4
ROW_TILE = 512
IN_TILE = 1024
FFN_TILE = 1024
FF_CHUNK = 256

_NT = (((1,), (1,)), ((), ()))


def _dot(a, b):
    return jnp.dot(a, b, preferred_element_type=F32)


def _dot_nt(a, b):
    return lax.dot_general(a, b, _NT, preferred_element_type=F32)


def _layer_norm(t, g, b):
    mu = jnp.mean(t, axis=-1, keepdims=True)
    d = t - mu
    var = jnp.mean(d * d, axis=-1, keepdims=True)
    return d * lax.rsqrt(var + LN_EPS) * g + b


def _to_step_layout(dst_ref, src, slab_ref):
    chunks, width = dst_ref.shape[0], src.shape[1]
    for j in range(width // LANES):
        slab_ref[j] = src[:, j * LANES:(j + 1) * LANES]
    for s in range(SSM_CHUNK):
        for j in range(width // LANES):
            c0 = s * width + j * LANES
            dst_ref[:, c0:c0 + LANES] = slab_ref[j, pl.ds(s, chunks, stride=SSM_CHUNK), :].astype(dst_ref.dtype)


def _from_step_layout(src_ref, slab_ref):
    chunks, width = src_ref.shape[0], slab_ref.shape[0] * LANES
    for s in range(SSM_CHUNK):
        for j in range(width // LANES):
            c0 = s * width + j * LANES
            slab_ref[j, pl.ds(s, chunks, stride=SSM_CHUNK), :] = src_ref[:, c0:c0 + LANES].astype(slab_ref.dtype)
    return jnp.concatenate([slab_ref[j] for j in range(width // LANES)], axis=1)


def _inproj_kernel(x_ref, w_ref, wvt_ref, cos_ref, s1_ref, s2_ref,
                   q_ref, k_ref, vt_ref, us_ref, ga_ref, gs_ref, km_ref, tmp_ref, *, aw, sw, dm, scale):
    xb = x_ref[...].astype(BF16)
    cos_t, s1_t, s2_t = cos_ref[...], s1_ref[...], s2_ref[...]
    tm = xb.shape[0]

    def rotary(t):
        return t * cos_t + pltpu.roll(t, ROT_DIM // 2, 1) * s1_t + pltpu.roll(t, LANES - ROT_DIM // 2, 1) * s2_t

    q = _dot(xb, w_ref[:, 0:aw])
    for j in range(aw // LANES):
        sl = slice(j * LANES, (j + 1) * LANES)
        q_ref[:, sl] = (rotary(q[:, sl]) * scale).astype(BF16)
    k = _dot(xb, w_ref[:, aw:2 * aw])
    for j in range(aw // LANES):
        sl = slice(j * LANES, (j + 1) * LANES)
        kr = rotary(k[:, sl])
        k_ref[:, sl] = kr.astype(BF16)
        for r in range(tm // MOBA_BLOCK):
            km_ref[r, :, sl] = jnp.mean(kr[r * MOBA_BLOCK:(r + 1) * MOBA_BLOCK], axis=0, keepdims=True)
    vt = _dot_nt(wvt_ref[...], xb)
    for r in range(tm // MOBA_BLOCK):
        for j in range(aw // LANES):
            vt_ref[r, j] = vt[j * LANES:(j + 1) * LANES, r * MOBA_BLOCK:(r + 1) * MOBA_BLOCK].astype(BF16)
    _to_step_layout(us_ref, _dot(xb, w_ref[:, 3 * aw:3 * aw + sw]), tmp_ref)
    c0 = 3 * aw + sw
    ga_ref[...] = jax.nn.sigmoid(_dot(xb, w_ref[:, c0:c0 + dm])).astype(BF16)
    gs_ref[...] = jax.nn.sigmoid(_dot(xb, w_ref[:, c0 + dm:c0 + 2 * dm])).astype(BF16)


def _rotary_tables(seq):
    half = ROT_DIM // 2
    inv_freq = ROPE_THETA ** (-jnp.arange(0, ROT_DIM, 2, dtype=F32) / ROT_DIM)
    ang = jnp.arange(seq, dtype=F32)[:, None] * inv_freq[None, :]
    cos, sin = jnp.cos(ang), jnp.sin(ang)
    ones = jnp.ones((seq, HEAD_DIM - ROT_DIM), F32)
    zeros = jnp.zeros((seq, HEAD_DIM - ROT_DIM), F32)
    zh = jnp.zeros((seq, half), F32)
    cos_h = jnp.concatenate([cos, cos, ones], axis=1)
    s1_h = jnp.concatenate([zh, sin, zeros], axis=1)
    s2_h = jnp.concatenate([-sin, zh, zeros], axis=1)
    rep = LANES // HEAD_DIM
    return tuple(jnp.tile(t, (1, rep)) for t in (cos_h, s1_h, s2_h))


def _in_projection(x2, w_in, seq):
    n, dm = x2.shape
    aw, sw = N_HEADS * HEAD_DIM, w_in.shape[1] - 3 * N_HEADS * HEAD_DIM - 2 * dm
    tm = IN_TILE
    assert seq % tm == 0 and tm % MOBA_BLOCK == 0
    cos_t, s1_t, s2_t = _rotary_tables(seq)
    w_vt = w_in[:, 2 * aw:3 * aw].T
    hp = aw // LANES
    tiles_per_seq = seq // tm
    row = lambda i: (i, 0)
    tab = lambda i: (i % tiles_per_seq, 0)
    L = SSM_CHUNK
    step_rows = lambda i: (i, 0)
    out_shape = (
        jax.ShapeDtypeStruct((n, aw), BF16), jax.ShapeDtypeStruct((n, aw), BF16),
        jax.ShapeDtypeStruct((n // MOBA_BLOCK, hp, LANES, MOBA_BLOCK), BF16),
        jax.ShapeDtypeStruct((n // L, L * sw), BF16),
        jax.ShapeDtypeStruct((n, dm), BF16), jax.ShapeDtypeStruct((n, dm), BF16),
        jax.ShapeDtypeStruct((n // MOBA_BLOCK, 1, aw), F32),
    )
    return pl.pallas_call(
        functools.partial(_inproj_kernel, aw=aw, sw=sw, dm=dm, scale=HEAD_DIM ** -0.5 * math.log2(math.e)),
        grid=(n // tm,),
        in_specs=[
            pl.BlockSpec((tm, dm), row),
            pl.BlockSpec(w_in.shape, lambda i: (0, 0)),
            pl.BlockSpec(w_vt.shape, lambda i: (0, 0)),
            pl.BlockSpec((tm, LANES), tab), pl.BlockSpec((tm, LANES), tab), pl.BlockSpec((tm, LANES), tab),
        ],
        out_specs=(
            pl.BlockSpec((tm, aw), row), pl.BlockSpec((tm, aw), row),
            pl.BlockSpec((tm // MOBA_BLOCK, hp, LANES, MOBA_BLOCK), lambda i: (i, 0, 0, 0)),
            pl.BlockSpec((tm // L, L * sw), step_rows),
            pl.BlockSpec((tm, dm), row), pl.BlockSpec((tm, dm), row),
            pl.BlockSpec((tm // MOBA_BLOCK, 1, aw), lambda i: (i, 0, 0)),
        ),
        out_shape=out_shape,
        scratch_shapes=[pltpu.VMEM((sw // LANES, tm, LANES), F32)],
        compiler_params=pltpu.CompilerParams(dimension_semantics=("parallel",), vmem_limit_bytes=VMEM_LIMIT),
        name="in_projection",
    )(x2, w_in, w_vt, cos_t, s1_t, s2_t)


PAIR = 2 * MOBA_BLOCK
Q_TILE = PAIR
MASKED = 2.0 * NEG_INF
V_ROWS = HEAD_DIM + BF16_ROWS
SEL_ROWS = SUBLANES
SEL_CHUNK = 1024


def _head_split(q2):
    qf = q2.astype(F32)
    lane = lax.broadcasted_iota(jnp.int32, qf.shape, 1)
    return (jnp.where(lane < HEAD_DIM, qf, 0.0).astype(BF16), jnp.where(lane >= HEAD_DIM, qf, 0.0).astype(BF16))


def _select_kernel(q_ref, km_ref, sel_ref, *, nb):
    kmh = _head_split(km_ref[0])
    seq = q_ref.shape[1]
    shape = (nb, SEL_CHUNK)
    blk = lax.broadcasted_iota(jnp.int32, shape, 0)
    for c in range(seq // SEL_CHUNK):
        cols = slice(c * SEL_CHUNK, (c + 1) * SEL_CHUNK)
        own = (lax.broadcasted_iota(jnp.int32, shape, 1) + c * SEL_CHUNK) // MOBA_BLOCK
        q2 = q_ref[0, cols, :]
        for h in range(2):
            g = jnp.where(blk < own, _dot_nt(kmh[h], q2), NEG_INF)
            rows = []
            for _ in range(min(MOBA_TOP_K, nb)):
                mx = jnp.max(g, axis=0, keepdims=True)
                first = jnp.min(jnp.where(g == mx, blk, nb), axis=0, keepdims=True)
                rows.append(jnp.where(first < own[0:1], first, -1))
                g = jnp.where(blk == first, -jnp.inf, g)
            rows.append(jnp.full((SEL_ROWS - len(rows), SEL_CHUNK), -1, jnp.int32))
            sel_ref[0, 0, h, :, cols] = jnp.concatenate(rows, axis=0)


def _moba_select(q, kmean):
    bsz, seq, aw = q.shape
    nb = seq // MOBA_BLOCK
    hp = aw // LANES
    assert seq % SEL_CHUNK == 0
    return pl.pallas_call(
        functools.partial(_select_kernel, nb=nb),
        grid=(bsz, hp),
        in_specs=[
            pl.BlockSpec((1, seq, LANES), lambda b, h: (b, 0, h)),
            pl.BlockSpec((1, nb, LANES), lambda b, h: (b, 0, h)),
        ],
        out_specs=pl.BlockSpec((1, 1, 2, SEL_ROWS, seq), lambda b, h: (b, h, 0, 0, 0)),
        out_shape=jax.ShapeDtypeStruct((bsz, hp, 2, SEL_ROWS, seq), jnp.int32),
        compiler_params=pltpu.CompilerParams(
            dimension_semantics=("parallel", "parallel"), vmem_limit_bytes=VMEM_LIMIT),
        name="moba_select",
    )(q, kmean)


def _attn_kernel(q_ref, k_ref, vt_ref, sel_ref, o_ref,
                 va_ref, vb_ref, s0_ref, s1_ref, p0_ref, p1_ref, cm_ref, al_ref, acc_ref, *, nb):
    i = pl.program_id(2)
    blk = MOBA_BLOCK
    v_refs = (va_ref, vb_ref)
    s_refs = (s0_ref, s1_ref)
    p_refs = (p0_ref, p1_ref)

    @pl.when(i == 0)
    def _():
        ones = jnp.ones((V_ROWS - HEAD_DIM, blk), BF16)

        def fill(j, c):
            for half in range(2):
                vt = vt_ref[0, 2 * j + half, 0]
                cols = slice(half * blk, (half + 1) * blk)
                for h in range(2):
                    v_refs[h][j, 0:HEAD_DIM, cols] = vt[h * HEAD_DIM:(h + 1) * HEAD_DIM]
                    v_refs[h][j, HEAD_DIM:V_ROWS, cols] = ones
            return c
        lax.fori_loop(0, nb // 2, fill, 0)

    head_a = jnp.where(lax.broadcasted_iota(jnp.int32, (1, LANES), 1) < HEAD_DIM, 1.0, 0.0).astype(BF16)
    q2 = q_ref[0]
    qh = (q2 * head_a, q2 * (1.0 - head_a).astype(BF16))
    sel = (sel_ref[0, 0, 0], sel_ref[0, 0, 1])

    def chosen(h, n):
        sv = sel[h]
        hit = sv[0:1] == n
        for r in range(1, MOBA_TOP_K):
            hit = jnp.logical_or(hit, sv[r:r + 1] == n)
        return hit

    def past_bias(pair):
        return [tuple(jnp.where(chosen(h, 2 * pair + half), 0.0, MASKED) for half in range(2)) for h in range(2)]

    key_i = lax.broadcasted_iota(jnp.int32, (blk, blk), 0)
    qry_i = lax.broadcasted_iota(jnp.int32, (blk, blk), 1)
    causal = key_i <= qry_i

    def own_mask(s, h):
        top = jnp.concatenate([jnp.where(causal, s[:blk, :blk], MASKED),
                               jnp.where(chosen(h, 2 * i)[:, blk:], s[:blk, blk:], MASKED)], axis=1)
        bot = jnp.concatenate([jnp.full((blk, blk), MASKED, F32),
                               jnp.where(causal, s[blk:, blk:], MASKED)], axis=1)
        return jnp.concatenate([top, bot], axis=0)

    def stage_a(pair, slot, own):
        kp = k_ref[0, pl.ds(pl.multiple_of(pair * PAIR, PAIR), PAIR), :]
        for h in range(2):
            s = _dot_nt(kp, qh[h])
            if own:
                s = own_mask(s, h)
            s_refs[slot][h] = s
            cm_ref[slot, 2 * h:2 * h + 1, :] = jnp.max(s[:blk], axis=0, keepdims=True)
            cm_ref[slot, 2 * h + 1:2 * h + 2, :] = jnp.max(s[blk:], axis=0, keepdims=True)

    def stage_b(slot, m, bias):
        m_out = []
        for h in range(2):
            cm0 = cm_ref[slot, 2 * h:2 * h + 1, :]
            cm1 = cm_ref[slot, 2 * h + 1:2 * h + 2, :]
            if bias is None:
                m_new = jnp.maximum(m[h], jnp.maximum(cm0, cm1))
                c0 = c1 = m_new
            else:
                b0, b1 = bias[h]
                m_new = jnp.maximum(m[h], jnp.maximum(cm0 + b0, cm1 + b1))
                c0, c1 = m_new - b0, m_new - b1
            al_ref[slot, h:h + 1, :] = jnp.exp2(m[h] - m_new)
            p_refs[slot][h, 0:blk] = jnp.exp2(s_refs[slot][h, 0:blk] - c0).astype(BF16)
            p_refs[slot][h, blk:PAIR] = jnp.exp2(s_refs[slot][h, blk:PAIR] - c1).astype(BF16)
            m_out.append(m_new)
        return tuple(m_out)

    def stage_c(pair, slot):
        for h in range(2):
            pv = _dot(v_refs[h][pair], p_refs[slot][h])
            acc_ref[h] = acc_ref[h] * al_ref[slot, h:h + 1, :] + pv

    def trip(t, slot, m):
        stage_a(t - 1, slot, False)
        m = stage_b(1 - slot, m, past_bias(t - 2))
        stage_c(jnp.where(t == 2, i, t - 3), slot)
        return m

    acc_ref[...] = jnp.zeros_like(acc_ref)
    floor = jnp.full((1, Q_TILE), NEG_INF, F32)
    stage_a(i, 0, True)
    stage_a(0, 1, False)
    m0 = stage_b(0, (floor, floor), None)
    n_list = i + 1

    @pl.when(i == 0)
    def _():
        stage_c(i, 0)

    @pl.when(i > 0)
    def _():
        def double_trip(d, m):
            t = 2 + 2 * d
            return trip(t + 1, 1, trip(t, 0, m))

        m1 = lax.fori_loop(0, (n_list - 2) // 2, double_trip, m0)

        @pl.when(n_list % 2 == 1)
        def _():
            m2 = trip(n_list - 1, 0, m1)
            stage_b(0, m2, past_bias(n_list - 2))
            stage_c(n_list - 3, 1)
            stage_c(n_list - 2, 0)

        @pl.when(n_list % 2 == 0)
        def _():
            stage_b(1, m1, past_bias(n_list - 2))
            stage_c(jnp.where(n_list == 2, i, n_list - 3), 0)
            stage_c(n_list - 2, 1)

    outs = []
    for h in range(2):
        acc = acc_ref[h]
        outs.append(acc[:HEAD_DIM] / acc[HEAD_DIM:HEAD_DIM + 1])
    o_ref[0] = jnp.concatenate(outs, axis=0).T.astype(o_ref.dtype)


def _moba_attention(q, k, vt, sel):
    bsz, seq, aw = q.shape
    nb = seq // MOBA_BLOCK
    hp = aw // LANES
    assert nb % 2 == 0
    return pl.pallas_call(
        functools.partial(_attn_kernel, nb=nb),
        grid=(bsz, hp, nb // 2),
        in_specs=[
            pl.BlockSpec((1, Q_TILE, LANES), lambda b, h, i: (b, i, h)),
            pl.BlockSpec((1, seq, LANES), lambda b, h, i: (b, 0, h)),
            pl.BlockSpec((1, nb, 1, LANES, MOBA_BLOCK), lambda b, h, i: (b, 0, h, 0, 0)),
            pl.BlockSpec((1, 1, 2, SEL_ROWS, Q_TILE), lambda b, h, i: (b, h, 0, 0, i)),
        ],
        out_specs=pl.BlockSpec((1, Q_TILE, LANES), lambda b, h, i: (b, i, h)),
        out_shape=jax.ShapeDtypeStruct((bsz, seq, aw), BF16),
        scratch_shapes=[
            pltpu.VMEM((nb // 2, V_ROWS, PAIR), BF16), pltpu.VMEM((nb // 2, V_ROWS, PAIR), BF16),
            pltpu.VMEM((2, PAIR, Q_TILE), F32), pltpu.VMEM((2, PAIR, Q_TILE), F32),
            pltpu.VMEM((2, PAIR, Q_TILE), BF16), pltpu.VMEM((2, PAIR, Q_TILE), BF16),
            pltpu.VMEM((2, SUBLANES, Q_TILE), F32),
            pltpu.VMEM((2, SUBLANES, Q_TILE), F32),
            pltpu.VMEM((2, V_ROWS, Q_TILE), F32),
        ],
        compiler_params=pltpu.CompilerParams(
            dimension_semantics=("parallel", "parallel", "arbitrary"), vmem_limit_bytes=VMEM_LIMIT),
        name="moba_attention",
    )(q, k, vt, sel)


def _s5_tables(a_re, a_im, log_dt, b_re, b_im, c_re, c_im, d_skip):
    g, p = a_re.shape
    L, hch = SSM_CHUNK, SSM_GROUP
    w = L * hch
    two = lambda t: jnp.concatenate([t, t], axis=-1)
    operands = (
        log_dt.reshape(g, 1, 1),
        two(a_re).reshape(g, 1, 2 * p), two(a_im).reshape(g, 1, 2 * p),
        a_re.reshape(g, p, 1), a_im.reshape(g, p, 1),
        jnp.tile(b_re, (1, 1, L)), jnp.tile(b_im, (1, 1, L)),
        two(c_re), two(c_im),
        jnp.tile(d_skip, (1, L)).reshape(g, 1, w),
    )
    grp = lambda i: (i, 0, 0)
    out_shape = (
        jax.ShapeDtypeStruct((g, w, w), BF16), jax.ShapeDtypeStruct((g, 2 * p, w), BF16),
        jax.ShapeDtypeStruct((g, 2 * p, w), BF16), jax.ShapeDtypeStruct((g, w, 2 * p), BF16),
        jax.ShapeDtypeStruct((g, 2, 2 * p), F32),
    )
    return pl.pallas_call(
        _s5_tables_kernel,
        grid=(g,),
        in_specs=[pl.BlockSpec((1,) + t.shape[1:], grp) for t in operands],
        out_specs=tuple(pl.BlockSpec((1,) + t.shape[1:], grp) for t in out_shape),
        out_shape=out_shape,
        compiler_params=pltpu.CompilerParams(dimension_semantics=("parallel",), vmem_limit_bytes=VMEM_LIMIT),
        name="s5_tables",
    )(*operands)


def _s5_tables_kernel(ldt_ref, ar2_ref, ai2_ref, arc_ref, aic_ref, btr_ref, bti_ref, c2r_ref, c2i_ref, dskip_ref,
                      toept_ref, bpt_ref, bpst_ref, cpt_ref, lam_ref):
    L, hch, p = SSM_CHUNK, SSM_GROUP, SSM_STATE
    w = L * hch
    dt = jnp.exp(ldt_ref[0])

    ar, ai = ar2_ref[0] * dt, ai2_ref[0] * dt
    re_half = lax.broadcasted_iota(jnp.int32, (1, 2 * p), 1) < p

    def lam_pow_rows(e):
        mag, ang = jnp.exp(e * ar), e * ai
        return mag * jnp.cos(ang), mag * jnp.sin(ang)

    c_re = jnp.concatenate([c2r_ref[0]] * L, axis=0)
    c_im = jnp.concatenate([c2i_ref[0]] * L, axis=0)
    x_c = jnp.where(re_half, c_re, -c_im)
    y_c = jnp.where(re_half, -c_im, -c_re)
    step = lax.broadcasted_iota(jnp.int32, (L, 1), 0).astype(F32)

    def per_channel(t):
        return jnp.broadcast_to(t[:, None, :], (L, hch, 2 * p)).reshape(w, 2 * p)

    pr, pi = lam_pow_rows(step + 1.0)
    cpt_ref[0] = (x_c * per_channel(pr) + y_c * per_channel(pi)).astype(cpt_ref.dtype)
    pr, pi = lam_pow_rows(step)
    c_lag = x_c * per_channel(pr) + y_c * per_channel(pi)
    pr, pi = lam_pow_rows(jnp.full((1, 1), float(L), F32))
    lam_ref[0, 0:1, :] = pr
    lam_ref[0, 1:2, :] = jnp.where(re_half, -pi, pi)

    a_r, a_i = arc_ref[0], aic_ref[0]
    mag = jnp.exp(a_r * dt)
    n_re, n_im = mag * jnp.cos(a_i * dt) - 1.0, mag * jnp.sin(a_i * dt)
    den = a_r * a_r + a_i * a_i
    z_re = (n_re * a_r + n_im * a_i) / den
    z_im = (n_im * a_r - n_re * a_i) / den
    b_r, b_i = btr_ref[0], bti_ref[0]
    zb_re = z_re * b_r - z_im * b_i
    zb_im = z_re * b_i + z_im * b_r
    e = (L - 1 - lax.broadcasted_iota(jnp.int32, (1, L), 1)).astype(F32)
    mag, ang = jnp.exp(e * (a_r * dt)), e * (a_i * dt)
    spread = (lax.broadcasted_iota(jnp.int32, (L, w), 1) // hch
              == lax.broadcasted_iota(jnp.int32, (L, w), 0)).astype(F32)
    q_re = jnp.dot(mag * jnp.cos(ang), spread, preferred_element_type=F32, precision=lax.Precision.HIGHEST)
    q_im = jnp.dot(mag * jnp.sin(ang), spread, preferred_element_type=F32, precision=lax.Precision.HIGHEST)
    w_re = q_re * zb_re - q_im * zb_im
    w_im = q_re * zb_im + q_im * zb_re
    bpt_ref[0] = jnp.concatenate([w_re, w_im], axis=0).astype(bpt_ref.dtype)
    bpst_ref[0] = jnp.concatenate([w_im, w_re], axis=0).astype(bpst_ref.dtype)

    k_lag = jnp.dot(c_lag, jnp.concatenate([zb_re, zb_im], axis=0),
                    preferred_element_type=F32, precision=lax.Precision.HIGHEST)
    lane_s = lax.broadcasted_iota(jnp.int32, (w, w), 1) // hch
    toep = jnp.zeros((w, w), F32)
    for s in range(L):
        shifted = k_lag if s == 0 else jnp.concatenate([jnp.zeros((s * hch, w), F32), k_lag[:w - s * hch]], axis=0)
        toep = jnp.where(lane_s == s, shifted, toep)
    diag = lax.broadcasted_iota(jnp.int32, (w, w), 0) == lax.broadcasted_iota(jnp.int32, (w, w), 1)
    toept_ref[0] = (toep + jnp.where(diag, dskip_ref[0], 0.0)).astype(toept_ref.dtype)


def _s5_input_kernel(us_ref, ut_ref):
    ut = us_ref[...].astype(F32).T
    ut_ref[...] = ut.reshape(ut_ref.shape).astype(ut_ref.dtype)


def _s5_input(us, n_groups, bsz, seq):
    L = SSM_CHUNK
    nc = seq // L
    sw = us.shape[1] // L
    assert nc % LANES == 0
    return pl.pallas_call(
        _s5_input_kernel,
        grid=(bsz, L),
        in_specs=[pl.BlockSpec((nc, sw), lambda b, s: (b, s))],
        out_specs=pl.BlockSpec((n_groups, SSM_GROUP, nc), lambda b, s: (0, s, b)),
        out_shape=jax.ShapeDtypeStruct((n_groups, L * SSM_GROUP, bsz * nc), BF16),
        compiler_params=pltpu.CompilerParams(
            dimension_semantics=("parallel", "parallel"), vmem_limit_bytes=VMEM_LIMIT),
        name="s5_input",
    )(us)


def _s5_kernel(ut_ref, toept_ref, bpt_ref, bpst_ref, cpt_ref, lam_ref, yt_ref, s_ref, ss_ref, hp_ref, *, bsz, nc):
    groups = range(ut_ref.shape[0])
    for j in groups:
        ut = ut_ref[j]
        s_ref[j] = _dot(bpt_ref[j], ut).T
        ss_ref[j] = _dot(bpst_ref[j], ut).T
    a = [lam_ref[j, 0:1, :] for j in groups]
    bv = [lam_ref[j, 1:2, :] for j in groups]

    def body(c, carry):
        rows = pl.ds(c, bsz, stride=nc)
        out = []
        for j in groups:
            h, hs = carry[j]
            hp_ref[j, rows, :] = h
            out.append((a[j] * h + bv[j] * hs + s_ref[j, rows, :], a[j] * hs - bv[j] * h + ss_ref[j, rows, :]))
        return tuple(out)

    zero = jnp.zeros((bsz, 2 * SSM_STATE), F32)
    lax.fori_loop(0, nc, body, tuple((zero, zero) for _ in groups), unroll=8)
    for j in groups:
        y = _dot(toept_ref[j], ut_ref[j]) + _dot_nt(cpt_ref[j], hp_ref[j].astype(BF16))
        yt_ref[j] = y.astype(yt_ref.dtype)


def _s5_scan(ut, tables, bsz, seq):
    toept, bpt, bpst, cpt, lam = tables
    g, w, cols = ut.shape
    nc = seq // SSM_CHUNK
    p2 = 2 * SSM_STATE
    gs = S5_GROUPS_PER_STEP
    assert g % gs == 0
    grp = lambda i: (i, 0, 0)
    return pl.pallas_call(
        functools.partial(_s5_kernel, bsz=bsz, nc=nc),
        grid=(g // gs,),
        in_specs=[
            pl.BlockSpec((gs, w, cols), grp), pl.BlockSpec((gs, w, w), grp),
            pl.BlockSpec((gs, p2, w), grp), pl.BlockSpec((gs, p2, w), grp),
            pl.BlockSpec((gs, w, p2), grp), pl.BlockSpec((gs, 2, p2), grp),
        ],
        out_specs=pl.BlockSpec((gs, w, cols), grp),
        out_shape=jax.ShapeDtypeStruct((g, w, cols), BF16),
        scratch_shapes=[pltpu.VMEM((gs, cols, p2), F32), pltpu.VMEM((gs, cols, p2), F32),
                        pltpu.VMEM((gs, cols, p2), F32)],
        compiler_params=pltpu.CompilerParams(dimension_semantics=("parallel",), vmem_limit_bytes=VMEM_LIMIT),
        name="s5_scan",
    )(ut, toept, bpt, bpst, cpt, lam)


def _s5_post_kernel(yt_ref, wglut_ref, ss_ref):
    gh, nc = yt_ref.shape[0] * yt_ref.shape[1], yt_ref.shape[2]
    gt = jax.nn.gelu(yt_ref[...].astype(F32).reshape(gh, nc))
    ssm_t = gt * jax.nn.sigmoid(_dot(wglut_ref[...], gt.astype(BF16)))
    ss_ref[...] = ssm_t.T.astype(ss_ref.dtype)


def _s5_post(yt, w_glu_t, bsz, seq):
    g = yt.shape[0]
    L = SSM_CHUNK
    nc = seq // L
    sw = g * SSM_GROUP
    return pl.pallas_call(
        _s5_post_kernel,
        grid=(bsz, L),
        in_specs=[
            pl.BlockSpec((g, SSM_GROUP, nc), lambda b, s: (0, s, b)),
            pl.BlockSpec(w_glu_t.shape, lambda b, s: (0, 0)),
        ],
        out_specs=pl.BlockSpec((nc, sw), lambda b, s: (b, s)),
        out_shape=jax.ShapeDtypeStruct((bsz * nc, L * sw), BF16),
        compiler_params=pltpu.CompilerParams(
            dimension_semantics=("parallel", "parallel"), vmem_limit_bytes=VMEM_LIMIT),
        name="s5_post",
    )(yt, w_glu_t)


def _merge_kernel(x_ref, attn_ref, ss_ref, ga_ref, gs_ref, wap_ref, wsp_ref, wout_ref, g_ref, b_ref,
                  h_ref, slab_ref, *, alpha):
    ssm = _from_step_layout(ss_ref, slab_ref)
    merged = (ga_ref[...].astype(F32) * _dot(attn_ref[...], wap_ref[...])
              + gs_ref[...].astype(F32) * _dot(ssm.astype(BF16), wsp_ref[...]))
    mix = _dot(merged.astype(BF16), wout_ref[...])
    h_ref[...] = _layer_norm(alpha * x_ref[...] + mix, g_ref[...], b_ref[...])


def _merge(x2, attn, ss, ga, gs, w_ap, w_sp, w_out, ln_g, ln_b, alpha):
    n, dm = x2.shape
    aw = attn.shape[1]
    L = SSM_CHUNK
    sw = ss.shape[1] // L
    tm = ROW_TILE
    row = lambda i: (i, 0)
    full = lambda i: (0, 0)
    return pl.pallas_call(
        functools.partial(_merge_kernel, alpha=alpha),
        grid=(n // tm,),
        in_specs=[
            pl.BlockSpec((tm, dm), row), pl.BlockSpec((tm, aw), row), pl.BlockSpec((tm // L, L * sw), row),
            pl.BlockSpec((tm, dm), row), pl.BlockSpec((tm, dm), row),
            pl.BlockSpec(w_ap.shape, full), pl.BlockSpec(w_sp.shape, full),
            pl.BlockSpec(w_out.shape, full), pl.BlockSpec((1, dm), full), pl.BlockSpec((1, dm), full),
        ],
        out_specs=pl.BlockSpec((tm, dm), row),
        out_shape=jax.ShapeDtypeStruct((n, dm), F32),
        scratch_shapes=[pltpu.VMEM((sw // LANES, tm, LANES), F32)],
        compiler_params=pltpu.CompilerParams(dimension_semantics=("parallel",), vmem_limit_bytes=VMEM_LIMIT),
        name="merge_ln1",
    )(x2, attn, ss, ga, gs, w_ap, w_sp, w_out, ln_g, ln_b)


def _ffn_kernel(h_ref, wup_ref, cw_ref, wd_ref, g_ref, b_ref, o_ref, halo_ref, act_ref,
                *, alpha, tiles_per_seq, dff):
    i = pl.program_id(0)
    h = h_ref[...]
    hb = h.astype(BF16)
    tm = h.shape[0]
    ck = FF_CHUNK
    row8 = lax.broadcasted_iota(jnp.int32, (SUBLANES, ck), 0)

    @pl.when((i % tiles_per_seq) == 0)
    def _():
        halo_ref[...] = jnp.zeros_like(halo_ref)

    def conv(up, c0):
        cols = slice(c0, c0 + ck)
        prev = halo_ref[:, cols]
        halo_ref[:, cols] = up[tm - SUBLANES:]
        p1 = prev[SUBLANES - 1:SUBLANES]
        p2 = prev[SUBLANES - 2:SUBLANES - 1]
        r1 = pltpu.roll(up, 1, 0)
        r2 = pltpu.roll(up, 2, 0)
        d1 = jnp.concatenate([jnp.where(row8 == 0, p1, r1[:SUBLANES]), r1[SUBLANES:]], axis=0)
        d2 = jnp.concatenate([jnp.where(row8 == 0, p2, jnp.where(row8 == 1, p1, r2[:SUBLANES])), r2[SUBLANES:]],
                             axis=0)
        cw = cw_ref[:, cols]
        return cw[3:4] + cw[0:1] * d2 + cw[1:2] * d1 + cw[2:3] * up

    for j in range(dff // ck):
        c0 = j * ck
        val = conv(_dot(hb, wup_ref[:, c0:c0 + ck]), c0)
        gate = conv(_dot(hb, wup_ref[:, dff + c0:dff + c0 + ck]), dff + c0)
        act_ref[:, c0:c0 + ck] = (jax.nn.gelu(gate) * val).astype(BF16)
    ff = _dot(act_ref[...], wd_ref[...])
    o_ref[...] = _layer_norm(alpha * h + ff, g_ref[...], b_ref[...])


def _ffn(h, w_up, conv_w, conv_b, w_down, ln_g, ln_b, alpha, seq):
    n, dm = h.shape
    dff = w_down.shape[0]
    tm = FFN_TILE
    assert dff % FF_CHUNK == 0 and seq % tm == 0
    assert conv_w.shape[0] == CONV_WIDTH
    cw = jnp.concatenate([conv_w, conv_b[None, :]], axis=0)
    cw = jnp.pad(cw, ((0, SUBLANES - cw.shape[0]), (0, 0)))
    row = lambda i: (i, 0)
    full = lambda i: (0, 0)
    once = pl.Buffered(1)
    return pl.pallas_call(
        functools.partial(_ffn_kernel, alpha=alpha, tiles_per_seq=seq // tm, dff=dff),
        grid=(n // tm,),
        in_specs=[
            pl.BlockSpec((tm, dm), row),
            pl.BlockSpec(w_up.shape, full, pipeline_mode=once),
            pl.BlockSpec(cw.shape, full),
            pl.BlockSpec(w_down.shape, full, pipeline_mode=once),
            pl.BlockSpec((1, dm), full), pl.BlockSpec((1, dm), full),
        ],
        out_specs=pl.BlockSpec((tm, dm), row),
        out_shape=jax.ShapeDtypeStruct((n, dm), F32),
        scratch_shapes=[
            pltpu.VMEM((SUBLANES, 2 * dff), F32),
            pltpu.VMEM((tm, dff), BF16),
        ],
        compiler_params=pltpu.CompilerParams(dimension_semantics=("arbitrary",), vmem_limit_bytes=VMEM_LIMIT),
        name="conv_ffn_ln2",
    )(h, w_up, cw, w_down, ln_g, ln_b)


def kernel(x, w_in, w_attn_proj, ssm_a_re, ssm_a_im, ssm_log_dt, ssm_b_re, ssm_b_im, ssm_c_re, ssm_c_im, ssm_d,
           w_glu, w_ssm_proj, w_out, ln1_g, ln1_b, w_up, conv_w, conv_b, w_down, ln2_g, ln2_b):
    bsz, seq, dm = x.shape
    depth = w_in.shape[0]
    alpha = (2.0 * depth) ** 0.25
    n = bsz * seq
    assert seq % ROW_TILE == 0 and ROW_TILE % MOBA_BLOCK == 0 and seq % SSM_CHUNK == 0
    h = x
    for l in range(depth):
        w_in_b = w_in[l].astype(BF16)
        h2 = h.reshape(n, dm)
        q, k, vt, us, ga, gs, kmean = _in_projection(h2, w_in[l].astype(BF16), seq)
        aw = q.shape[1]
        nb = seq // MOBA_BLOCK
        q3 = q.reshape(bsz, seq, aw)
        sel = _moba_select(q3, kmean.reshape(bsz, nb, aw))
        attn = _moba_attention(q3, k.reshape(bsz, seq, aw), vt.reshape(bsz, nb, *vt.shape[1:]), sel).reshape(n, aw)
        tables = _s5_tables(ssm_a_re[l], ssm_a_im[l], ssm_log_dt[l], ssm_b_re[l], ssm_b_im[l],
                            ssm_c_re[l], ssm_c_im[l], ssm_d[l])
        ut = _s5_input(us, ssm_a_re.shape[1], bsz, seq)
        ss = _s5_post(_s5_scan(ut, tables, bsz, seq), w_glu[l].T.astype(BF16), bsz, seq)
        h1 = _merge(h2, attn, ss, ga, gs, w_attn_proj[l].astype(BF16), w_ssm_proj[l].astype(BF16),
                    w_out[l].astype(BF16), ln1_g[l][None], ln1_b[l][None], alpha)
        h = _ffn(h1, w_up[l].astype(BF16), conv_w[l], conv_b[l], w_down[l].astype(BF16),
                 ln2_g[l][None], ln2_b[l][None], alpha, seq).reshape(bsz, seq, dm)
    return h
```

```python
import functools
import math

import jax
import jax.numpy as jnp
from jax import lax
from jax.experimental import pallas as pl
from jax.experimental.pallas import tpu as pltpu

F32 = jnp.float32
BF16 = jnp.bfloat16

N_HEADS = 8
HEAD_DIM = 64
ROT_DIM = HEAD_DIM // 4
ROPE_THETA = 500000.0
MOBA_BLOCK = 256
MOBA_TOP_K = 3
SSM_GROUP = 16
SSM_STATE = 64
CONV_WIDTH = 3
LN_EPS = 1e-5
NEG_INF = -1e30

LANES = 128
SUBLANES = 8
BF16_ROWS = 16
VMEM_LIMIT = 56 * 1024 * 1024

SSM_CHUNK = 16
S5_GROUPS_PER_STEP = 4
ROW_TILE = 512
IN_TILE = 1024
FFN_TILE = 1024
FF_CHUNK = 256

_NT = (((1,), (1,)), ((), ()))


def _dot(a, b):
    return jnp.dot(a, b, preferred_element_type=F32)


def _dot_nt(a, b):
    return lax.dot_general(a, b, _NT, preferred_element_type=F32)


def _layer_norm(t, g, b):
    mu = jnp.mean(t, axis=-1, keepdims=True)
    d = t - mu
    var = jnp.mean(d * d, axis=-1, keepdims=True)
    return d * lax.rsqrt(var + LN_EPS) * g + b


def _to_step_layout(dst_ref, src, slab_ref):
    chunks, width = dst_ref.shape[0], src.shape[1]
    for j in range(width // LANES):
        slab_ref[j] = src[:, j * LANES:(j + 1) * LANES]
    for s in range(SSM_CHUNK):
        for j in range(width // LANES):
            c0 = s * width + j * LANES
            dst_ref[:, c0:c0 + LANES] = slab_ref[j, pl.ds(s, chunks, stride=SSM_CHUNK), :].astype(dst_ref.dtype)


def _from_step_layout(src_ref, slab_ref):
    chunks, width = src_ref.shape[0], slab_ref.shape[0] * LANES
    for s in range(SSM_CHUNK):
        for j in range(width // LANES):
            c0 = s * width + j * LANES
            slab_ref[j, pl.ds(s, chunks, stride=SSM_CHUNK), :] = src_ref[:, c0:c0 + LANES].astype(slab_ref.dtype)
    return jnp.concatenate([slab_ref[j] for j in range(width // LANES)], axis=1)


def _inproj_kernel(x_ref, w_ref, cos_ref, s1_ref, s2_ref,
                   q_ref, k_ref, vt_ref, us_ref, ga_ref, gs_ref, km_ref, tmp_ref, *, aw, sw, dm, scale):
    xb = x_ref[...].astype(BF16)
    cos_t, s1_t, s2_t = cos_ref[...], s1_ref[...], s2_ref[...]
    tm = xb.shape[0]

    def rotary(t):
        return t * cos_t + pltpu.roll(t, ROT_DIM // 2, 1) * s1_t + pltpu.roll(t, LANES - ROT_DIM // 2, 1) * s2_t

    q = _dot(xb, w_ref[:, 0:aw])
    for j in range(aw // LANES):
        sl = slice(j * LANES, (j + 1) * LANES)
        q_ref[:, sl] = (rotary(q[:, sl]) * scale).astype(BF16)
    k = _dot(xb, w_ref[:, aw:2 * aw])
    for j in range(aw // LANES):
        sl = slice(j * LANES, (j + 1) * LANES)
        kr = rotary(k[:, sl])
        k_ref[:, sl] = kr.astype(BF16)
        for r in range(tm // MOBA_BLOCK):
            km_ref[r, :, sl] = jnp.mean(kr[r * MOBA_BLOCK:(r + 1) * MOBA_BLOCK], axis=0, keepdims=True)
    vt = lax.dot_general(w_ref[:, 2 * aw:3 * aw], xb, (((0,), (1,)), ((), ())), preferred_element_type=F32)
    for r in range(tm // MOBA_BLOCK):
        for j in range(aw // LANES):
            vt_ref[r, j] = vt[j * LANES:(j + 1) * LANES, r * MOBA_BLOCK:(r + 1) * MOBA_BLOCK].astype(BF16)
    _to_step_layout(us_ref, _dot(xb, w_ref[:, 3 * aw:3 * aw + sw]), tmp_ref)
    c0 = 3 * aw + sw
    ga_ref[...] = jax.nn.sigmoid(_dot(xb, w_ref[:, c0:c0 + dm])).astype(BF16)
    gs_ref[...] = jax.nn.sigmoid(_dot(xb, w_ref[:, c0 + dm:c0 + 2 * dm])).astype(BF16)


def _rotary_tables(seq):
    half = ROT_DIM // 2
    inv_freq = ROPE_THETA ** (-jnp.arange(0, ROT_DIM, 2, dtype=F32) / ROT_DIM)
    ang = jnp.arange(seq, dtype=F32)[:, None] * inv_freq[None, :]
    cos, sin = jnp.cos(ang), jnp.sin(ang)
    ones = jnp.ones((seq, HEAD_DIM - ROT_DIM), F32)
    zeros = jnp.zeros((seq, HEAD_DIM - ROT_DIM), F32)
    zh = jnp.zeros((seq, half), F32)
    cos_h = jnp.concatenate([cos, cos, ones], axis=1)
    s1_h = jnp.concatenate([zh, sin, zeros], axis=1)
    s2_h = jnp.concatenate([-sin, zh, zeros], axis=1)
    rep = LANES // HEAD_DIM
    return tuple(jnp.tile(t, (1, rep)) for t in (cos_h, s1_h, s2_h))


def _in_projection(x2, w_in, seq):
    n, dm = x2.shape
    aw, sw = N_HEADS * HEAD_DIM, w_in.shape[1] - 3 * N_HEADS * HEAD_DIM - 2 * dm
    tm = IN_TILE
    assert seq % tm == 0 and tm % MOBA_BLOCK == 0
    cos_t, s1_t, s2_t = _rotary_tables(seq)
    hp = aw // LANES
    tiles_per_seq = seq // tm
    row = lambda i: (i, 0)
    tab = lambda i: (i % tiles_per_seq, 0)
    L = SSM_CHUNK
    step_rows = lambda i: (i, 0)
    out_shape = (
        jax.ShapeDtypeStruct((n, aw), BF16), jax.ShapeDtypeStruct((n, aw), BF16),
        jax.ShapeDtypeStruct((n // MOBA_BLOCK, hp, LANES, MOBA_BLOCK), BF16),
        jax.ShapeDtypeStruct((n // L, L * sw), BF16),
        jax.ShapeDtypeStruct((n, dm), BF16), jax.ShapeDtypeStruct((n, dm), BF16),
        jax.ShapeDtypeStruct((n // MOBA_BLOCK, 1, aw), F32),
    )
    return pl.pallas_call(
        functools.partial(_inproj_kernel, aw=aw, sw=sw, dm=dm, scale=HEAD_DIM ** -0.5 * math.log2(math.e)),
        grid=(n // tm,),
        in_specs=[
            pl.BlockSpec((tm, dm), row),
            pl.BlockSpec(w_in.shape, lambda i: (0, 0)),
            pl.BlockSpec((tm, LANES), tab), pl.BlockSpec((tm, LANES), tab), pl.BlockSpec((tm, LANES), tab),
        ],
        out_specs=(
            pl.BlockSpec((tm, aw), row), pl.BlockSpec((tm, aw), row),
            pl.BlockSpec((tm // MOBA_BLOCK, hp, LANES, MOBA_BLOCK), lambda i: (i, 0, 0, 0)),
            pl.BlockSpec((tm // L, L * sw), step_rows),
            pl.BlockSpec((tm, dm), row), pl.BlockSpec((tm, dm), row),
            pl.BlockSpec((tm // MOBA_BLOCK, 1, aw), lambda i: (i, 0, 0)),
        ),
        out_shape=out_shape,
        scratch_shapes=[pltpu.VMEM((sw // LANES, tm, LANES), F32)],
        compiler_params=pltpu.CompilerParams(dimension_semantics=("parallel",), vmem_limit_bytes=VMEM_LIMIT),
        name="in_projection",
    )(x2, w_in, cos_t, s1_t, s2_t)


PAIR = 2 * MOBA_BLOCK
Q_TILE = PAIR
MASKED = 2.0 * NEG_INF
V_ROWS = HEAD_DIM + BF16_ROWS
SEL_ROWS = SUBLANES
SEL_CHUNK = 1024


def _head_split(q2):
    qf = q2.astype(F32)
    lane = lax.broadcasted_iota(jnp.int32, qf.shape, 1)
    return (jnp.where(lane < HEAD_DIM, qf, 0.0).astype(BF16), jnp.where(lane >= HEAD_DIM, qf, 0.0).astype(BF16))


def _select_kernel(q_ref, km_ref, sel_ref, *, nb):
    kmh = _head_split(km_ref[0])
    seq = q_ref.shape[1]
    shape = (nb, SEL_CHUNK)
    blk = lax.broadcasted_iota(jnp.int32, shape, 0)
    for c in range(seq // SEL_CHUNK):
        cols = slice(c * SEL_CHUNK, (c + 1) * SEL_CHUNK)
        own = (lax.broadcasted_iota(jnp.int32, shape, 1) + c * SEL_CHUNK) // MOBA_BLOCK
        q2 = q_ref[0, cols, :]
        for h in range(2):
            g = jnp.where(blk < own, _dot_nt(kmh[h], q2), NEG_INF)
            rows = []
            for _ in range(min(MOBA_TOP_K, nb)):
                mx = jnp.max(g, axis=0, keepdims=True)
                first = jnp.min(jnp.where(g == mx, blk, nb), axis=0, keepdims=True)
                rows.append(jnp.where(first < own[0:1], first, -1))
                g = jnp.where(blk == first, -jnp.inf, g)
            rows.append(jnp.full((SEL_ROWS - len(rows), SEL_CHUNK), -1, jnp.int32))
            sel_ref[0, 0, h, :, cols] = jnp.concatenate(rows, axis=0)


def _moba_select(q, kmean):
    bsz, seq, aw = q.shape
    nb = seq // MOBA_BLOCK
    hp = aw // LANES
    assert seq % SEL_CHUNK == 0
    return pl.pallas_call(
        functools.partial(_select_kernel, nb=nb),
        grid=(bsz, hp),
        in_specs=[
            pl.BlockSpec((1, seq, LANES), lambda b, h: (b, 0, h)),
            pl.BlockSpec((1, nb, LANES), lambda b, h: (b, 0, h)),
        ],
        out_specs=pl.BlockSpec((1, 1, 2, SEL_ROWS, seq), lambda b, h: (b, h, 0, 0, 0)),
        out_shape=jax.ShapeDtypeStruct((bsz, hp, 2, SEL_ROWS, seq), jnp.int32),
        compiler_params=pltpu.CompilerParams(
            dimension_semantics=("parallel", "parallel"), vmem_limit_bytes=VMEM_LIMIT),
        name="moba_select",
    )(q, kmean)


def _attn_kernel(q_ref, k_ref, vt_ref, sel_ref, o_ref,
                 va_ref, vb_ref, s0_ref, s1_ref, p0_ref, p1_ref, cm_ref, al_ref, acc_ref, *, nb):
    i = pl.program_id(2)
    blk = MOBA_BLOCK
    v_refs = (va_ref, vb_ref)
    s_refs = (s0_ref, s1_ref)
    p_refs = (p0_ref, p1_ref)

    @pl.when(i == 0)
    def _():
        ones = jnp.ones((V_ROWS - HEAD_DIM, blk), BF16)

        def fill(j, c):
            for half in range(2):
                vt = vt_ref[0, 2 * j + half, 0]
                cols = slice(half * blk, (half + 1) * blk)
                for h in range(2):
                    v_refs[h][j, 0:HEAD_DIM, cols] = vt[h * HEAD_DIM:(h + 1) * HEAD_DIM]
                    v_refs[h][j, HEAD_DIM:V_ROWS, cols] = ones
            return c
        lax.fori_loop(0, nb // 2, fill, 0)

    head_a = jnp.where(lax.broadcasted_iota(jnp.int32, (1, LANES), 1) < HEAD_DIM, 1.0, 0.0).astype(BF16)
    q2 = q_ref[0]
    qh = (q2 * head_a, q2 * (1.0 - head_a).astype(BF16))
    sel = (sel_ref[0, 0, 0], sel_ref[0, 0, 1])

    def chosen(h, n):
        sv = sel[h]
        hit = sv[0:1] == n
        for r in range(1, MOBA_TOP_K):
            hit = jnp.logical_or(hit, sv[r:r + 1] == n)
        return hit

    def past_bias(pair):
        return [tuple(jnp.where(chosen(h, 2 * pair + half), 0.0, MASKED) for half in range(2)) for h in range(2)]

    key_i = lax.broadcasted_iota(jnp.int32, (blk, blk), 0)
    qry_i = lax.broadcasted_iota(jnp.int32, (blk, blk), 1)
    causal = key_i <= qry_i

    def own_mask(s, h):
        top = jnp.concatenate([jnp.where(causal, s[:blk, :blk], MASKED),
                               jnp.where(chosen(h, 2 * i)[:, blk:], s[:blk, blk:], MASKED)], axis=1)
        bot = jnp.concatenate([jnp.full((blk, blk), MASKED, F32),
                               jnp.where(causal, s[blk:, blk:], MASKED)], axis=1)
        return jnp.concatenate([top, bot], axis=0)

    def stage_a(pair, slot, own):
        kp = k_ref[0, pl.ds(pl.multiple_of(pair * PAIR, PAIR), PAIR), :]
        for h in range(2):
            s = _dot_nt(kp, qh[h])
            if own:
                s = own_mask(s, h)
            s_refs[slot][h] = s
            cm_ref[slot, 2 * h:2 * h + 1, :] = jnp.max(s[:blk], axis=0, keepdims=True)
            cm_ref[slot, 2 * h + 1:2 * h + 2, :] = jnp.max(s[blk:], axis=0, keepdims=True)

    def stage_b(slot, m, bias):
        m_out = []
        for h in range(2):
            cm0 = cm_ref[slot, 2 * h:2 * h + 1, :]
            cm1 = cm_ref[slot, 2 * h + 1:2 * h + 2, :]
            if bias is None:
                m_new = jnp.maximum(m[h], jnp.maximum(cm0, cm1))
                c0 = c1 = m_new
            else:
                b0, b1 = bias[h]
                m_new = jnp.maximum(m[h], jnp.maximum(cm0 + b0, cm1 + b1))
                c0, c1 = m_new - b0, m_new - b1
            al_ref[slot, h:h + 1, :] = jnp.exp2(m[h] - m_new)
            p_refs[slot][h, 0:blk] = jnp.exp2(s_refs[slot][h, 0:blk] - c0).astype(BF16)
            p_refs[slot][h, blk:PAIR] = jnp.exp2(s_refs[slot][h, blk:PAIR] - c1).astype(BF16)
            m_out.append(m_new)
        return tuple(m_out)

    def stage_c(pair, slot):
        for h in range(2):
            pv = _dot(v_refs[h][pair], p_refs[slot][h])
            acc_ref[h] = acc_ref[h] * al_ref[slot, h:h + 1, :] + pv

    def trip(t, slot, m):
        stage_a(t - 1, slot, False)
        m = stage_b(1 - slot, m, past_bias(t - 2))
        stage_c(jnp.where(t == 2, i, t - 3), slot)
        return m

    acc_ref[...] = jnp.zeros_like(acc_ref)
    floor = jnp.full((1, Q_TILE), NEG_INF, F32)
    stage_a(i, 0, True)
    stage_a(0, 1, False)
    m0 = stage_b(0, (floor, floor), None)
    n_list = i + 1

    @pl.when(i == 0)
    def _():
        stage_c(i, 0)

    @pl.when(i > 0)
    def _():
        def double_trip(d, m):
            t = 2 + 2 * d
            return trip(t + 1, 1, trip(t, 0, m))

        m1 = lax.fori_loop(0, (n_list - 2) // 2, double_trip, m0)

        @pl.when(n_list % 2 == 1)
        def _():
            m2 = trip(n_list - 1, 0, m1)
            stage_b(0, m2, past_bias(n_list - 2))
            stage_c(n_list - 3, 1)
            stage_c(n_list - 2, 0)

        @pl.when(n_list % 2 == 0)
        def _():
            stage_b(1, m1, past_bias(n_list - 2))
            stage_c(jnp.where(n_list == 2, i, n_list - 3), 0)
            stage_c(n_list - 2, 1)

    outs = []
    for h in range(2):
        acc = acc_ref[h]
        outs.append(acc[:HEAD_DIM] / acc[HEAD_DIM:HEAD_DIM + 1])
    o_ref[0] = jnp.concatenate(outs, axis=0).T.astype(o_ref.dtype)


def _moba_attention(q, k, vt, sel):
    bsz, seq, aw = q.shape
    nb = seq // MOBA_BLOCK
    hp = aw // LANES
    assert nb % 2 == 0
    return pl.pallas_call(
        functools.partial(_attn_kernel, nb=nb),
        grid=(bsz, hp, nb // 2),
        in_specs=[
            pl.BlockSpec((1, Q_TILE, LANES), lambda b, h, i: (b, i, h)),
            pl.BlockSpec((1, seq, LANES), lambda b, h, i: (b, 0, h)),
            pl.BlockSpec((1, nb, 1, LANES, MOBA_BLOCK), lambda b, h, i: (b, 0, h, 0, 0)),
            pl.BlockSpec((1, 1, 2, SEL_ROWS, Q_TILE), lambda b, h, i: (b, h, 0, 0, i)),
        ],
        out_specs=pl.BlockSpec((1, Q_TILE, LANES), lambda b, h, i: (b, i, h)),
        out_shape=jax.ShapeDtypeStruct((bsz, seq, aw), BF16),
        scratch_shapes=[
            pltpu.VMEM((nb // 2, V_ROWS, PAIR), BF16), pltpu.VMEM((nb // 2, V_ROWS, PAIR), BF16),
            pltpu.VMEM((2, PAIR, Q_TILE), F32), pltpu.VMEM((2, PAIR, Q_TILE), F32),
            pltpu.VMEM((2, PAIR, Q_TILE), BF16), pltpu.VMEM((2, PAIR, Q_TILE), BF16),
            pltpu.VMEM((2, SUBLANES, Q_TILE), F32),
            pltpu.VMEM((2, SUBLANES, Q_TILE), F32),
            pltpu.VMEM((2, V_ROWS, Q_TILE), F32),
        ],
        compiler_params=pltpu.CompilerParams(
            dimension_semantics=("parallel", "parallel", "arbitrary"), vmem_limit_bytes=VMEM_LIMIT),
        name="moba_attention",
    )(q, k, vt, sel)


def _s5_tables(a_re, a_im, log_dt, b_re, b_im, c_re, c_im, d_skip):
    g, p = a_re.shape
    L, hch = SSM_CHUNK, SSM_GROUP
    w = L * hch
    two = lambda t: jnp.concatenate([t, t], axis=-1)
    operands = (
        log_dt.reshape(g, 1, 1),
        two(a_re).reshape(g, 1, 2 * p), two(a_im).reshape(g, 1, 2 * p),
        a_re.reshape(g, p, 1), a_im.reshape(g, p, 1),
        jnp.tile(b_re, (1, 1, L)), jnp.tile(b_im, (1, 1, L)),
        two(c_re), two(c_im),
        jnp.tile(d_skip, (1, L)).reshape(g, 1, w),
    )
    grp = lambda i: (i, 0, 0)
    out_shape = (
        jax.ShapeDtypeStruct((g, w, w), BF16), jax.ShapeDtypeStruct((g, 2 * p, w), BF16),
        jax.ShapeDtypeStruct((g, 2 * p, w), BF16), jax.ShapeDtypeStruct((g, w, 2 * p), BF16),
        jax.ShapeDtypeStruct((g, 2, 2 * p), F32),
    )
    return pl.pallas_call(
        _s5_tables_kernel,
        grid=(g,),
        in_specs=[pl.BlockSpec((1,) + t.shape[1:], grp) for t in operands],
        out_specs=tuple(pl.BlockSpec((1,) + t.shape[1:], grp) for t in out_shape),
        out_shape=out_shape,
        compiler_params=pltpu.CompilerParams(dimension_semantics=("parallel",), vmem_limit_bytes=VMEM_LIMIT),
        name="s5_tables",
    )(*operands)


def _s5_tables_kernel(ldt_ref, ar2_ref, ai2_ref, arc_ref, aic_ref, btr_ref, bti_ref, c2r_ref, c2i_ref, dskip_ref,
                      toept_ref, bpt_ref, bpst_ref, cpt_ref, lam_ref):
    L, hch, p = SSM_CHUNK, SSM_GROUP, SSM_STATE
    w = L * hch
    dt = jnp.exp(ldt_ref[0])

    ar, ai = ar2_ref[0] * dt, ai2_ref[0] * dt
    re_half = lax.broadcasted_iota(jnp.int32, (1, 2 * p), 1) < p

    def lam_pow_rows(e):
        mag, ang = jnp.exp(e * ar), e * ai
        return mag * jnp.cos(ang), mag * jnp.sin(ang)

    c_re = jnp.concatenate([c2r_ref[0]] * L, axis=0)
    c_im = jnp.concatenate([c2i_ref[0]] * L, axis=0)
    x_c = jnp.where(re_half, c_re, -c_im)
    y_c = jnp.where(re_half, -c_im, -c_re)
    step = lax.broadcasted_iota(jnp.int32, (L, 1), 0).astype(F32)

    def per_channel(t):
        return jnp.broadcast_to(t[:, None, :], (L, hch, 2 * p)).reshape(w, 2 * p)

    pr, pi = lam_pow_rows(step + 1.0)
    cpt_ref[0] = (x_c * per_channel(pr) + y_c * per_channel(pi)).astype(cpt_ref.dtype)
    pr, pi = lam_pow_rows(step)
    c_lag = x_c * per_channel(pr) + y_c * per_channel(pi)
    pr, pi = lam_pow_rows(jnp.full((1, 1), float(L), F32))
    lam_ref[0, 0:1, :] = pr
    lam_ref[0, 1:2, :] = jnp.where(re_half, -pi, pi)

    a_r, a_i = arc_ref[0], aic_ref[0]
    mag = jnp.exp(a_r * dt)
    n_re, n_im = mag * jnp.cos(a_i * dt) - 1.0, mag * jnp.sin(a_i * dt)
    den = a_r * a_r + a_i * a_i
    z_re = (n_re * a_r + n_im * a_i) / den
    z_im = (n_im * a_r - n_re * a_i) / den
    b_r, b_i = btr_ref[0], bti_ref[0]
    zb_re = z_re * b_r - z_im * b_i
    zb_im = z_re * b_i + z_im * b_r
    e = (L - 1 - lax.broadcasted_iota(jnp.int32, (1, L), 1)).astype(F32)
    mag, ang = jnp.exp(e * (a_r * dt)), e * (a_i * dt)
    spread = (lax.broadcasted_iota(jnp.int32, (L, w), 1) // hch
              == lax.broadcasted_iota(jnp.int32, (L, w), 0)).astype(F32)
    q_re = jnp.dot(mag * jnp.cos(ang), spread, preferred_element_type=F32, precision=lax.Precision.HIGHEST)
    q_im = jnp.dot(mag * jnp.sin(ang), spread, preferred_element_type=F32, precision=lax.Precision.HIGHEST)
    w_re = q_re * zb_re - q_im * zb_im
    w_im = q_re * zb_im + q_im * zb_re
    bpt_ref[0] = jnp.concatenate([w_re, w_im], axis=0).astype(bpt_ref.dtype)
    bpst_ref[0] = jnp.concatenate([w_im, w_re], axis=0).astype(bpst_ref.dtype)

    k_lag = jnp.dot(c_lag, jnp.concatenate([zb_re, zb_im], axis=0),
                    preferred_element_type=F32, precision=lax.Precision.HIGHEST)
    lane_s = lax.broadcasted_iota(jnp.int32, (w, w), 1) // hch
    toep = jnp.zeros((w, w), F32)
    for s in range(L):
        shifted = k_lag if s == 0 else jnp.concatenate([jnp.zeros((s * hch, w), F32), k_lag[:w - s * hch]], axis=0)
        toep = jnp.where(lane_s == s, shifted, toep)
    diag = lax.broadcasted_iota(jnp.int32, (w, w), 0) == lax.broadcasted_iota(jnp.int32, (w, w), 1)
    toept_ref[0] = (toep + jnp.where(diag, dskip_ref[0], 0.0)).astype(toept_ref.dtype)


def _s5_input_kernel(us_ref, ut_ref):
    ut = us_ref[...].astype(F32).T
    ut_ref[...] = ut.reshape(ut_ref.shape).astype(ut_ref.dtype)


def _s5_input(us, n_groups, bsz, seq):
    L = SSM_CHUNK
    nc = seq // L
    sw = us.shape[1] // L
    assert nc % LANES == 0
    return pl.pallas_call(
        _s5_input_kernel,
        grid=(bsz, L),
        in_specs=[pl.BlockSpec((nc, sw), lambda b, s: (b, s))],
        out_specs=pl.BlockSpec((n_groups, SSM_GROUP, nc), lambda b, s: (0, s, b)),
        out_shape=jax.ShapeDtypeStruct((n_groups, L * SSM_GROUP, bsz * nc), BF16),
        compiler_params=pltpu.CompilerParams(
            dimension_semantics=("parallel", "parallel"), vmem_limit_bytes=VMEM_LIMIT),
        name="s5_input",
    )(us)


def _s5_kernel(ut_ref, toept_ref, bpt_ref, bpst_ref, cpt_ref, lam_ref, yt_ref, s_ref, ss_ref, hp_ref, *, bsz, nc):
    groups = range(ut_ref.shape[0])
    for j in groups:
        ut = ut_ref[j]
        s_ref[j] = _dot(bpt_ref[j], ut).T
        ss_ref[j] = _dot(bpst_ref[j], ut).T
    a = [lam_ref[j, 0:1, :] for j in groups]
    bv = [lam_ref[j, 1:2, :] for j in groups]

    def body(c, carry):
        rows = pl.ds(c, bsz, stride=nc)
        out = []
        for j in groups:
            h, hs = carry[j]
            hp_ref[j, rows, :] = h
            out.append((a[j] * h + bv[j] * hs + s_ref[j, rows, :], a[j] * hs - bv[j] * h + ss_ref[j, rows, :]))
        return tuple(out)

    zero = jnp.zeros((bsz, 2 * SSM_STATE), F32)
    lax.fori_loop(0, nc, body, tuple((zero, zero) for _ in groups), unroll=8)
    for j in groups:
        y = _dot(toept_ref[j], ut_ref[j]) + _dot_nt(cpt_ref[j], hp_ref[j].astype(BF16))
        yt_ref[j] = y.astype(yt_ref.dtype)


def _s5_scan(ut, tables, bsz, seq):
    toept, bpt, bpst, cpt, lam = tables
    g, w, cols = ut.shape
    nc = seq // SSM_CHUNK
    p2 = 2 * SSM_STATE
    gs = S5_GROUPS_PER_STEP
    assert g % gs == 0
    grp = lambda i: (i, 0, 0)
    return pl.pallas_call(
        functools.partial(_s5_kernel, bsz=bsz, nc=nc),
        grid=(g // gs,),
        in_specs=[
            pl.BlockSpec((gs, w, cols), grp), pl.BlockSpec((gs, w, w), grp),
            pl.BlockSpec((gs, p2, w), grp), pl.BlockSpec((gs, p2, w), grp),
            pl.BlockSpec((gs, w, p2), grp), pl.BlockSpec((gs, 2, p2), grp),
        ],
        out_specs=pl.BlockSpec((gs, w, cols), grp),
        out_shape=jax.ShapeDtypeStruct((g, w, cols), BF16),
        scratch_shapes=[pltpu.VMEM((gs, cols, p2), F32), pltpu.VMEM((gs, cols, p2), F32),
                        pltpu.VMEM((gs, cols, p2), F32)],
        compiler_params=pltpu.CompilerParams(dimension_semantics=("parallel",), vmem_limit_bytes=VMEM_LIMIT),
        name="s5_scan",
    )(ut, toept, bpt, bpst, cpt, lam)


def _s5_post_kernel(yt_ref, wglu_ref, ss_ref):
    gh, nc = yt_ref.shape[0] * yt_ref.shape[1], yt_ref.shape[2]
    gt = jax.nn.gelu(yt_ref[...].astype(F32).reshape(gh, nc))
    z_t = lax.dot_general(wglu_ref[...], gt.astype(BF16), (((0,), (0,)), ((), ())), preferred_element_type=F32)
    ssm_t = gt * jax.nn.sigmoid(z_t)
    ss_ref[...] = ssm_t.T.astype(ss_ref.dtype)


def _s5_post(yt, w_glu, bsz, seq):
    g = yt.shape[0]
    L = SSM_CHUNK
    nc = seq // L
    sw = g * SSM_GROUP
    return pl.pallas_call(
        _s5_post_kernel,
        grid=(bsz, L),
        in_specs=[
            pl.BlockSpec((g, SSM_GROUP, nc), lambda b, s: (0, s, b)),
            pl.BlockSpec(w_glu.shape, lambda b, s: (0, 0)),
        ],
        out_specs=pl.BlockSpec((nc, sw), lambda b, s: (b, s)),
        out_shape=jax.ShapeDtypeStruct((bsz * nc, L * sw), BF16),
        compiler_params=pltpu.CompilerParams(
            dimension_semantics=("parallel", "parallel"), vmem_limit_bytes=VMEM_LIMIT),
        name="s5_post",
    )(yt, w_glu)


def _merge_kernel(x_ref, attn_ref, ss_ref, ga_ref, gs_ref, wap_ref, wsp_ref, wout_ref, g_ref, b_ref,
                  h_ref, slab_ref, *, alpha):
    ssm = _from_step_layout(ss_ref, slab_ref)
    merged = (ga_ref[...].astype(F32) * _dot(attn_ref[...], wap_ref[...])
              + gs_ref[...].astype(F32) * _dot(ssm.astype(BF16), wsp_ref[...]))
    mix = _dot(merged.astype(BF16), wout_ref[...])
    h_ref[...] = _layer_norm(alpha * x_ref[...] + mix, g_ref[...], b_ref[...])


def _merge(x2, attn, ss, ga, gs, w_ap, w_sp, w_out, ln_g, ln_b, alpha):
    n, dm = x2.shape
    aw = attn.shape[1]
    L = SSM_CHUNK
    sw = ss.shape[1] // L
    tm = ROW_TILE
    row = lambda i: (i, 0)
    full = lambda i: (0, 0)
    return pl.pallas_call(
        functools.partial(_merge_kernel, alpha=alpha),
        grid=(n // tm,),
        in_specs=[
            pl.BlockSpec((tm, dm), row), pl.BlockSpec((tm, aw), row), pl.BlockSpec((tm // L, L * sw), row),
            pl.BlockSpec((tm, dm), row), pl.BlockSpec((tm, dm), row),
            pl.BlockSpec(w_ap.shape, full), pl.BlockSpec(w_sp.shape, full),
            pl.BlockSpec(w_out.shape, full), pl.BlockSpec((1, dm), full), pl.BlockSpec((1, dm), full),
        ],
        out_specs=pl.BlockSpec((tm, dm), row),
        out_shape=jax.ShapeDtypeStruct((n, dm), F32),
        scratch_shapes=[pltpu.VMEM((sw // LANES, tm, LANES), F32)],
        compiler_params=pltpu.CompilerParams(dimension_semantics=("parallel",), vmem_limit_bytes=VMEM_LIMIT),
        name="merge_ln1",
    )(x2, attn, ss, ga, gs, w_ap, w_sp, w_out, ln_g, ln_b)


def _ffn_kernel(h_ref, wup_ref, cw_ref, wd_ref, g_ref, b_ref, o_ref, halo_ref, act_ref,
                *, alpha, tiles_per_seq, dff):
    i = pl.program_id(0)
    h = h_ref[...]
    hb = h.astype(BF16)
    tm = h.shape[0]
    ck = FF_CHUNK
    row8 = lax.broadcasted_iota(jnp.int32, (SUBLANES, ck), 0)

    @pl.when((i % tiles_per_seq) == 0)
    def _():
        halo_ref[...] = jnp.zeros_like(halo_ref)

    def conv(up, c0):
        cols = slice(c0, c0 + ck)
        prev = halo_ref[:, cols]
        halo_ref[:, cols] = up[tm - SUBLANES:]
        p1 = prev[SUBLANES - 1:SUBLANES]
        p2 = prev[SUBLANES - 2:SUBLANES - 1]
        r1 = pltpu.roll(up, 1, 0)
        r2 = pltpu.roll(up, 2, 0)
        d1 = jnp.concatenate([jnp.where(row8 == 0, p1, r1[:SUBLANES]), r1[SUBLANES:]], axis=0)
        d2 = jnp.concatenate([jnp.where(row8 == 0, p2, jnp.where(row8 == 1, p1, r2[:SUBLANES])), r2[SUBLANES:]],
                             axis=0)
        cw = cw_ref[:, cols]
        return cw[3:4] + cw[0:1] * d2 + cw[1:2] * d1 + cw[2:3] * up

    for j in range(dff // ck):
        c0 = j * ck
        val = conv(_dot(hb, wup_ref[:, c0:c0 + ck]), c0)
        gate = conv(_dot(hb, wup_ref[:, dff + c0:dff + c0 + ck]), dff + c0)
        act_ref[:, c0:c0 + ck] = (jax.nn.gelu(gate) * val).astype(BF16)
    ff = _dot(act_ref[...], wd_ref[...])
    o_ref[...] = _layer_norm(alpha * h + ff, g_ref[...], b_ref[...])


def _ffn(h, w_up, conv_w, conv_b, w_down, ln_g, ln_b, alpha, seq):
    n, dm = h.shape
    dff = w_down.shape[0]
    tm = FFN_TILE
    assert dff % FF_CHUNK == 0 and seq % tm == 0
    assert conv_w.shape[0] == CONV_WIDTH
    cw = jnp.concatenate([conv_w, conv_b[None, :]], axis=0)
    cw = jnp.pad(cw, ((0, SUBLANES - cw.shape[0]), (0, 0)))
    row = lambda i: (i, 0)
    full = lambda i: (0, 0)
    once = pl.Buffered(1)
    return pl.pallas_call(
        functools.partial(_ffn_kernel, alpha=alpha, tiles_per_seq=seq // tm, dff=dff),
        grid=(n // tm,),
        in_specs=[
            pl.BlockSpec((tm, dm), row),
            pl.BlockSpec(w_up.shape, full, pipeline_mode=once),
            pl.BlockSpec(cw.shape, full),
            pl.BlockSpec(w_down.shape, full, pipeline_mode=once),
            pl.BlockSpec((1, dm), full), pl.BlockSpec((1, dm), full),
        ],
        out_specs=pl.BlockSpec((tm, dm), row),
        out_shape=jax.ShapeDtypeStruct((n, dm), F32),
        scratch_shapes=[
            pltpu.VMEM((SUBLANES, 2 * dff), F32),
            pltpu.VMEM((tm, dff), BF16),
        ],
        compiler_params=pltpu.CompilerParams(dimension_semantics=("arbitrary",), vmem_limit_bytes=VMEM_LIMIT),
        name="conv_ffn_ln2",
    )(h, w_up, cw, w_down, ln_g, ln_b)


def kernel(x, w_in, w_attn_proj, ssm_a_re, ssm_a_im, ssm_log_dt, ssm_b_re, ssm_b_im, ssm_c_re, ssm_c_im, ssm_d,
           w_glu, w_ssm_proj, w_out, ln1_g, ln1_b, w_up, conv_w, conv_b, w_down, ln2_g, ln2_b):
    bsz, seq, dm = x.shape
    depth = w_in.shape[0]
    alpha = (2.0 * depth) ** 0.25
    n = bsz * seq
    assert seq % ROW_TILE == 0 and ROW_TILE % MOBA_BLOCK == 0 and seq % SSM_CHUNK == 0
    h = x
    for l in range(depth):
        w_in_b = w_in[l].astype(BF16)
        h2 = h.reshape(n, dm)
        q, k, vt, us, ga, gs, kmean = _in_projection(h2, w_in[l].astype(BF16), seq)
        aw = q.shape[1]
        nb = seq // MOBA_BLOCK
        q3 = q.reshape(bsz, seq, aw)
        sel = _moba_select(q3, kmean.reshape(bsz, nb, aw))
        attn = _moba_attention(q3, k.reshape(bsz, seq, aw), vt.reshape(bsz, nb, *vt.shape[1:]), sel).reshape(n, aw)
        tables = _s5_tables(ssm_a_re[l], ssm_a_im[l], ssm_log_dt[l], ssm_b_re[l], ssm_b_im[l],
                            ssm_c_re[l], ssm_c_im[l], ssm_d[l])
        ut = _s5_input(us, ssm_a_re.shape[1], bsz, seq)
        ss = _s5_post(_s5_scan(ut, tables, bsz, seq), w_glu[l].astype(BF16), bsz, seq)
        h1 = _merge(h2, attn, ss, ga, gs, w_attn_proj[l].astype(BF16), w_ssm_proj[l].astype(BF16),
                    w_out[l].astype(BF16), ln1_g[l][None], ln1_b[l][None], alpha)
        h = _ffn(h1, w_up[l].astype(BF16), conv_w[l], conv_b[l], w_down[l].astype(BF16),
                 ln2_g[l][None], ln2_b[l][None], alpha, seq).reshape(bsz, seq, dm)
    return h
```
